```python
import math
import jax, jax.numpy as jnp
from jax import lax
import numpy as np

D_MODEL = 1024
BATCH = 8
SEQ = 4096
DEPTH = 2

N_A_LAYERS = DEPTH // 2
N_B_LAYERS = DEPTH - N_A_LAYERS
MOBA_HEADS = 16
MOBA_HEAD_DIM = D_MODEL // MOBA_HEADS
MOBA_BLOCK = 256
MOBA_TOP_K = 3
MOBA_Q_CHUNK = 64
DIFF_HEADS = 8
DIFF_QK_DIM = D_MODEL // (2 * DIFF_HEADS)
DIFF_V_DIM = 2 * DIFF_QK_DIM
DIFF_Q_BLOCK = 128
REL_BUCKETS = 32
REL_MAX_DISTANCE = 128
REL_BIAS_COLS = MOBA_HEADS
MOE_GROUPS = 4
MOE_EXPERTS_PER_GROUP = 8
MOE_TOP_E = 2
MOE_D_FF = D_MODEL // 4
PLE_DIM = 256
RMS_EPS = 1e-6

kernel_name = 'yoco_moba_diffattn_hmoe_trunk'


def rms_norm(x, gain):
    xf = x.astype(jnp.float32)
    y = xf * lax.rsqrt(jnp.mean(xf * xf, axis=-1, keepdims=True) + RMS_EPS)
    return (y * gain.astype(jnp.float32)).astype(x.dtype)


def rel_bucket(dist):
    n = jnp.maximum(dist, 0)
    max_exact = REL_BUCKETS // 2
    nf = jnp.maximum(n, max_exact).astype(jnp.float32)
    large = max_exact + (jnp.log(nf / max_exact) / math.log(REL_MAX_DISTANCE / max_exact)
                         * (REL_BUCKETS - max_exact)).astype(jnp.int32)
    large = jnp.minimum(large, REL_BUCKETS - 1)
    return jnp.where(n < max_exact, n, large)


def moba_attention(q, k, v, rel_bias):
    B, S, H, Dh = q.shape
    nb = -(-S // MOBA_BLOCK)
    pad = nb * MOBA_BLOCK - S
    kp = jnp.pad(k, ((0, 0), (0, pad), (0, 0), (0, 0)))
    vp = jnp.pad(v, ((0, 0), (0, pad), (0, 0), (0, 0)))
    kb = kp.reshape(B, nb, MOBA_BLOCK, H, Dh).transpose(0, 3, 1, 2, 4)
    vb = vp.reshape(B, nb, MOBA_BLOCK, H, Dh).transpose(0, 3, 1, 2, 4)
    k_mean = jnp.mean(kb.astype(jnp.float32), axis=3)
    pos = jnp.arange(S)
    cur_blk = pos // MOBA_BLOCK
    gate = jnp.einsum('bshd,bhnd->bhsn', q.astype(jnp.float32), k_mean)
    past = jnp.arange(nb)[None, :] < cur_blk[:, None]
    gate = jnp.where(past, gate, -jnp.inf)
    if nb < MOBA_TOP_K:
        gate = jnp.pad(gate, ((0, 0), (0, 0), (0, 0), (0, MOBA_TOP_K - nb)),
                       constant_values=-jnp.inf)
    _, sel = lax.top_k(gate, MOBA_TOP_K)
    sel_valid = jnp.arange(MOBA_TOP_K)[None, :] < cur_blk[:, None]
    sel = jnp.where(sel_valid[None, None], sel, 0)

    nq = S // MOBA_Q_CHUNK
    q_ch = q.reshape(B * nq, MOBA_Q_CHUNK, H, Dh)
    sel_ch = sel.transpose(0, 2, 1, 3).reshape(B * nq, MOBA_Q_CHUNK, H, MOBA_TOP_K)
    b_ids = jnp.repeat(jnp.arange(B), nq)
    c_ids = jnp.tile(jnp.arange(nq), B)
    bias_tab = rel_bias.astype(jnp.float32)
    scale = Dh ** -0.5
    head_ids = jnp.arange(H)
    blk_off = jnp.arange(MOBA_BLOCK)

    def chunk(args):
        b, c, qi, si = args
        k_b = kb[b]
        v_b = vb[b]
        q_pos = c * MOBA_Q_CHUNK + jnp.arange(MOBA_Q_CHUNK)
        own = (c * MOBA_Q_CHUNK) // MOBA_BLOCK
        k_own = lax.dynamic_index_in_dim(k_b, own, axis=1, keepdims=False)
        v_own = lax.dynamic_index_in_dim(v_b, own, axis=1, keepdims=False)
        d_own = q_pos[:, None] - (own * MOBA_BLOCK + blk_off)[None, :]
        l_own = (jnp.einsum('qhd,hkd->qhk', qi, k_own).astype(jnp.float32) * scale
                 + bias_tab[rel_bucket(d_own)].transpose(0, 2, 1))
        l_own = jnp.where((d_own >= 0)[:, None, :], l_own, -jnp.inf)
        k_sel = k_b[head_ids[None, :, None], si]
        v_sel = v_b[head_ids[None, :, None], si]
        d_sel = q_pos[:, None, None, None] - (si[..., None] * MOBA_BLOCK + blk_off)
        l_sel = (jnp.einsum('qhd,qhjkd->qhjk', qi, k_sel).astype(jnp.float32) * scale
                 + bias_tab[rel_bucket(d_sel), head_ids[None, :, None, None]])
        l_sel = jnp.where((jnp.arange(MOBA_TOP_K) < own)[None, None, :, None], l_sel, -jnp.inf)
        logits = jnp.concatenate(
            [l_own, l_sel.reshape(MOBA_Q_CHUNK, H, MOBA_TOP_K * MOBA_BLOCK)], axis=-1)
        probs = jax.nn.softmax(logits, axis=-1)
        p_own = probs[..., :MOBA_BLOCK]
        p_sel = probs[..., MOBA_BLOCK:].reshape(MOBA_Q_CHUNK, H, MOBA_TOP_K, MOBA_BLOCK)
        out = (jnp.einsum('qhk,hkd->qhd', p_own, v_own)
               + jnp.einsum('qhjk,qhjkd->qhd', p_sel, v_sel))
        return out.astype(q.dtype)

    out = lax.map(chunk, (b_ids, c_ids, q_ch, sel_ch))
    return out.reshape(B, S, H, Dh)


def diff_attention(q, k, v, rel_bias, lam, sub_gain, lam_init):
    B, S = q.shape[0], q.shape[1]
    nq = S // DIFF_Q_BLOCK
    q_bl = q.reshape(B, nq, DIFF_Q_BLOCK, DIFF_HEADS, 2, DIFF_QK_DIM).swapaxes(0, 1)
    key_pos = jnp.arange(S)
    bias_tab = rel_bias.astype(jnp.float32)
    scale = DIFF_QK_DIM ** -0.5

    def block(args):
        c, qb = args
        q_pos = c * DIFF_Q_BLOCK + jnp.arange(DIFF_Q_BLOCK)
        dist = q_pos[:, None] - key_pos[None, :]
        bias = bias_tab[rel_bucket(dist)].reshape(DIFF_Q_BLOCK, S, DIFF_HEADS, 2).transpose(2, 3, 0, 1)
        logits = jnp.einsum('bqhmd,bkhmd->bhmqk', qb, k).astype(jnp.float32) * scale + bias
        logits = jnp.where(dist >= 0, logits, -jnp.inf)
        probs = jax.nn.softmax(logits, axis=-1)
        att = probs[:, :, 0] - lam * probs[:, :, 1]
        return jnp.einsum('bhqk,bkhe->bqhe', att, v).astype(q.dtype)

    out = lax.map(block, (jnp.arange(nq), q_bl))
    out = out.swapaxes(0, 1).reshape(B, S, DIFF_HEADS, DIFF_V_DIM)
    out = rms_norm(out, sub_gain) * (1.0 - lam_init)
    return out.reshape(B, S, D_MODEL)


def hier_moe(x, w_group, b_group, w_router, b_router, w_gate, w_up, w_down):
    B, S, D = x.shape
    xt = x.reshape(B * S, D)
    g_logits = (xt @ w_group + b_group).astype(jnp.float32)
    g_prob = jax.nn.softmax(g_logits, axis=-1)
    g_sel = jnp.argmax(g_logits, axis=-1)
    g_onehot = jax.nn.one_hot(g_sel, MOE_GROUPS, dtype=jnp.float32)
    g_gate = jnp.sum(g_prob * g_onehot, axis=-1)
    e_logits = (xt @ w_router + b_router).astype(jnp.float32).reshape(-1, MOE_GROUPS, MOE_EXPERTS_PER_GROUP)
    e_logits = jnp.take_along_axis(e_logits, g_sel[:, None, None], axis=1)[:, 0]
    e_prob = jax.nn.softmax(e_logits, axis=-1)
    top_v, top_i = lax.top_k(e_prob, MOE_TOP_E)
    top_v = top_v / jnp.sum(top_v, axis=-1, keepdims=True)
    w_in = jnp.sum(jax.nn.one_hot(top_i, MOE_EXPERTS_PER_GROUP, dtype=jnp.float32) * top_v[..., None], axis=1)
    combine = g_onehot[:, :, None] * (g_gate[:, None] * w_in)[:, None, :]
    y = jnp.zeros_like(xt)
    for g in range(MOE_GROUPS):
        hg = jax.nn.silu(jnp.einsum('td,edf->tef', xt, w_gate[g])) * jnp.einsum('td,edf->tef', xt, w_up[g])
        hg = hg * combine[:, g, :, None].astype(hg.dtype)
        y = y + jnp.einsum('tef,efd->td', hg, w_down[g])
    return y.reshape(B, S, D)


def setup_inputs(seed: int = 0) -> dict:
    key = jax.random.key(seed)
    ks = jax.random.split(key, 32)
    f32 = jnp.float32
    D, G, E, F = D_MODEL, MOE_GROUPS, MOE_EXPERTS_PER_GROUP, MOE_D_FF

    def nrm(k, shape, scale):
        return jax.random.normal(k, shape, f32) * scale

    def gain(k, shape):
        return 1.0 + 0.02 * jax.random.normal(k, shape, f32)

    return {
        'x': nrm(ks[0], (BATCH, SEQ, D), 1.0),
        'p': nrm(ks[1], (DEPTH, BATCH, SEQ, PLE_DIM), 1.0),
        'rel_bias': nrm(ks[2], (REL_BUCKETS, REL_BIAS_COLS), 0.2),
        'attn_norm_a': gain(ks[3], (N_A_LAYERS, D)),
        'w_qkv_a': nrm(ks[4], (N_A_LAYERS, D, 3 * D), D ** -0.5),
        'w_o_a': nrm(ks[5], (N_A_LAYERS, D, D), D ** -0.5),
        'kv_norm': gain(ks[6], (D,)),
        'w_kv': nrm(ks[7], (D, 2 * D), D ** -0.5),
        'attn_norm_b': gain(ks[8], (N_B_LAYERS, D)),
        'w_q_b': nrm(ks[9], (N_B_LAYERS, D, D), D ** -0.5),
        'lambda_q1': nrm(ks[10], (N_B_LAYERS, DIFF_QK_DIM), 0.1),
        'lambda_k1': nrm(ks[11], (N_B_LAYERS, DIFF_QK_DIM), 0.1),
        'lambda_q2': nrm(ks[12], (N_B_LAYERS, DIFF_QK_DIM), 0.1),
        'lambda_k2': nrm(ks[13], (N_B_LAYERS, DIFF_QK_DIM), 0.1),
        'subln_b': gain(ks[14], (N_B_LAYERS, DIFF_V_DIM)),
        'w_o_b': nrm(ks[15], (N_B_LAYERS, D, D), D ** -0.5),
        'ffn_norm': gain(ks[16], (DEPTH, D)),
        'w_group': nrm(ks[17], (DEPTH, D, G), D ** -0.5),
        'b_group': nrm(ks[18], (DEPTH, G), 0.01),
        'w_router': nrm(ks[19], (DEPTH, D, G * E), D ** -0.5),
        'b_router': nrm(ks[20], (DEPTH, G * E), 0.01),
        'w_gate': nrm(ks[21], (DEPTH, G, E, D, F), D ** -0.5),
        'w_up': nrm(ks[22], (DEPTH, G, E, D, F), D ** -0.5),
        'w_down': nrm(ks[23], (DEPTH, G, E, F, D), F ** -0.5),
        'ple_norm': gain(ks[24], (DEPTH, D)),
        'w_ple_gate': nrm(ks[25], (DEPTH, D, D), D ** -0.5),
        'w_ple_proj': nrm(ks[26], (DEPTH, PLE_DIM, D), PLE_DIM ** -0.5),
        'final_norm': gain(ks[27], (D,)),
    }


def reference(x, p, rel_bias, attn_norm_a, w_qkv_a, w_o_a, kv_norm, w_kv,
              attn_norm_b, w_q_b, lambda_q1, lambda_k1, lambda_q2, lambda_k2, subln_b, w_o_b,
              ffn_norm, w_group, b_group, w_router, b_router, w_gate, w_up, w_down,
              ple_norm, w_ple_gate, w_ple_proj, final_norm):
    B, S, D = x.shape
    h = x
    k_s = None
    v_s = None
    for i in range(DEPTH):
        if i < N_A_LAYERS:
            a = i
            hn = rms_norm(h, attn_norm_a[a])
            qkv = (hn @ w_qkv_a[a]).reshape(B, S, 3, MOBA_HEADS, MOBA_HEAD_DIM)
            att = moba_attention(qkv[:, :, 0], qkv[:, :, 1], qkv[:, :, 2], rel_bias)
            h = h + att.reshape(B, S, D) @ w_o_a[a]
        else:
            j = i - N_A_LAYERS
            lam_init = 0.8 - 0.6 * math.exp(-0.3 * i)
            lam = (jnp.exp(jnp.sum(lambda_q1[j].astype(jnp.float32) * lambda_k1[j].astype(jnp.float32)))
                   - jnp.exp(jnp.sum(lambda_q2[j].astype(jnp.float32) * lambda_k2[j].astype(jnp.float32)))
                   + lam_init)
            hn = rms_norm(h, attn_norm_b[j])
            q = (hn @ w_q_b[j]).reshape(B, S, DIFF_HEADS, 2, DIFF_QK_DIM)
            att = diff_attention(q, k_s, v_s, rel_bias, lam, subln_b[j], lam_init)
            h = h + att @ w_o_b[j]
        h = h + hier_moe(rms_norm(h, ffn_norm[i]), w_group[i], b_group[i], w_router[i], b_router[i],
                         w_gate[i], w_up[i], w_down[i])
        gate = jax.nn.sigmoid(rms_norm(h, ple_norm[i]) @ w_ple_gate[i])
        h = h + gate * (p[i] @ w_ple_proj[i])
        if i == N_A_LAYERS - 1:
            kv = rms_norm(h, kv_norm) @ w_kv
            k_s = kv[..., :D].reshape(B, S, DIFF_HEADS, 2, DIFF_QK_DIM)
            v_s = kv[..., D:].reshape(B, S, DIFF_HEADS, DIFF_V_DIM)
    return rms_norm(h, final_norm)
```

```python
import functools
import math

import jax
import jax.numpy as jnp
from jax import lax
from jax.experimental import pallas as pl
from jax.experimental.pallas import tpu as pltpu

F32 = jnp.float32
BF16 = jnp.bfloat16

D_MODEL = 1024
DEPTH = 2
N_A_LAYERS = DEPTH // 2
HEAD_DIM = 64
LANES = 128
MOBA_BLOCK = 256
MOBA_TOP_K = 3
REL_BUCKETS = 32
REL_MAX_DISTANCE = 128
MOE_GROUPS = 4
MOE_EXPERTS_PER_GROUP = 8
MOE_EXPERTS = MOE_GROUPS * MOE_EXPERTS_PER_GROUP
MOE_D_FF = D_MODEL // 4
PLE_DIM = 256
RMS_EPS = 1e-6
NEG = -1e30

ROW_TILE = 512
MOE_ROW_TILE = 1024
VMEM_LIMIT = 52 * 1024 * 1024


def _dot(a, b):
    return jnp.dot(a, b, preferred_element_type=F32)


def _dot_nt(a, b):
    return lax.dot_general(a, b, (((1,), (1,)), ((), ())), preferred_element_type=F32)


def _dot_hi(a, b):
    return jnp.dot(a, b, preferred_element_type=F32, precision=lax.Precision.HIGHEST)


def _rms(x, gain):
    y = x * lax.rsqrt(jnp.mean(x * x, axis=-1, keepdims=True) + RMS_EPS)
    return y * gain


def _params(n_axes):
    return pltpu.CompilerParams(dimension_semantics=("arbitrary",) * n_axes,
                                vmem_limit_bytes=VMEM_LIMIT)


def _full(shape):
    nd = len(shape)
    return pl.BlockSpec(shape, lambda *_: (0,) * nd)


def _qkv_kernel(x_ref, g_ref, w_ref, qkv_ref, kmean_ref):
    hn = _rms(x_ref[...], g_ref[...]).astype(BF16)
    for c in range(3):
        y = _dot(hn, w_ref[:, c * D_MODEL:(c + 1) * D_MODEL])
        qkv_ref[:, c * D_MODEL:(c + 1) * D_MODEL] = y.astype(BF16)
        if c == 1:
            nb = y.shape[0] // MOBA_BLOCK
            kmean_ref[...] = jnp.mean(y.reshape(nb, MOBA_BLOCK, D_MODEL), axis=1, keepdims=True)


def _qkv_proj(x2, gain, w_bf16):
    t = x2.shape[0]
    tm = ROW_TILE
    return pl.pallas_call(
        _qkv_kernel,
        grid=(t // tm,),
        in_specs=[pl.BlockSpec((tm, D_MODEL), lambda i: (i, 0)),
                  _full((1, D_MODEL)),
                  _full((D_MODEL, 3 * D_MODEL))],
        out_specs=[pl.BlockSpec((tm, 3 * D_MODEL), lambda i: (i, 0)),
                   pl.BlockSpec((tm // MOBA_BLOCK, 1, D_MODEL), lambda i: (i, 0, 0))],
        out_shape=[jax.ShapeDtypeStruct((t, 3 * D_MODEL), BF16),
                   jax.ShapeDtypeStruct((t // MOBA_BLOCK, 1, D_MODEL), F32)],
        compiler_params=_params(1),
        name="qkv_proj",
    )(x2, gain.reshape(1, D_MODEL), w_bf16)


def _rel_bucket(dist):
    n = jnp.maximum(dist, 0)
    max_exact = REL_BUCKETS // 2
    nf = jnp.maximum(n, max_exact).astype(F32)
    large = max_exact + (jnp.log(nf / max_exact) / math.log(REL_MAX_DISTANCE / max_exact)
                         * (REL_BUCKETS - max_exact)).astype(jnp.int32)
    large = jnp.minimum(large, REL_BUCKETS - 1)
    return jnp.where(n < max_exact, n, large)


def _build_bias(tab_ref, bias_ref, col0):
    tq = MOBA_BLOCK
    rows = 64
    for r0 in range(0, tq, rows):
        q_loc = lax.broadcasted_iota(jnp.int32, (rows, 2 * tq), 0) + r0
        col = lax.broadcasted_iota(jnp.int32, (rows, 2 * tq), 1)
        dist = q_loc + tq - col
        bkt = _rel_bucket(dist)
        for h in range(2):
            acc = jnp.zeros((rows, 2 * tq), F32)
            for i in range(REL_BUCKETS):
                acc = jnp.where(bkt == i, tab_ref[i, col0 + h], acc)
            far = tab_ref[REL_BUCKETS - 1, col0 + h]
            bias_ref[h, r0:r0 + rows, :] = jnp.where(dist >= 0, acc - far, NEG)


def _build_kv(k_ref, v_ref, ka_ref, va_ref, seq, mask_v):
    tq = MOBA_BLOCK
    nb = seq // tq
    lane = lax.broadcasted_iota(jnp.int32, (tq, LANES), 1)
    first = lane < HEAD_DIM
    ka_ref[0, 0:tq, :] = (lane == HEAD_DIM + nb).astype(F32).astype(BF16)
    ka_ref[1, 0:tq, :] = (lane == nb).astype(F32).astype(BF16)
    if mask_v:
        va_ref[0, 0:tq, :] = jnp.zeros((tq, LANES), BF16)
        va_ref[1, 0:tq, :] = jnp.zeros((tq, LANES), BF16)
    else:
        va_ref[0, 0:tq, :] = jnp.zeros((tq, LANES), BF16)

    def body(j, _):
        r = pl.multiple_of(j * tq, tq)
        k = k_ref[0, pl.ds(r, tq), :].astype(F32)
        oh0 = (lane == HEAD_DIM + j).astype(F32)
        oh1 = (lane == j).astype(F32)
        ka_ref[0, pl.ds(r + tq, tq), :] = jnp.where(first, k, oh0).astype(BF16)
        ka_ref[1, pl.ds(r + tq, tq), :] = jnp.where(first, oh1, k).astype(BF16)
        v = v_ref[0, pl.ds(r, tq), :]
        if mask_v:
            vf = v.astype(F32)
            va_ref[0, pl.ds(r + tq, tq), :] = jnp.where(first, vf, 0.0).astype(BF16)
            va_ref[1, pl.ds(r + tq, tq), :] = jnp.where(first, 0.0, vf).astype(BF16)
        else:
            va_ref[0, pl.ds(r + tq, tq), :] = v
        return 0

    lax.fori_loop(0, nb, body, 0)


def _moba_select(gate, n, qi):
    rem = (n >= 0) & (n < qi)
    sel = jnp.zeros(gate.shape, jnp.bool_)
    for _ in range(MOBA_TOP_K):
        gm = jnp.where(rem, gate, -jnp.inf)
        mx = jnp.max(gm, axis=1, keepdims=True)
        cand = rem & (gm == mx)
        idx = jnp.min(jnp.where(cand, n, 1 << 20), axis=1, keepdims=True)
        pick = cand & (n == idx)
        sel = sel | pick
        rem = rem & jnp.logical_not(pick)
    return sel


def _flash(qa, ka_ref, va_ref, bias_ref, qi, v_index):
    tq = MOBA_BLOCK
    r2 = pl.multiple_of(qi * tq, tq)
    state = []
    for h in range(2):
        s = _dot_nt(qa[h], ka_ref[h, pl.ds(r2, 2 * tq), :]) + bias_ref[h]
        m = jnp.max(s, axis=1, keepdims=True)
        p = jnp.exp(s - m)
        l = jnp.sum(p, axis=1, keepdims=True)
        acc = _dot(p.astype(BF16), va_ref[v_index[h], pl.ds(r2, 2 * tq), :])
        state += [m, l, acc]

    def body(j, carry):
        r = pl.multiple_of((j + 1) * tq, tq)
        out = []
        for h in range(2):
            m, l, acc = carry[3 * h:3 * h + 3]
            s = _dot_nt(qa[h], ka_ref[h, pl.ds(r, tq), :])
            m_new = jnp.maximum(m, jnp.max(s, axis=1, keepdims=True))
            alpha = jnp.exp(m - m_new)
            p = jnp.exp(s - m_new)
            l = alpha * l + jnp.sum(p, axis=1, keepdims=True)
            acc = alpha * acc + _dot(p.astype(BF16), va_ref[v_index[h], pl.ds(r, tq), :])
            out += [m_new, l, acc]
        return tuple(out)

    carry = lax.fori_loop(0, jnp.maximum(qi - 1, 0), body, tuple(state))
    return [(carry[2], carry[1]), (carry[5], carry[4])]


def _moba_kernel(tab_ref, q_ref, k_ref, v_ref, km_ref, o_ref, ka_ref, va_ref, bias_ref, *, seq):
    hp, b, qi = pl.program_id(0), pl.program_id(1), pl.program_id(2)
    tq = MOBA_BLOCK
    nb = seq // tq

    @pl.when((b == 0) & (qi == 0))
    def _():
        _build_bias(tab_ref, bias_ref, 2 * hp)

    @pl.when(qi == 0)
    def _():
        _build_kv(k_ref, v_ref, ka_ref, va_ref, seq, mask_v=True)

    q = q_ref[0]
    lane = lax.broadcasted_iota(jnp.int32, (tq, LANES), 1)
    first = lane < HEAD_DIM

    km = km_ref[:, 0, :]
    lane_k = lax.broadcasted_iota(jnp.int32, (nb, LANES), 1)
    km0 = jnp.where(lane_k < HEAD_DIM, km, 0.0)
    km1 = jnp.where(lane_k < HEAD_DIM, 0.0, km)
    pad = jnp.zeros((HEAD_DIM - nb, LANES), F32)
    kmx = jnp.concatenate([km1, pad, km0, pad], axis=0)
    kmx_hi = kmx.astype(BF16)
    kmx_lo = (kmx - kmx_hi.astype(F32)).astype(BF16)
    gate = _dot_nt(q, kmx_hi) + _dot_nt(q, kmx_lo)

    qs = q.astype(F32) * (HEAD_DIM ** -0.5)
    qa = []
    for h in range(2):
        n = lane - (HEAD_DIM if h == 0 else 0)
        attend = _moba_select(gate, n, qi) | (n == qi)
        pen = jnp.where((n >= 0) & (n <= nb) & jnp.logical_not(attend), NEG, 0.0)
        own = first if h == 0 else jnp.logical_not(first)
        qa.append(jnp.where(own, qs, pen).astype(BF16))

    (a0, l0), (a1, l1) = _flash(qa, ka_ref, va_ref, bias_ref, qi, (0, 1))
    o_ref[0] = (a0 / l0 + a1 / l1).astype(o_ref.dtype)


def _diff_kernel(tab_ref, lam_ref, q_ref, k_ref, v_ref, sg_ref, o_ref, ka_ref, va_ref, bias_ref, *, seq, lam_init):
    hd, b, qi = pl.program_id(0), pl.program_id(1), pl.program_id(2)
    tq = MOBA_BLOCK
    nb = seq // tq

    @pl.when((b == 0) & (qi == 0))
    def _():
        _build_bias(tab_ref, bias_ref, 2 * hd)

    @pl.when(qi == 0)
    def _():
        _build_kv(k_ref, v_ref, ka_ref, va_ref, seq, mask_v=False)

    lv = lam_ref[...]
    lam = (jnp.exp(jnp.sum(lv[0:1] * lv[1:2], axis=1, keepdims=True))
           - jnp.exp(jnp.sum(lv[2:3] * lv[3:4], axis=1, keepdims=True)) + lam_init)

    q = q_ref[0]
    lane = lax.broadcasted_iota(jnp.int32, (tq, LANES), 1)
    first = lane < HEAD_DIM
    qs = q.astype(F32) * (HEAD_DIM ** -0.5)
    qa = []
    for h in range(2):
        n = lane - (HEAD_DIM if h == 0 else 0)
        pen = jnp.where(n == nb, NEG, 0.0)
        own = first if h == 0 else jnp.logical_not(first)
        qa.append(jnp.where(own, qs, pen).astype(BF16))

    (a0, l0), (a1, l1) = _flash(qa, ka_ref, va_ref, bias_ref, qi, (0, 0))
    att = a0 / l0 - lam * (a1 / l1)
    o_ref[0] = (_rms(att, sg_ref[...]) * (1.0 - lam_init)).astype(o_ref.dtype)


def _attention(kind, q_src, kv_src, rel_bias, extra, *, batch, seq, q_col, k_col, v_col, lam_init=None):
    tq = MOBA_BLOCK
    nb = seq // tq
    n_hp = D_MODEL // LANES
    grid = (n_hp, batch, nb)
    smem = pl.BlockSpec(memory_space=pltpu.SMEM)
    q_spec = pl.BlockSpec((1, tq, LANES), lambda h, b, i: (b, i, q_col + h))
    k_spec = pl.BlockSpec((1, seq, LANES), lambda h, b, i: (b, 0, k_col + h))
    v_spec = pl.BlockSpec((1, seq, LANES), lambda h, b, i: (b, 0, v_col + h))
    o_spec = pl.BlockSpec((1, tq, LANES), lambda h, b, i: (b, i, h))
    scratch_k = pltpu.VMEM((2, seq + tq, LANES), BF16)
    bias_s = pltpu.VMEM((2, tq, 2 * tq), F32)
    out_shape = jax.ShapeDtypeStruct((batch, seq, D_MODEL), BF16)
    if kind == "moba":
        kmean = extra
        km_spec = pl.BlockSpec((nb, 1, LANES), lambda h, b, i: (b, 0, h))
        return pl.pallas_call(
            functools.partial(_moba_kernel, seq=seq),
            grid=grid,
            in_specs=[smem, q_spec, k_spec, v_spec, km_spec],
            out_specs=o_spec, out_shape=out_shape,
            scratch_shapes=[scratch_k, pltpu.VMEM((2, seq + tq, LANES), BF16), bias_s],
            compiler_params=_params(3), name="moba_attention",
        )(rel_bias, q_src, kv_src, kv_src, kmean)
    lam_rows, sub_gain = extra
    return pl.pallas_call(
        functools.partial(_diff_kernel, seq=seq, lam_init=lam_init),
        grid=grid,
        in_specs=[smem, _full((8, LANES)), q_spec, k_spec, v_spec, _full((1, LANES))],
        out_specs=o_spec, out_shape=out_shape,
        scratch_shapes=[scratch_k, pltpu.VMEM((1, seq + tq, LANES), BF16), bias_s],
        compiler_params=_params(3), name="diff_attention",
    )(rel_bias, lam_rows, q_src, kv_src, kv_src, sub_gain)


def _router(xn, wg_ref, bg_ref, we_ref, be_ref):
    lane = lax.broadcasted_iota(jnp.int32, (xn.shape[0], LANES), 1)
    live = lane < MOE_EXPERTS
    gl = jnp.where(live, _dot_hi(xn, wg_ref[...]) + bg_ref[...], -jnp.inf)
    el = _dot_hi(xn, we_ref[...]) + be_ref[...]
    gmax = jnp.max(gl, axis=1, keepdims=True)
    gsum = jnp.sum(jnp.exp(gl - gmax), axis=1, keepdims=True) / MOE_EXPERTS_PER_GROUP
    g_gate = 1.0 / gsum
    first = jnp.min(jnp.where(gl == gmax, lane, 1 << 20), axis=1, keepdims=True)
    group_shift = MOE_EXPERTS_PER_GROUP.bit_length() - 1
    in_group = jnp.right_shift(lane, group_shift) == jnp.right_shift(first, group_shift)
    em = jnp.where(in_group & live, el, -jnp.inf)
    m1 = jnp.max(em, axis=1, keepdims=True)
    i1 = jnp.min(jnp.where(em == m1, lane, 1 << 20), axis=1, keepdims=True)
    em2 = jnp.where(lane == i1, -jnp.inf, em)
    m2 = jnp.max(em2, axis=1, keepdims=True)
    i2 = jnp.min(jnp.where(em2 == m2, lane, 1 << 20), axis=1, keepdims=True)
    p2 = jnp.exp(m2 - m1)
    w1 = 1.0 / (1.0 + p2)
    w2 = p2 / (1.0 + p2)
    return g_gate * jnp.where(lane == i1, w1, jnp.where(lane == i2, w2, 0.0))


def _attn_out_kernel(a_ref, h_ref, wo_ref, fg_ref, wg_ref, bg_ref, we_ref, be_ref, h1_ref, xn_ref, comb_ref):
    h1 = h_ref[...] + _dot(a_ref[...], wo_ref[...])
    h1_ref[...] = h1
    xn = _rms(h1, fg_ref[...])
    xn_ref[...] = xn.astype(BF16)
    comb_ref[...] = _router(xn, wg_ref, bg_ref, we_ref, be_ref)


def _attn_out(att2, h2, wo_bf16, ffn_gain, wg_x, bg_x, we_x, be_x):
    t = h2.shape[0]
    tm = ROW_TILE
    row = lambda w: pl.BlockSpec((tm, w), lambda i: (i, 0))
    return pl.pallas_call(
        _attn_out_kernel,
        grid=(t // tm,),
        in_specs=[row(D_MODEL), row(D_MODEL), _full((D_MODEL, D_MODEL)), _full((1, D_MODEL)),
                  _full((D_MODEL, LANES)), _full((1, LANES)), _full((D_MODEL, LANES)), _full((1, LANES))],
        out_specs=[row(D_MODEL), row(D_MODEL), row(LANES)],
        out_shape=[jax.ShapeDtypeStruct((t, D_MODEL), F32), jax.ShapeDtypeStruct((t, D_MODEL), BF16),
                   jax.ShapeDtypeStruct((t, LANES), F32)],
        compiler_params=_params(1), name="attn_out_router",
    )(att2, h2, wo_bf16, ffn_gain.reshape(1, D_MODEL), wg_x, bg_x, we_x, be_x)


def _moe_dense_kernel(xn_ref, comb_ref, h_ref, wg_ref, wu_ref, wd_ref, o_ref):
    e = pl.program_id(1)

    @pl.when(e == 0)
    def _():
        o_ref[...] = h_ref[...]

    x = xn_ref[...]
    g = _dot(x, wg_ref[0])
    u = _dot(x, wu_ref[0])
    pick = (lax.broadcasted_iota(jnp.int32, (LANES, MOE_D_FF), 0) == e).astype(F32)
    c = _dot_hi(comb_ref[...], pick)
    hh = (g * jax.nn.sigmoid(g)) * u
    hh = hh * c
    o_ref[...] += _dot(hh.astype(BF16), wd_ref[0])


def _moe_dense(xn, comb, h1, wg, wu, wd):
    t = h1.shape[0]
    tm = MOE_ROW_TILE
    row = lambda w: pl.BlockSpec((tm, w), lambda i, e: (i, 0))
    return pl.pallas_call(
        _moe_dense_kernel,
        grid=(t // tm, MOE_EXPERTS),
        in_specs=[row(D_MODEL), row(LANES), row(D_MODEL),
                  pl.BlockSpec((1, D_MODEL, MOE_D_FF), lambda i, e: (e, 0, 0)),
                  pl.BlockSpec((1, D_MODEL, MOE_D_FF), lambda i, e: (e, 0, 0)),
                  pl.BlockSpec((1, MOE_D_FF, D_MODEL), lambda i, e: (e, 0, 0))],
        out_specs=row(D_MODEL),
        out_shape=jax.ShapeDtypeStruct((t, D_MODEL), F32),
        compiler_params=_params(2), name="moe_dense",
    )(xn, comb, h1, wg, wu, wd)


def _ple_update(h_ref, p_ref, pg_ref, wgate_ref, wproj_ref):
    h = h_ref[...]
    gate = jax.nn.sigmoid(_dot(_rms(h, pg_ref[...]).astype(BF16), wgate_ref[...]))
    return h + gate * _dot(p_ref[...].astype(BF16), wproj_ref[...])


def _ple_mid_kernel(h_ref, p_ref, pg_ref, wgate_ref, wproj_ref, kvg_ref, wkv_ref, qg_ref, wq_ref,
                    h3_ref, kv_ref, q_ref):
    h3 = _ple_update(h_ref, p_ref, pg_ref, wgate_ref, wproj_ref)
    h3_ref[...] = h3
    kn = _rms(h3, kvg_ref[...]).astype(BF16)
    for c in range(2):
        kv_ref[:, c * D_MODEL:(c + 1) * D_MODEL] = _dot(kn, wkv_ref[:, c * D_MODEL:(c + 1) * D_MODEL]).astype(BF16)
    q_ref[...] = _dot(_rms(h3, qg_ref[...]).astype(BF16), wq_ref[...]).astype(BF16)


def _ple_last_kernel(h_ref, p_ref, pg_ref, wgate_ref, wproj_ref, fg_ref, o_ref):
    h3 = _ple_update(h_ref, p_ref, pg_ref, wgate_ref, wproj_ref)
    o_ref[...] = _rms(h3, fg_ref[...])


def _ple_mid(h2, p2, ple_gain, wgate, wproj, kv_gain, wkv, q_gain, wq):
    t = h2.shape[0]
    tm = ROW_TILE
    row = lambda w: pl.BlockSpec((tm, w), lambda i: (i, 0))
    vec = _full((1, D_MODEL))
    return pl.pallas_call(
        _ple_mid_kernel,
        grid=(t // tm,),
        in_specs=[row(D_MODEL), row(PLE_DIM), vec, _full((D_MODEL, D_MODEL)), _full((PLE_DIM, D_MODEL)),
                  vec, _full((D_MODEL, 2 * D_MODEL)), vec, _full((D_MODEL, D_MODEL))],
        out_specs=[row(D_MODEL), row(2 * D_MODEL), row(D_MODEL)],
        out_shape=[jax.ShapeDtypeStruct((t, D_MODEL), F32), jax.ShapeDtypeStruct((t, 2 * D_MODEL), BF16),
                   jax.ShapeDtypeStruct((t, D_MODEL), BF16)],
        compiler_params=_params(1), name="ple_kv_q",
    )(h2, p2, ple_gain.reshape(1, -1), wgate, wproj, kv_gain.reshape(1, -1), wkv, q_gain.reshape(1, -1), wq)


def _ple_last(h2, p2, ple_gain, wgate, wproj, final_gain):
    t = h2.shape[0]
    tm = ROW_TILE
    row = lambda w: pl.BlockSpec((tm, w), lambda i: (i, 0))
    vec = _full((1, D_MODEL))
    return pl.pallas_call(
        _ple_last_kernel,
        grid=(t // tm,),
        in_specs=[row(D_MODEL), row(PLE_DIM), vec, _full((D_MODEL, D_MODEL)), _full((PLE_DIM, D_MODEL)), vec],
        out_specs=row(D_MODEL),
        out_shape=jax.ShapeDtypeStruct((t, D_MODEL), F32),
        compiler_params=_params(1), name="ple_final_norm",
    )(h2, p2, ple_gain.reshape(1, -1), wgate, wproj, final_gain.reshape(1, -1))


def _router_operands(w_group, b_group, w_router, b_router):
    pad = LANES - MOE_EXPERTS
    wg = jnp.pad(jnp.repeat(w_group, MOE_EXPERTS_PER_GROUP, axis=1), ((0, 0), (0, pad)))
    bg = jnp.pad(jnp.repeat(b_group, MOE_EXPERTS_PER_GROUP), (0, pad)).reshape(1, LANES)
    we = jnp.pad(w_router, ((0, 0), (0, pad)))
    be = jnp.pad(b_router, (0, pad)).reshape(1, LANES)
    return wg, bg, we, be


def _moe_layer(att, h, wo, i, ffn_norm, w_group, b_group, w_router, b_router, w_gate, w_up, w_down):
    h1, xn, comb = _attn_out(att, h, wo.astype(BF16), ffn_norm[i],
                             *_router_operands(w_group[i], b_group[i], w_router[i], b_router[i]))
    wg = w_gate[i].reshape(MOE_EXPERTS, D_MODEL, MOE_D_FF).astype(BF16)
    wu = w_up[i].reshape(MOE_EXPERTS, D_MODEL, MOE_D_FF).astype(BF16)
    wd = w_down[i].reshape(MOE_EXPERTS, MOE_D_FF, D_MODEL).astype(BF16)
    return _moe_dense(xn, comb, h1, wg, wu, wd)


def kernel(x, p, rel_bias, attn_norm_a, w_qkv_a, w_o_a, kv_norm, w_kv, attn_norm_b, w_q_b, lambda_q1, lambda_k1,
           lambda_q2, lambda_k2, subln_b, w_o_b, ffn_norm, w_group, b_group, w_router, b_router, w_gate, w_up,
           w_down, ple_norm, w_ple_gate, w_ple_proj, final_norm):
    batch, seq, d = x.shape
    assert d == D_MODEL and seq % MOBA_BLOCK == 0 and seq // MOBA_BLOCK <= 16
    t = batch * seq
    assert t % MOE_ROW_TILE == 0
    n_hp = D_MODEL // LANES
    moe = (ffn_norm, w_group, b_group, w_router, b_router, w_gate, w_up, w_down)

    h = x.reshape(t, d)
    qkv, kmean = _qkv_proj(h, attn_norm_a[0], w_qkv_a[0].astype(BF16))
    qkv3 = qkv.reshape(batch, seq, 3 * d)
    att = _attention("moba", qkv3, qkv3, rel_bias, kmean, batch=batch, seq=seq,
                     q_col=0, k_col=n_hp, v_col=2 * n_hp)
    h = _moe_layer(att.reshape(t, d), h, w_o_a[0], 0, *moe)
    h, kv, q = _ple_mid(h, p[0].reshape(t, PLE_DIM), ple_norm[0], w_ple_gate[0].astype(BF16),
                        w_ple_proj[0].astype(BF16), kv_norm, w_kv.astype(BF16), attn_norm_b[0],
                        w_q_b[0].astype(BF16))

    lam_init = 0.8 - 0.6 * math.exp(-0.3 * 1)
    lam_rows = jnp.pad(jnp.stack([lambda_q1[0], lambda_k1[0], lambda_q2[0], lambda_k2[0]]).astype(F32),
                       ((0, 4), (0, LANES - HEAD_DIM)))
    att = _attention("diff", q.reshape(batch, seq, d), kv.reshape(batch, seq, 2 * d), rel_bias,
                     (lam_rows, subln_b[0].reshape(1, LANES)), batch=batch, seq=seq,
                     q_col=0, k_col=0, v_col=n_hp, lam_init=lam_init)
    h = _moe_layer(att.reshape(t, d), h, w_o_b[0], 1, *moe)
    out = _ple_last(h, p[1].reshape(t, PLE_DIM), ple_norm[1], w_ple_gate[1].astype(BF16),
                    w_ple_proj[1].astype(BF16), final_norm)
    return out.reshape(batch, seq, d)
```

```python
import functools
import math

import jax
import jax.numpy as jnp
from jax import lax
from jax.experimental import pallas as pl
from jax.experimental.pallas import tpu as pltpu

F32 = jnp.float32
BF16 = jnp.bfloat16

D_MODEL = 1024
DEPTH = 2
N_A_LAYERS = DEPTH // 2
HEAD_DIM = 64
LANES = 128
MOBA_BLOCK = 256
MOBA_TOP_K = 3
REL_BUCKETS = 32
REL_MAX_DISTANCE = 128
MOE_GROUPS = 4
MOE_EXPERTS_PER_GROUP = 8
MOE_EXPERTS = MOE_GROUPS * MOE_EXPERTS_PER_GROUP
MOE_D_FF = D_MODEL // 4
PLE_DIM = 256
RMS_EPS = 1e-6
NEG = -1e30
LOG2E = math.log2(math.e)

ROW_TILE = 512
MOE_ROW_TILE = 1024
VMEM_LIMIT = 52 * 1024 * 1024


def _dot(a, b):
    return jnp.dot(a, b, preferred_element_type=F32)


def _dot_hi(a, b):
    return jnp.dot(a, b, preferred_element_type=F32, precision=lax.Precision.HIGHEST)


def _rms(x, gain):
    y = x * lax.rsqrt(jnp.mean(x * x, axis=-1, keepdims=True) + RMS_EPS)
    return y * gain


def _params(n_axes):
    return pltpu.CompilerParams(dimension_semantics=("arbitrary",) * n_axes,
                                vmem_limit_bytes=VMEM_LIMIT)


def _full(shape):
    nd = len(shape)
    return pl.BlockSpec(shape, lambda *_: (0,) * nd)


def _qkv_kernel(x_ref, g_ref, w_ref, qkv_ref, kmean_ref):
    hn = _rms(x_ref[...], g_ref[...]).astype(BF16)
    for c in range(3):
        y = _dot(hn, w_ref[:, c * D_MODEL:(c + 1) * D_MODEL])
        qkv_ref[:, c * D_MODEL:(c + 1) * D_MODEL] = y.astype(BF16)
        if c == 1:
            nb = y.shape[0] // MOBA_BLOCK
            kmean_ref[...] = jnp.mean(y.reshape(nb, MOBA_BLOCK, D_MODEL), axis=1, keepdims=True)


def _qkv_proj(x2, gain, w_bf16):
    t = x2.shape[0]
    tm = ROW_TILE
    return pl.pallas_call(
        _qkv_kernel,
        grid=(t // tm,),
        in_specs=[pl.BlockSpec((tm, D_MODEL), lambda i: (i, 0)),
                  _full((1, D_MODEL)),
                  _full((D_MODEL, 3 * D_MODEL))],
        out_specs=[pl.BlockSpec((tm, 3 * D_MODEL), lambda i: (i, 0)),
                   pl.BlockSpec((tm // MOBA_BLOCK, 1, D_MODEL), lambda i: (i, 0, 0))],
        out_shape=[jax.ShapeDtypeStruct((t, 3 * D_MODEL), BF16),
                   jax.ShapeDtypeStruct((t // MOBA_BLOCK, 1, D_MODEL), F32)],
        compiler_params=_params(1),
        name="qkv_proj",
    )(x2, gain.reshape(1, D_MODEL), w_bf16)


def _rel_bucket(dist):
    n = jnp.maximum(dist, 0)
    max_exact = REL_BUCKETS // 2
    nf = jnp.maximum(n, max_exact).astype(F32)
    large = max_exact + (jnp.log(nf / max_exact) / math.log(REL_MAX_DISTANCE / max_exact)
                         * (REL_BUCKETS - max_exact)).astype(jnp.int32)
    large = jnp.minimum(large, REL_BUCKETS - 1)
    return jnp.where(n < max_exact, n, large)


def _build_bias(tab_ref, bias_ref, col0):
    tq = MOBA_BLOCK
    rows = 64
    for r0 in range(0, 2 * tq, rows):
        key = lax.broadcasted_iota(jnp.int32, (rows, tq), 0) + r0
        qry = lax.broadcasted_iota(jnp.int32, (rows, tq), 1)
        dist = qry + tq - key
        bkt = _rel_bucket(dist)
        for h in range(2):
            acc = jnp.zeros((rows, tq), F32)
            for i in range(REL_BUCKETS):
                acc = jnp.where(bkt == i, tab_ref[i, col0 + h], acc)
            far = tab_ref[REL_BUCKETS - 1, col0 + h]
            bias_ref[h, r0:r0 + rows, :] = jnp.where(dist >= 0, (acc - far) * LOG2E, NEG)


def _build_kv(k_ref, v_ref, ka_ref, vt_ref, seq, mask_v):
    tq = MOBA_BLOCK
    nb = seq // tq
    lane = lax.broadcasted_iota(jnp.int32, (tq, LANES), 1)
    first = lane < HEAD_DIM
    ka_ref[0, 0:tq, :] = (lane == HEAD_DIM + nb).astype(F32).astype(BF16)
    ka_ref[1, 0:tq, :] = (lane == nb).astype(F32).astype(BF16)
    vt_ref[0] = jnp.zeros((LANES, tq), BF16)
    if mask_v:
        vt_ref[nb + 1] = jnp.zeros((LANES, tq), BF16)

    def body(j, _):
        r = pl.multiple_of(j * tq, tq)
        k = k_ref[0, pl.ds(r, tq), :].astype(F32)
        oh0 = (lane == HEAD_DIM + j).astype(F32)
        oh1 = (lane == j).astype(F32)
        ka_ref[0, pl.ds(r + tq, tq), :] = jnp.where(first, k, oh0).astype(BF16)
        ka_ref[1, pl.ds(r + tq, tq), :] = jnp.where(first, oh1, k).astype(BF16)
        vf = v_ref[0, pl.ds(r, tq), :].astype(F32)
        if mask_v:
            vt_ref[j + 1] = jnp.where(first, vf, 0.0).T.astype(BF16)
            vt_ref[nb + 2 + j] = jnp.where(first, 0.0, vf).T.astype(BF16)
        else:
            vt_ref[j + 1] = vf.T.astype(BF16)
        return 0

    lax.fori_loop(0, nb, body, 0)


def _moba_attend(gate, qi):
    n = lax.broadcasted_iota(jnp.int32, gate.shape, 0)
    rem = n < qi
    sel = n == qi
    for _ in range(MOBA_TOP_K):
        gm = jnp.where(rem, gate, -jnp.inf)
        mx = jnp.max(gm, axis=0, keepdims=True)
        cand = rem & (gm == mx)
        idx = jnp.min(jnp.where(cand, n, 1 << 20), axis=0, keepdims=True)
        pick = cand & (n == idx)
        sel = sel | pick
        rem = rem & jnp.logical_not(pick)
    return sel


def _augment_queries(qs_t, pens, nb):
    tq = qs_t.shape[1]
    out = []
    for h in range(2):
        if pens is not None:
            pen16 = pens[h]
        else:
            pen16 = jnp.where(lax.broadcasted_iota(jnp.int32, (16, tq), 0) >= nb, NEG, 0.0)
        tail = [pen16, jnp.full((8, tq), NEG, F32), jnp.zeros((HEAD_DIM - 24, tq), F32)]
        parts = [qs_t[0:HEAD_DIM]] + tail if h == 0 else tail + [qs_t[HEAD_DIM:]]
        out.append(jnp.concatenate(parts, axis=0).astype(BF16))
    return out


def _flash(qa, ka_ref, vt_ref, bias_ref, qi, v_base):
    tq = MOBA_BLOCK

    def scores(h, blk0):
        r = pl.multiple_of(blk0 * tq, tq)
        return _dot(ka_ref[h, pl.ds(r, 2 * tq), :], qa[h])

    def pv(h, blk0, p):
        pb = p.astype(BF16)
        return (_dot(vt_ref[v_base[h] + blk0], pb[0:tq]) + _dot(vt_ref[v_base[h] + blk0 + 1], pb[tq:]))

    state = []
    for h in range(2):
        s = scores(h, qi) + bias_ref[h]
        m = jnp.max(s, axis=0, keepdims=True)
        p = jnp.exp2(s - m)
        state += [m, jnp.sum(p, axis=0, keepdims=True), pv(h, qi, p)]

    def body(t, carry):
        blk0 = qi - 2 * (t + 1)
        out = []
        both = [scores(h, blk0) for h in range(2)]
        for h in range(2):
            m, l, acc = carry[3 * h:3 * h + 3]
            s = both[h]
            m_new = jnp.maximum(m, jnp.max(s, axis=0, keepdims=True))
            alpha = jnp.exp2(m - m_new)
            p = jnp.exp2(s - m_new)
            out += [m_new, alpha * l + jnp.sum(p, axis=0, keepdims=True), alpha * acc + pv(h, blk0, p)]
        return tuple(out)

    carry = lax.fori_loop(0, lax.shift_right_logical(qi, 1), body, tuple(state))
    return [(carry[2], carry[1]), (carry[5], carry[4])]


def _moba_kernel(tab_ref, q_ref, k_ref, v_ref, km_ref, o_ref, ka_ref, vt_ref, bias_ref, *, seq):
    hp, b, qi = pl.program_id(0), pl.program_id(1), pl.program_id(2)
    nb = seq // MOBA_BLOCK

    @pl.when((b == 0) & (qi == 0))
    def _():
        _build_bias(tab_ref, bias_ref, 2 * hp)

    @pl.when(qi == 0)
    def _():
        _build_kv(k_ref, v_ref, ka_ref, vt_ref, seq, mask_v=True)

    q_t = q_ref[0].astype(F32).T
    km = km_ref[:, 0, :]
    lane_k = lax.broadcasted_iota(jnp.int32, (nb, LANES), 1)
    km0 = jnp.where(lane_k < HEAD_DIM, km, 0.0)
    km1 = jnp.where(lane_k < HEAD_DIM, 0.0, km)
    pad = jnp.zeros((HEAD_DIM - nb, LANES), F32)
    kmx = jnp.concatenate([km1, pad, km0, pad], axis=0)
    kmx_hi = kmx.astype(BF16)
    kmx_lo = (kmx - kmx_hi.astype(F32)).astype(BF16)
    q_bf = q_t.astype(BF16)
    gate = _dot(kmx_hi, q_bf) + _dot(kmx_lo, q_bf)
    pens = [jnp.where(_moba_attend(gate[base:base + 16], qi), 0.0, NEG) for base in (HEAD_DIM, 0)]

    qa = _augment_queries(q_t * (HEAD_DIM ** -0.5 * LOG2E), pens, nb)
    (a0, l0), (a1, l1) = _flash(qa, ka_ref, vt_ref, bias_ref, qi, (0, nb + 1))
    o_ref[0] = (a0 / l0 + a1 / l1).T.astype(o_ref.dtype)


def _diff_kernel(tab_ref, lam_ref, q_ref, k_ref, v_ref, sg_ref, o_ref, ka_ref, vt_ref, bias_ref, *, seq, lam_init):
    hd, b, qi = pl.program_id(0), pl.program_id(1), pl.program_id(2)

    @pl.when((b == 0) & (qi == 0))
    def _():
        _build_bias(tab_ref, bias_ref, 2 * hd)

    @pl.when(qi == 0)
    def _():
        _build_kv(k_ref, v_ref, ka_ref, vt_ref, seq, mask_v=False)

    lv = lam_ref[...]
    lam = (jnp.exp(jnp.sum(lv[0:1] * lv[1:2], axis=1, keepdims=True))
           - jnp.exp(jnp.sum(lv[2:3] * lv[3:4], axis=1, keepdims=True)) + lam_init)

    q_t = q_ref[0].astype(F32).T
    qa = _augment_queries(q_t * (HEAD_DIM ** -0.5 * LOG2E), None, seq // MOBA_BLOCK)
    (a0, l0), (a1, l1) = _flash(qa, ka_ref, vt_ref, bias_ref, qi, (0, 0))
    att = a0 / l0 - lam * (a1 / l1)
    y = att * lax.rsqrt(jnp.mean(att * att, axis=0, keepdims=True) + RMS_EPS)
    o_ref[0] = ((y.T * sg_ref[...]) * (1.0 - lam_init)).astype(o_ref.dtype)


def _attention(kind, q_src, kv_src, rel_bias, extra, *, batch, seq, q_col, k_col, v_col, lam_init=None):
    tq = MOBA_BLOCK
    nb = seq // tq
    n_hp = D_MODEL // LANES
    grid = (n_hp, batch, nb)
    smem = pl.BlockSpec(memory_space=pltpu.SMEM)
    q_spec = pl.BlockSpec((1, tq, LANES), lambda h, b, i: (b, i, q_col + h))
    k_spec = pl.BlockSpec((1, seq, LANES), lambda h, b, i: (b, 0, k_col + h))
    v_spec = pl.BlockSpec((1, seq, LANES), lambda h, b, i: (b, 0, v_col + h))
    o_spec = pl.BlockSpec((1, tq, LANES), lambda h, b, i: (b, i, h))
    scratch_k = pltpu.VMEM((2, seq + tq, LANES), BF16)
    bias_s = pltpu.VMEM((2, 2 * tq, tq), F32)
    out_shape = jax.ShapeDtypeStruct((batch, seq, D_MODEL), BF16)
    if kind == "moba":
        kmean = extra
        km_spec = pl.BlockSpec((nb, 1, LANES), lambda h, b, i: (b, 0, h))
        return pl.pallas_call(
            functools.partial(_moba_kernel, seq=seq),
            grid=grid,
            in_specs=[smem, q_spec, k_spec, v_spec, km_spec],
            out_specs=o_spec, out_shape=out_shape,
            scratch_shapes=[scratch_k, pltpu.VMEM((2 * (nb + 1), LANES, tq), BF16), bias_s],
            compiler_params=_params(3), name="moba_attention",
        )(rel_bias, q_src, kv_src, kv_src, kmean)
    lam_rows, sub_gain = extra
    return pl.pallas_call(
        functools.partial(_diff_kernel, seq=seq, lam_init=lam_init),
        grid=grid,
        in_specs=[smem, _full((8, LANES)), q_spec, k_spec, v_spec, _full((1, LANES))],
        out_specs=o_spec, out_shape=out_shape,
        scratch_shapes=[scratch_k, pltpu.VMEM((nb + 1, LANES, tq), BF16), bias_s],
        compiler_params=_params(3), name="diff_attention",
    )(rel_bias, lam_rows, q_src, kv_src, kv_src, sub_gain)


def _router(xn, wg_ref, bg_ref, we_ref, be_ref):
    lane = lax.broadcasted_iota(jnp.int32, (xn.shape[0], LANES), 1)
    live = lane < MOE_EXPERTS
    gl = jnp.where(live, _dot_hi(xn, wg_ref[...]) + bg_ref[...], -jnp.inf)
    el = _dot_hi(xn, we_ref[...]) + be_ref[...]
    gmax = jnp.max(gl, axis=1, keepdims=True)
    gsum = jnp.sum(jnp.exp(gl - gmax), axis=1, keepdims=True) / MOE_EXPERTS_PER_GROUP
    g_gate = 1.0 / gsum
    first = jnp.min(jnp.where(gl == gmax, lane, 1 << 20), axis=1, keepdims=True)
    group_shift = MOE_EXPERTS_PER_GROUP.bit_length() - 1
    in_group = jnp.right_shift(lane, group_shift) == jnp.right_shift(first, group_shift)
    em = jnp.where(in_group & live, el, -jnp.inf)
    m1 = jnp.max(em, axis=1, keepdims=True)
    i1 = jnp.min(jnp.where(em == m1, lane, 1 << 20), axis=1, keepdims=True)
    em2 = jnp.where(lane == i1, -jnp.inf, em)
    m2 = jnp.max(em2, axis=1, keepdims=True)
    i2 = jnp.min(jnp.where(em2 == m2, lane, 1 << 20), axis=1, keepdims=True)
    p2 = jnp.exp(m2 - m1)
    w1 = 1.0 / (1.0 + p2)
    w2 = p2 / (1.0 + p2)
    return g_gate * jnp.where(lane == i1, w1, jnp.where(lane == i2, w2, 0.0))


def _attn_out_kernel(a_ref, h_ref, wo_ref, fg_ref, wg_ref, bg_ref, we_ref, be_ref, h1_ref, xn_ref, comb_ref):
    h1 = h_ref[...] + _dot(a_ref[...], wo_ref[...])
    h1_ref[...] = h1
    xn = _rms(h1, fg_ref[...])
    xn_ref[...] = xn.astype(BF16)
    comb_ref[...] = _router(xn, wg_ref, bg_ref, we_ref, be_ref)


def _attn_out(att2, h2, wo_bf16, ffn_gain, wg_x, bg_x, we_x, be_x):
    t = h2.shape[0]
    tm = ROW_TILE
    row = lambda w: pl.BlockSpec((tm, w), lambda i: (i, 0))
    return pl.pallas_call(
        _attn_out_kernel,
        grid=(t // tm,),
        in_specs=[row(D_MODEL), row(D_MODEL), _full((D_MODEL, D_MODEL)), _full((1, D_MODEL)),
                  _full((D_MODEL, LANES)), _full((1, LANES)), _full((D_MODEL, LANES)), _full((1, LANES))],
        out_specs=[row(D_MODEL), row(D_MODEL), row(LANES)],
        out_shape=[jax.ShapeDtypeStruct((t, D_MODEL), F32), jax.ShapeDtypeStruct((t, D_MODEL), BF16),
                   jax.ShapeDtypeStruct((t, LANES), F32)],
        compiler_params=_params(1), name="attn_out_router",
    )(att2, h2, wo_bf16, ffn_gain.reshape(1, D_MODEL), wg_x, bg_x, we_x, be_x)


def _moe_dense_kernel(xn_ref, comb_ref, h_ref, wg_ref, wu_ref, wd_ref, o_ref):
    e = pl.program_id(1)

    @pl.when(e == 0)
    def _():
        o_ref[...] = h_ref[...]

    x = xn_ref[...]
    g = _dot(x, wg_ref[0])
    u = _dot(x, wu_ref[0])
    pick = (lax.broadcasted_iota(jnp.int32, (LANES, MOE_D_FF), 0) == e).astype(F32)
    c = _dot_hi(comb_ref[...], pick)
    hh = (g * jax.nn.sigmoid(g)) * u
    hh = hh * c
    o_ref[...] += _dot(hh.astype(BF16), wd_ref[0])


def _moe_dense(xn, comb, h1, wg, wu, wd):
    t = h1.shape[0]
    tm = MOE_ROW_TILE
    row = lambda w: pl.BlockSpec((tm, w), lambda i, e: (i, 0))
    return pl.pallas_call(
        _moe_dense_kernel,
        grid=(t // tm, MOE_EXPERTS),
        in_specs=[row(D_MODEL), row(LANES), row(D_MODEL),
                  pl.BlockSpec((1, D_MODEL, MOE_D_FF), lambda i, e: (e, 0, 0)),
                  pl.BlockSpec((1, D_MODEL, MOE_D_FF), lambda i, e: (e, 0, 0)),
                  pl.BlockSpec((1, MOE_D_FF, D_MODEL), lambda i, e: (e, 0, 0))],
        out_specs=row(D_MODEL),
        out_shape=jax.ShapeDtypeStruct((t, D_MODEL), F32),
        compiler_params=_params(2), name="moe_dense",
    )(xn, comb, h1, wg, wu, wd)


def _ple_update(h_ref, p_ref, pg_ref, wgate_ref, wproj_ref):
    h = h_ref[...]
    gate = jax.nn.sigmoid(_dot(_rms(h, pg_ref[...]).astype(BF16), wgate_ref[...]))
    return h + gate * _dot(p_ref[...].astype(BF16), wproj_ref[...])


def _ple_mid_kernel(h_ref, p_ref, pg_ref, wgate_ref, wproj_ref, kvg_ref, wkv_ref, qg_ref, wq_ref,
                    h3_ref, kv_ref, q_ref):
    h3 = _ple_update(h_ref, p_ref, pg_ref, wgate_ref, wproj_ref)
    h3_ref[...] = h3
    kn = _rms(h3, kvg_ref[...]).astype(BF16)
    for c in range(2):
        kv_ref[:, c * D_MODEL:(c + 1) * D_MODEL] = _dot(kn, wkv_ref[:, c * D_MODEL:(c + 1) * D_MODEL]).astype(BF16)
    q_ref[...] = _dot(_rms(h3, qg_ref[...]).astype(BF16), wq_ref[...]).astype(BF16)


def _ple_last_kernel(h_ref, p_ref, pg_ref, wgate_ref, wproj_ref, fg_ref, o_ref):
    h3 = _ple_update(h_ref, p_ref, pg_ref, wgate_ref, wproj_ref)
    o_ref[...] = _rms(h3, fg_ref[...])


def _ple_mid(h2, p2, ple_gain, wgate, wproj, kv_gain, wkv, q_gain, wq):
    t = h2.shape[0]
    tm = ROW_TILE
    row = lambda w: pl.BlockSpec((tm, w), lambda i: (i, 0))
    vec = _full((1, D_MODEL))
    return pl.pallas_call(
        _ple_mid_kernel,
        grid=(t // tm,),
        in_specs=[row(D_MODEL), row(PLE_DIM), vec, _full((D_MODEL, D_MODEL)), _full((PLE_DIM, D_MODEL)),
                  vec, _full((D_MODEL, 2 * D_MODEL)), vec, _full((D_MODEL, D_MODEL))],
        out_specs=[row(D_MODEL), row(2 * D_MODEL), row(D_MODEL)],
        out_shape=[jax.ShapeDtypeStruct((t, D_MODEL), F32), jax.ShapeDtypeStruct((t, 2 * D_MODEL), BF16),
                   jax.ShapeDtypeStruct((t, D_MODEL), BF16)],
        compiler_params=_params(1), name="ple_kv_q",
    )(h2, p2, ple_gain.reshape(1, -1), wgate, wproj, kv_gain.reshape(1, -1), wkv, q_gain.reshape(1, -1), wq)


def _ple_last(h2, p2, ple_gain, wgate, wproj, final_gain):
    t = h2.shape[0]
    tm = ROW_TILE
    row = lambda w: pl.BlockSpec((tm, w), lambda i: (i, 0))
    vec = _full((1, D_MODEL))
    return pl.pallas_call(
        _ple_last_kernel,
        grid=(t // tm,),
        in_specs=[row(D_MODEL), row(PLE_DIM), vec, _full((D_MODEL, D_MODEL)), _full((PLE_DIM, D_MODEL)), vec],
        out_specs=row(D_MODEL),
        out_shape=jax.ShapeDtypeStruct((t, D_MODEL), F32),
        compiler_params=_params(1), name="ple_final_norm",
    )(h2, p2, ple_gain.reshape(1, -1), wgate, wproj, final_gain.reshape(1, -1))


def _router_operands(w_group, b_group, w_router, b_router):
    pad = LANES - MOE_EXPERTS
    wg = jnp.pad(jnp.repeat(w_group, MOE_EXPERTS_PER_GROUP, axis=1), ((0, 0), (0, pad)))
    bg = jnp.pad(jnp.repeat(b_group, MOE_EXPERTS_PER_GROUP), (0, pad)).reshape(1, LANES)
    we = jnp.pad(w_router, ((0, 0), (0, pad)))
    be = jnp.pad(b_router, (0, pad)).reshape(1, LANES)
    return wg, bg, we, be


def _moe_layer(att, h, wo, i, ffn_norm, w_group, b_group, w_router, b_router, w_gate, w_up, w_down):
    h1, xn, comb = _attn_out(att, h, wo.astype(BF16), ffn_norm[i],
                             *_router_operands(w_group[i], b_group[i], w_router[i], b_router[i]))
    wg = w_gate[i].reshape(MOE_EXPERTS, D_MODEL, MOE_D_FF).astype(BF16)
    wu = w_up[i].reshape(MOE_EXPERTS, D_MODEL, MOE_D_FF).astype(BF16)
    wd = w_down[i].reshape(MOE_EXPERTS, MOE_D_FF, D_MODEL).astype(BF16)
    return _moe_dense(xn, comb, h1, wg, wu, wd)


def kernel(x, p, rel_bias, attn_norm_a, w_qkv_a, w_o_a, kv_norm, w_kv, attn_norm_b, w_q_b, lambda_q1, lambda_k1,
           lambda_q2, lambda_k2, subln_b, w_o_b, ffn_norm, w_group, b_group, w_router, b_router, w_gate, w_up,
           w_down, ple_norm, w_ple_gate, w_ple_proj, final_norm):
    batch, seq, d = x.shape
    assert d == D_MODEL and seq % MOBA_BLOCK == 0 and seq // MOBA_BLOCK <= 16
    t = batch * seq
    assert t % MOE_ROW_TILE == 0
    n_hp = D_MODEL // LANES
    moe = (ffn_norm, w_group, b_group, w_router, b_router, w_gate, w_up, w_down)

    h = x.reshape(t, d)
    qkv, kmean = _qkv_proj(h, attn_norm_a[0], w_qkv_a[0].astype(BF16))
    qkv3 = qkv.reshape(batch, seq, 3 * d)
    att = _attention("moba", qkv3, qkv3, rel_bias, kmean, batch=batch, seq=seq,
                     q_col=0, k_col=n_hp, v_col=2 * n_hp)
    h = _moe_layer(att.reshape(t, d), h, w_o_a[0], 0, *moe)
    h, kv, q = _ple_mid(h, p[0].reshape(t, PLE_DIM), ple_norm[0], w_ple_gate[0].astype(BF16),
                        w_ple_proj[0].astype(BF16), kv_norm, w_kv.astype(BF16), attn_norm_b[0],
                        w_q_b[0].astype(BF16))

    lam_init = 0.8 - 0.6 * math.exp(-0.3 * 1)
    lam_rows = jnp.pad(jnp.stack([lambda_q1[0], lambda_k1[0], lambda_q2[0], lambda_k2[0]]).astype(F32),
                       ((0, 4), (0, LANES - HEAD_DIM)))
    att = _attention("diff", q.reshape(batch, seq, d), kv.reshape(batch, seq, 2 * d), rel_bias,
                     (lam_rows, subln_b[0].reshape(1, LANES)), batch=batch, seq=seq,
                     q_col=0, k_col=0, v_col=n_hp, lam_init=lam_init)
    h = _moe_layer(att.reshape(t, d), h, w_o_b[0], 1, *moe)
    out = _ple_last(h, p[1].reshape(t, PLE_DIM), ple_norm[1], w_ple_gate[1].astype(BF16),
                    w_ple_proj[1].astype(BF16), final_norm)
    return out.reshape(batch, seq, d)
```

```python
import functools
import math

import jax
import jax.numpy as jnp
from jax import lax
from jax.experimental import pallas as pl
from jax.experimental.pallas import tpu as pltpu

F32 = jnp.float32
BF16 = jnp.bfloat16

D_MODEL = 1024
DEPTH = 2
N_A_LAYERS = DEPTH // 2
HEAD_DIM = 64
LANES = 128
MOBA_BLOCK = 256
MOBA_TOP_K = 3
REL_BUCKETS = 32
REL_MAX_DISTANCE = 128
MOE_GROUPS = 4
MOE_EXPERTS_PER_GROUP = 8
MOE_EXPERTS = MOE_GROUPS * MOE_EXPERTS_PER_GROUP
MOE_D_FF = D_MODEL // 4
PLE_DIM = 256
RMS_EPS = 1e-6
NEG = -1e30
LOG2E = math.log2(math.e)

ROW_TILE = 512
GROUP_TILE = 512
VMEM_LIMIT = 52 * 1024 * 1024


def _dot(a, b):
    return jnp.dot(a, b, preferred_element_type=F32)


def _dot_hi(a, b):
    return jnp.dot(a, b, preferred_element_type=F32, precision=lax.Precision.HIGHEST)


def _rms(x, gain):
    y = x * lax.rsqrt(jnp.mean(x * x, axis=-1, keepdims=True) + RMS_EPS)
    return y * gain


def _params(n_axes):
    return pltpu.CompilerParams(dimension_semantics=("arbitrary",) * n_axes,
                                vmem_limit_bytes=VMEM_LIMIT)


def _full(shape):
    nd = len(shape)
    return pl.BlockSpec(shape, lambda *_: (0,) * nd)


def _qkv_kernel(x_ref, g_ref, w_ref, qkv_ref, kmean_ref):
    hn = _rms(x_ref[...], g_ref[...]).astype(BF16)
    for c in range(3):
        y = _dot(hn, w_ref[:, c * D_MODEL:(c + 1) * D_MODEL])
        qkv_ref[:, c * D_MODEL:(c + 1) * D_MODEL] = y.astype(BF16)
        if c == 1:
            nb = y.shape[0] // MOBA_BLOCK
            kmean_ref[...] = jnp.mean(y.reshape(nb, MOBA_BLOCK, D_MODEL), axis=1, keepdims=True)


def _qkv_proj(x2, gain, w_bf16):
    t = x2.shape[0]
    tm = ROW_TILE
    return pl.pallas_call(
        _qkv_kernel,
        grid=(t // tm,),
        in_specs=[pl.BlockSpec((tm, D_MODEL), lambda i: (i, 0)),
                  _full((1, D_MODEL)),
                  _full((D_MODEL, 3 * D_MODEL))],
        out_specs=[pl.BlockSpec((tm, 3 * D_MODEL), lambda i: (i, 0)),
                   pl.BlockSpec((tm // MOBA_BLOCK, 1, D_MODEL), lambda i: (i, 0, 0))],
        out_shape=[jax.ShapeDtypeStruct((t, 3 * D_MODEL), BF16),
                   jax.ShapeDtypeStruct((t // MOBA_BLOCK, 1, D_MODEL), F32)],
        compiler_params=_params(1),
        name="qkv_proj",
    )(x2, gain.reshape(1, D_MODEL), w_bf16)


def _rel_bucket(dist):
    n = jnp.maximum(dist, 0)
    max_exact = REL_BUCKETS // 2
    nf = jnp.maximum(n, max_exact).astype(F32)
    large = max_exact + (jnp.log(nf / max_exact) / math.log(REL_MAX_DISTANCE / max_exact)
                         * (REL_BUCKETS - max_exact)).astype(jnp.int32)
    large = jnp.minimum(large, REL_BUCKETS - 1)
    return jnp.where(n < max_exact, n, large)


def _build_bias(tab_ref, bias_ref, col0):
    tq = MOBA_BLOCK
    rows = 64
    for r0 in range(0, 2 * tq, rows):
        key = lax.broadcasted_iota(jnp.int32, (rows, tq), 0) + r0
        qry = lax.broadcasted_iota(jnp.int32, (rows, tq), 1)
        dist = qry + tq - key
        bkt = _rel_bucket(dist)
        for h in range(2):
            acc = jnp.zeros((rows, tq), F32)
            for i in range(REL_BUCKETS):
                acc = jnp.where(bkt == i, tab_ref[i, col0 + h], acc)
            far = tab_ref[REL_BUCKETS - 1, col0 + h]
            bias_ref[h, r0:r0 + rows, :] = jnp.where(dist >= 0, (acc - far) * LOG2E, NEG)


def _build_kv(k_ref, v_ref, ka_ref, vt_ref, seq, mask_v):
    tq = MOBA_BLOCK
    nb = seq // tq
    lane = lax.broadcasted_iota(jnp.int32, (tq, LANES), 1)
    first = lane < HEAD_DIM
    ka_ref[0, 0:tq, :] = (lane == HEAD_DIM + nb).astype(F32).astype(BF16)
    ka_ref[1, 0:tq, :] = (lane == nb).astype(F32).astype(BF16)
    vt_ref[0] = jnp.zeros((LANES, tq), BF16)
    if mask_v:
        vt_ref[nb + 1] = jnp.zeros((LANES, tq), BF16)

    def body(j, _):
        r = pl.multiple_of(j * tq, tq)
        k = k_ref[0, pl.ds(r, tq), :].astype(F32)
        oh0 = (lane == HEAD_DIM + j).astype(F32)
        oh1 = (lane == j).astype(F32)
        ka_ref[0, pl.ds(r + tq, tq), :] = jnp.where(first, k, oh0).astype(BF16)
        ka_ref[1, pl.ds(r + tq, tq), :] = jnp.where(first, oh1, k).astype(BF16)
        vf = v_ref[0, pl.ds(r, tq), :].astype(F32)
        if mask_v:
            vt_ref[j + 1] = jnp.where(first, vf, 0.0).T.astype(BF16)
            vt_ref[nb + 2 + j] = jnp.where(first, 0.0, vf).T.astype(BF16)
        else:
            vt_ref[j + 1] = vf.T.astype(BF16)
        return 0

    lax.fori_loop(0, nb, body, 0)


def _moba_attend(gate, qi):
    n = lax.broadcasted_iota(jnp.int32, gate.shape, 0)
    rem = n < qi
    sel = n == qi
    for _ in range(MOBA_TOP_K):
        gm = jnp.where(rem, gate, -jnp.inf)
        mx = jnp.max(gm, axis=0, keepdims=True)
        cand = rem & (gm == mx)
        idx = jnp.min(jnp.where(cand, n, 1 << 20), axis=0, keepdims=True)
        pick = cand & (n == idx)
        sel = sel | pick
        rem = rem & jnp.logical_not(pick)
    return sel


def _augment_queries(qs_t, pens, nb):
    tq = qs_t.shape[1]
    out = []
    for h in range(2):
        if pens is not None:
            pen16 = pens[h]
        else:
            pen16 = jnp.where(lax.broadcasted_iota(jnp.int32, (16, tq), 0) >= nb, NEG, 0.0)
        tail = [pen16, jnp.full((8, tq), NEG, F32), jnp.zeros((HEAD_DIM - 24, tq), F32)]
        parts = [qs_t[0:HEAD_DIM]] + tail if h == 0 else tail + [qs_t[HEAD_DIM:]]
        out.append(jnp.concatenate(parts, axis=0).astype(BF16))
    return out


def _flash(qa, ka_ref, vt_ref, bias_ref, qi, v_base):
    tq = MOBA_BLOCK

    def scores(h, blk0):
        r = pl.multiple_of(blk0 * tq, tq)
        return _dot(ka_ref[h, pl.ds(r, 2 * tq), :], qa[h])

    def pv(h, blk0, p):
        pb = p.astype(BF16)
        return (_dot(vt_ref[v_base[h] + blk0], pb[0:tq]) + _dot(vt_ref[v_base[h] + blk0 + 1], pb[tq:]))

    state = []
    for h in range(2):
        s = scores(h, qi) + bias_ref[h]
        m = jnp.max(s, axis=0, keepdims=True)
        p = jnp.exp2(s - m)
        state += [m, jnp.sum(p, axis=0, keepdims=True), pv(h, qi, p)]

    def body(t, carry):
        blk0 = qi - 2 * (t + 1)
        out = []
        both = [scores(h, blk0) for h in range(2)]
        for h in range(2):
            m, l, acc = carry[3 * h:3 * h + 3]
            s = both[h]
            m_new = jnp.maximum(m, jnp.max(s, axis=0, keepdims=True))
            alpha = jnp.exp2(m - m_new)
            p = jnp.exp2(s - m_new)
            out += [m_new, alpha * l + jnp.sum(p, axis=0, keepdims=True), alpha * acc + pv(h, blk0, p)]
        return tuple(out)

    carry = lax.fori_loop(0, lax.shift_right_logical(qi, 1), body, tuple(state))
    return [(carry[2], carry[1]), (carry[5], carry[4])]


def _moba_kernel(tab_ref, q_ref, k_ref, v_ref, km_ref, o_ref, ka_ref, vt_ref, bias_ref, *, seq):
    hp, b, qi = pl.program_id(0), pl.program_id(1), pl.program_id(2)
    nb = seq // MOBA_BLOCK

    @pl.when((b == 0) & (qi == 0))
    def _():
        _build_bias(tab_ref, bias_ref, 2 * hp)

    @pl.when(qi == 0)
    def _():
        _build_kv(k_ref, v_ref, ka_ref, vt_ref, seq, mask_v=True)

    q_t = q_ref[0].astype(F32).T
    km = km_ref[:, 0, :]
    lane_k = lax.broadcasted_iota(jnp.int32, (nb, LANES), 1)
    km0 = jnp.where(lane_k < HEAD_DIM, km, 0.0)
    km1 = jnp.where(lane_k < HEAD_DIM, 0.0, km)
    pad = jnp.zeros((HEAD_DIM - nb, LANES), F32)
    kmx = jnp.concatenate([km1, pad, km0, pad], axis=0)
    kmx_hi = kmx.astype(BF16)
    kmx_lo = (kmx - kmx_hi.astype(F32)).astype(BF16)
    q_bf = q_t.astype(BF16)
    gate = _dot(kmx_hi, q_bf) + _dot(kmx_lo, q_bf)
    pens = [jnp.where(_moba_attend(gate[base:base + 16], qi), 0.0, NEG) for base in (HEAD_DIM, 0)]

    qa = _augment_queries(q_t * (HEAD_DIM ** -0.5 * LOG2E), pens, nb)
    (a0, l0), (a1, l1) = _flash(qa, ka_ref, vt_ref, bias_ref, qi, (0, nb + 1))
    o_ref[0] = (a0 / l0 + a1 / l1).T.astype(o_ref.dtype)


def _diff_kernel(tab_ref, lam_ref, q_ref, k_ref, v_ref, sg_ref, o_ref, ka_ref, vt_ref, bias_ref, *, seq, lam_init):
    hd, b, qi = pl.program_id(0), pl.program_id(1), pl.program_id(2)

    @pl.when((b == 0) & (qi == 0))
    def _():
        _build_bias(tab_ref, bias_ref, 2 * hd)

    @pl.when(qi == 0)
    def _():
        _build_kv(k_ref, v_ref, ka_ref, vt_ref, seq, mask_v=False)

    lv = lam_ref[...]
    lam = (jnp.exp(jnp.sum(lv[0:1] * lv[1:2], axis=1, keepdims=True))
           - jnp.exp(jnp.sum(lv[2:3] * lv[3:4], axis=1, keepdims=True)) + lam_init)

    q_t = q_ref[0].astype(F32).T
    qa = _augment_queries(q_t * (HEAD_DIM ** -0.5 * LOG2E), None, seq // MOBA_BLOCK)
    (a0, l0), (a1, l1) = _flash(qa, ka_ref, vt_ref, bias_ref, qi, (0, 0))
    att = a0 / l0 - lam * (a1 / l1)
    y = att * lax.rsqrt(jnp.mean(att * att, axis=0, keepdims=True) + RMS_EPS)
    o_ref[0] = ((y.T * sg_ref[...]) * (1.0 - lam_init)).astype(o_ref.dtype)


def _attention(kind, q_src, kv_src, rel_bias, extra, *, batch, seq, q_col, k_col, v_col, lam_init=None):
    tq = MOBA_BLOCK
    nb = seq // tq
    n_hp = D_MODEL // LANES
    grid = (n_hp, batch, nb)
    smem = pl.BlockSpec(memory_space=pltpu.SMEM)
    q_spec = pl.BlockSpec((1, tq, LANES), lambda h, b, i: (b, i, q_col + h))
    k_spec = pl.BlockSpec((1, seq, LANES), lambda h, b, i: (b, 0, k_col + h))
    v_spec = pl.BlockSpec((1, seq, LANES), lambda h, b, i: (b, 0, v_col + h))
    o_spec = pl.BlockSpec((1, tq, LANES), lambda h, b, i: (b, i, h))
    scratch_k = pltpu.VMEM((2, seq + tq, LANES), BF16)
    bias_s = pltpu.VMEM((2, 2 * tq, tq), F32)
    out_shape = jax.ShapeDtypeStruct((batch, seq, D_MODEL), BF16)
    if kind == "moba":
        kmean = extra
        km_spec = pl.BlockSpec((nb, 1, LANES), lambda h, b, i: (b, 0, h))
        return pl.pallas_call(
            functools.partial(_moba_kernel, seq=seq),
            grid=grid,
            in_specs=[smem, q_spec, k_spec, v_spec, km_spec],
            out_specs=o_spec, out_shape=out_shape,
            scratch_shapes=[scratch_k, pltpu.VMEM((2 * (nb + 1), LANES, tq), BF16), bias_s],
            compiler_params=_params(3), name="moba_attention",
        )(rel_bias, q_src, kv_src, kv_src, kmean)
    lam_rows, sub_gain = extra
    return pl.pallas_call(
        functools.partial(_diff_kernel, seq=seq, lam_init=lam_init),
        grid=grid,
        in_specs=[smem, _full((8, LANES)), q_spec, k_spec, v_spec, _full((1, LANES))],
        out_specs=o_spec, out_shape=out_shape,
        scratch_shapes=[scratch_k, pltpu.VMEM((nb + 1, LANES, tq), BF16), bias_s],
        compiler_params=_params(3), name="diff_attention",
    )(rel_bias, lam_rows, q_src, kv_src, kv_src, sub_gain)


def _router(xn, wg_ref, bg_ref, we_ref, be_ref, tri_ref, count_ref):
    lane = lax.broadcasted_iota(jnp.int32, (xn.shape[0], LANES), 1)
    live = lane < MOE_EXPERTS
    gl = jnp.where(live, _dot_hi(xn, wg_ref[...]) + bg_ref[...], -jnp.inf)
    el = _dot_hi(xn, we_ref[...]) + be_ref[...]
    gmax = jnp.max(gl, axis=1, keepdims=True)
    gsum = jnp.sum(jnp.exp(gl - gmax), axis=1, keepdims=True) / MOE_EXPERTS_PER_GROUP
    g_gate = 1.0 / gsum
    first = jnp.min(jnp.where(gl == gmax, lane, 1 << 20), axis=1, keepdims=True)
    group_shift = MOE_EXPERTS_PER_GROUP.bit_length() - 1
    in_group = jnp.right_shift(lane, group_shift) == jnp.right_shift(first, group_shift)
    em = jnp.where(in_group & live, el, -jnp.inf)
    m1 = jnp.max(em, axis=1, keepdims=True)
    i1 = jnp.min(jnp.where(em == m1, lane, 1 << 20), axis=1, keepdims=True)
    em2 = jnp.where(lane == i1, -jnp.inf, em)
    m2 = jnp.max(em2, axis=1, keepdims=True)
    i2 = jnp.min(jnp.where(em2 == m2, lane, 1 << 20), axis=1, keepdims=True)
    p2 = jnp.exp(m2 - m1)
    w1 = g_gate * (1.0 / (1.0 + p2))
    w2 = g_gate * (p2 / (1.0 + p2))

    is1, is2 = lane == i1, lane == i2
    member = jnp.where(is1 | is2, 1.0, 0.0)
    before = _dot(tri_ref[...], member.astype(BF16)) + count_ref[...]
    rank1 = jnp.sum(jnp.where(is1, before, 0.0), axis=1, keepdims=True)
    rank2 = jnp.sum(jnp.where(is2, before, 0.0), axis=1, keepdims=True)
    count_ref[...] = count_ref[...] + jnp.sum(member, axis=0, keepdims=True)
    fields = (i1.astype(F32), i2.astype(F32), w1, w2, rank1, rank2)
    route = jnp.zeros(lane.shape, F32)
    for k, col in enumerate(fields):
        route = jnp.where(lane == k, col, route)
    return route


def _attn_out_kernel(a_ref, h_ref, wo_ref, fg_ref, wg_ref, bg_ref, we_ref, be_ref, tri_ref,
                     h1_ref, xn_ref, route_ref, counts_ref, count_acc):
    @pl.when(pl.program_id(0) == 0)
    def _():
        count_acc[...] = jnp.zeros_like(count_acc)

    h1 = h_ref[...] + _dot(a_ref[...], wo_ref[...])
    h1_ref[...] = h1
    xn = _rms(h1, fg_ref[...])
    xn_ref[...] = xn
    route_ref[...] = _router(xn, wg_ref, bg_ref, we_ref, be_ref, tri_ref, count_acc)
    counts_ref[...] = jnp.broadcast_to(count_acc[...], counts_ref.shape)


def _attn_out(att2, h2, wo_bf16, ffn_gain, wg_x, bg_x, we_x, be_x):
    t = h2.shape[0]
    tm = ROW_TILE
    row = lambda w: pl.BlockSpec((tm, w), lambda i: (i, 0))
    tri = jnp.tril(jnp.ones((tm, tm), F32), -1).astype(BF16)
    return pl.pallas_call(
        _attn_out_kernel,
        grid=(t // tm,),
        in_specs=[row(D_MODEL), row(D_MODEL), _full((D_MODEL, D_MODEL)), _full((1, D_MODEL)),
                  _full((D_MODEL, LANES)), _full((1, LANES)), _full((D_MODEL, LANES)), _full((1, LANES)),
                  _full((tm, tm))],
        out_specs=[row(D_MODEL), row(D_MODEL), row(LANES), _full((8, LANES))],
        out_shape=[jax.ShapeDtypeStruct((t, D_MODEL), F32), jax.ShapeDtypeStruct((t, D_MODEL), F32),
                   jax.ShapeDtypeStruct((t, LANES), F32), jax.ShapeDtypeStruct((8, LANES), F32)],
        scratch_shapes=[pltpu.VMEM((1, LANES), F32)],
        compiler_params=_params(1), name="attn_out_router",
    )(att2, h2, wo_bf16, ffn_gain.reshape(1, D_MODEL), wg_x, bg_x, we_x, be_x, tri)


def _scatter_kernel(pos1_ref, pos2_ref, xn_hbm, init_hbm, out_hbm, sem):
    del init_hbm
    tm = pos1_ref.shape[2]
    base = pl.program_id(0) * tm

    def issue(r, carry):
        src = xn_hbm.at[pl.ds(base + r, 1)]
        pltpu.make_async_copy(src, out_hbm.at[pl.ds(pos1_ref[0, 0, r], 1)], sem).start()
        pltpu.make_async_copy(src, out_hbm.at[pl.ds(pos2_ref[0, 0, r], 1)], sem).start()
        return carry

    lax.fori_loop(0, tm, issue, 0)
    pltpu.make_async_copy(xn_hbm.at[pl.ds(0, 2 * tm)], out_hbm.at[pl.ds(0, 2 * tm)], sem).wait()


def _scatter_rows(xn, pos1, pos2, n_rows):
    t = xn.shape[0]
    tm = ROW_TILE
    idx = lambda: pl.BlockSpec((1, 1, tm), lambda i: (i, 0, 0), memory_space=pltpu.SMEM)
    anyspec = pl.BlockSpec(memory_space=pl.ANY)
    return pl.pallas_call(
        _scatter_kernel,
        grid=(t // tm,),
        in_specs=[idx(), idx(), anyspec, anyspec],
        out_specs=anyspec,
        out_shape=jax.ShapeDtypeStruct((n_rows, D_MODEL), F32),
        scratch_shapes=[pltpu.SemaphoreType.DMA(())],
        input_output_aliases={3: 0},
        compiler_params=_params(1), name="moe_scatter",
    )(pos1.reshape(t // tm, 1, tm), pos2.reshape(t // tm, 1, tm), xn, jnp.zeros((n_rows, D_MODEL), F32))


def _moe_group_kernel(te_ref, x_ref, wg_ref, wu_ref, wd_ref, y_ref):
    del te_ref
    x = x_ref[...].astype(BF16)
    g = _dot(x, wg_ref[0])
    u = _dot(x, wu_ref[0])
    hh = (g * jax.nn.sigmoid(g)) * u
    y_ref[...] = _dot(hh.astype(BF16), wd_ref[0])


def _moe_group(x_sorted, tile_expert, wg, wu, wd):
    n_tiles = tile_expert.shape[0]
    tmg = GROUP_TILE
    wspec = lambda a, b: pl.BlockSpec((1, a, b), lambda w, te: (te[w], 0, 0))
    return pl.pallas_call(
        _moe_group_kernel,
        grid_spec=pltpu.PrefetchScalarGridSpec(
            num_scalar_prefetch=1, grid=(n_tiles,),
            in_specs=[pl.BlockSpec((tmg, D_MODEL), lambda w, te: (w, 0)),
                      wspec(D_MODEL, MOE_D_FF), wspec(D_MODEL, MOE_D_FF), wspec(MOE_D_FF, D_MODEL)],
            out_specs=pl.BlockSpec((tmg, D_MODEL), lambda w, te: (w, 0))),
        out_shape=jax.ShapeDtypeStruct((n_tiles * tmg, D_MODEL), F32),
        compiler_params=_params(1), name="moe_group_ffn",
    )(tile_expert, x_sorted, wg, wu, wd)


def _routing_tables(route, counts8, t):
    n_tiles = 2 * t // GROUP_TILE + MOE_EXPERTS
    counts = counts8[0, :MOE_EXPERTS].astype(jnp.int32)
    tiles_per = (counts + GROUP_TILE - 1) // GROUP_TILE
    tile_end = jnp.cumsum(tiles_per)
    offsets = (tile_end - tiles_per) * GROUP_TILE
    ids = route[:, 0:2].astype(jnp.int32)
    ranks = route[:, 4:6].astype(jnp.int32)
    pos = offsets[ids] + ranks
    tile_expert = jnp.minimum(jnp.searchsorted(tile_end, jnp.arange(n_tiles, dtype=jnp.int32), side="right"),
                              MOE_EXPERTS - 1).astype(jnp.int32)
    return pos[:, 0], pos[:, 1], tile_expert, n_tiles


def _start_gather(pos1_ref, pos2_ref, y_hbm, ybuf, sem):
    tm = pos1_ref.shape[2]

    def issue(r, carry):
        pltpu.make_async_copy(y_hbm.at[pl.ds(pos1_ref[0, 0, r], 1)], ybuf.at[0, pl.ds(r, 1)], sem).start()
        pltpu.make_async_copy(y_hbm.at[pl.ds(pos2_ref[0, 0, r], 1)], ybuf.at[1, pl.ds(r, 1)], sem).start()
        return carry

    lax.fori_loop(0, tm, issue, 0)


def _wait_gather(y_hbm, ybuf, sem):
    tm = ybuf.shape[1]
    for slot in range(2):
        pltpu.make_async_copy(y_hbm.at[pl.ds(0, tm)], ybuf.at[slot], sem).wait()


def _moe_ple_update(pos1_ref, pos2_ref, y_hbm, route_ref, h_ref, p_ref, pg_ref, wgate_ref, wproj_ref, ybuf, sem):
    _start_gather(pos1_ref, pos2_ref, y_hbm, ybuf, sem)
    proj = _dot(p_ref[...].astype(BF16), wproj_ref[...])
    route = route_ref[...]
    _wait_gather(y_hbm, ybuf, sem)
    h = h_ref[...] + (route[:, 2:3] * ybuf[0] + route[:, 3:4] * ybuf[1])
    gate = jax.nn.sigmoid(_dot(_rms(h, pg_ref[...]).astype(BF16), wgate_ref[...]))
    return h + gate * proj


def _ple_mid_kernel(pos1_ref, pos2_ref, y_hbm, route_ref, h_ref, p_ref, pg_ref, wgate_ref, wproj_ref,
                    kvg_ref, wkv_ref, qg_ref, wq_ref, h3_ref, kv_ref, q_ref, ybuf, sem):
    h3 = _moe_ple_update(pos1_ref, pos2_ref, y_hbm, route_ref, h_ref, p_ref, pg_ref, wgate_ref, wproj_ref, ybuf, sem)
    h3_ref[...] = h3
    kn = _rms(h3, kvg_ref[...]).astype(BF16)
    for c in range(2):
        kv_ref[:, c * D_MODEL:(c + 1) * D_MODEL] = _dot(kn, wkv_ref[:, c * D_MODEL:(c + 1) * D_MODEL]).astype(BF16)
    q_ref[...] = _dot(_rms(h3, qg_ref[...]).astype(BF16), wq_ref[...]).astype(BF16)


def _ple_last_kernel(pos1_ref, pos2_ref, y_hbm, route_ref, h_ref, p_ref, pg_ref, wgate_ref, wproj_ref,
                     fg_ref, o_ref, ybuf, sem):
    h3 = _moe_ple_update(pos1_ref, pos2_ref, y_hbm, route_ref, h_ref, p_ref, pg_ref, wgate_ref, wproj_ref, ybuf, sem)
    o_ref[...] = _rms(h3, fg_ref[...])


def _ple_call(body, name, moe_in, h1, p2, vecs_and_weights, in_tail, out_specs, out_shape):
    pos1, pos2, y_sorted, route = moe_in
    t = h1.shape[0]
    tm = ROW_TILE
    row = lambda w: pl.BlockSpec((tm, w), lambda i: (i, 0))
    idx = lambda: pl.BlockSpec((1, 1, tm), lambda i: (i, 0, 0), memory_space=pltpu.SMEM)
    return pl.pallas_call(
        body,
        grid=(t // tm,),
        in_specs=[idx(), idx(), pl.BlockSpec(memory_space=pl.ANY), row(LANES), row(D_MODEL), row(PLE_DIM)] + in_tail,
        out_specs=out_specs, out_shape=out_shape,
        scratch_shapes=[pltpu.VMEM((2, tm, D_MODEL), F32), pltpu.SemaphoreType.DMA(())],
        compiler_params=_params(1), name=name,
    )(pos1.reshape(t // tm, 1, tm), pos2.reshape(t // tm, 1, tm), y_sorted, route, h1, p2, *vecs_and_weights)


def _ple_mid(moe_in, h1, p2, ple_gain, wgate, wproj, kv_gain, wkv, q_gain, wq):
    t = h1.shape[0]
    tm = ROW_TILE
    row = lambda w: pl.BlockSpec((tm, w), lambda i: (i, 0))
    vec = _full((1, D_MODEL))
    return _ple_call(
        _ple_mid_kernel, "moe_combine_ple_kv_q", moe_in, h1, p2,
        (ple_gain.reshape(1, -1), wgate, wproj, kv_gain.reshape(1, -1), wkv, q_gain.reshape(1, -1), wq),
        [vec, _full((D_MODEL, D_MODEL)), _full((PLE_DIM, D_MODEL)), vec, _full((D_MODEL, 2 * D_MODEL)), vec,
         _full((D_MODEL, D_MODEL))],
        [row(D_MODEL), row(2 * D_MODEL), row(D_MODEL)],
        [jax.ShapeDtypeStruct((t, D_MODEL), F32), jax.ShapeDtypeStruct((t, 2 * D_MODEL), BF16),
         jax.ShapeDtypeStruct((t, D_MODEL), BF16)])


def _ple_last(moe_in, h1, p2, ple_gain, wgate, wproj, final_gain):
    t = h1.shape[0]
    tm = ROW_TILE
    vec = _full((1, D_MODEL))
    return _ple_call(
        _ple_last_kernel, "moe_combine_ple_final_norm", moe_in, h1, p2,
        (ple_gain.reshape(1, -1), wgate, wproj, final_gain.reshape(1, -1)),
        [vec, _full((D_MODEL, D_MODEL)), _full((PLE_DIM, D_MODEL)), vec],
        pl.BlockSpec((tm, D_MODEL), lambda i: (i, 0)),
        jax.ShapeDtypeStruct((t, D_MODEL), F32))


def _router_operands(w_group, b_group, w_router, b_router):
    pad = LANES - MOE_EXPERTS
    wg = jnp.pad(jnp.repeat(w_group, MOE_EXPERTS_PER_GROUP, axis=1), ((0, 0), (0, pad)))
    bg = jnp.pad(jnp.repeat(b_group, MOE_EXPERTS_PER_GROUP), (0, pad)).reshape(1, LANES)
    we = jnp.pad(w_router, ((0, 0), (0, pad)))
    be = jnp.pad(b_router, (0, pad)).reshape(1, LANES)
    return wg, bg, we, be


def _moe_layer(att, h, wo, i, ffn_norm, w_group, b_group, w_router, b_router, w_gate, w_up, w_down):
    t = h.shape[0]
    h1, xn, route, counts8 = _attn_out(att, h, wo.astype(BF16), ffn_norm[i],
                                       *_router_operands(w_group[i], b_group[i], w_router[i], b_router[i]))
    pos1, pos2, tile_expert, n_tiles = _routing_tables(route, counts8, t)
    x_sorted = _scatter_rows(xn, pos1, pos2, n_tiles * GROUP_TILE)
    wg = w_gate[i].reshape(MOE_EXPERTS, D_MODEL, MOE_D_FF).astype(BF16)
    wu = w_up[i].reshape(MOE_EXPERTS, D_MODEL, MOE_D_FF).astype(BF16)
    wd = w_down[i].reshape(MOE_EXPERTS, MOE_D_FF, D_MODEL).astype(BF16)
    y_sorted = _moe_group(x_sorted, tile_expert, wg, wu, wd)
    return (pos1, pos2, y_sorted, route), h1


def kernel(x, p, rel_bias, attn_norm_a, w_qkv_a, w_o_a, kv_norm, w_kv, attn_norm_b, w_q_b, lambda_q1, lambda_k1,
           lambda_q2, lambda_k2, subln_b, w_o_b, ffn_norm, w_group, b_group, w_router, b_router, w_gate, w_up,
           w_down, ple_norm, w_ple_gate, w_ple_proj, final_norm):
    batch, seq, d = x.shape
    assert d == D_MODEL and seq % MOBA_BLOCK == 0 and seq // MOBA_BLOCK <= 16
    t = batch * seq
    assert t % ROW_TILE == 0
    n_hp = D_MODEL // LANES
    moe = (ffn_norm, w_group, b_group, w_router, b_router, w_gate, w_up, w_down)

    h = x.reshape(t, d)
    qkv, kmean = _qkv_proj(h, attn_norm_a[0], w_qkv_a[0].astype(BF16))
    qkv3 = qkv.reshape(batch, seq, 3 * d)
    att = _attention("moba", qkv3, qkv3, rel_bias, kmean, batch=batch, seq=seq,
                     q_col=0, k_col=n_hp, v_col=2 * n_hp)
    moe_out, h = _moe_layer(att.reshape(t, d), h, w_o_a[0], 0, *moe)
    h, kv, q = _ple_mid(moe_out, h, p[0].reshape(t, PLE_DIM), ple_norm[0], w_ple_gate[0].astype(BF16),
                        w_ple_proj[0].astype(BF16), kv_norm, w_kv.astype(BF16), attn_norm_b[0],
                        w_q_b[0].astype(BF16))

    lam_init = 0.8 - 0.6 * math.exp(-0.3 * 1)
    lam_rows = jnp.pad(jnp.stack([lambda_q1[0], lambda_k1[0], lambda_q2[0], lambda_k2[0]]).astype(F32),
                       ((0, 4), (0, LANES - HEAD_DIM)))
    att = _attention("diff", q.reshape(batch, seq, d), kv.reshape(batch, seq, 2 * d), rel_bias,
                     (lam_rows, subln_b[0].reshape(1, LANES)), batch=batch, seq=seq,
                     q_col=0, k_col=0, v_col=n_hp, lam_init=lam_init)
    moe_out, h = _moe_layer(att.reshape(t, d), h, w_o_b[0], 1, *moe)
    out = _ple_last(moe_out, h, p[1].reshape(t, PLE_DIM), ple_norm[1], w_ple_gate[1].astype(BF16),
                    w_ple_proj[1].astype(BF16), final_norm)
    return out.reshape(batch, seq, d)
```

```python
import functools
import math

import jax
import jax.numpy as jnp
from jax import lax
from jax.experimental import pallas as pl
from jax.experimental.pallas import tpu as pltpu

F32 = jnp.float32
BF16 = jnp.bfloat16

D_MODEL = 1024
DEPTH = 2
N_A_LAYERS = DEPTH // 2
HEAD_DIM = 64
LANES = 128
MOBA_BLOCK = 256
MOBA_TOP_K = 3
REL_BUCKETS = 32
REL_MAX_DISTANCE = 128
MOE_GROUPS = 4
MOE_EXPERTS_PER_GROUP = 8
MOE_EXPERTS = MOE_GROUPS * MOE_EXPERTS_PER_GROUP
MOE_D_FF = D_MODEL // 4
PLE_DIM = 256
RMS_EPS = 1e-6
NEG = -1e30
LOG2E = math.log2(math.e)

ROW_TILE = 512
GROUP_TILE = 512
VMEM_LIMIT = 52 * 1024 * 1024


def _dot(a, b):
    return jnp.dot(a, b, preferred_element_type=F32)


def _dot_hi(a, b):
    return jnp.dot(a, b, preferred_element_type=F32, precision=lax.Precision.HIGHEST)


def _rms(x, gain):
    y = x * lax.rsqrt(jnp.mean(x * x, axis=-1, keepdims=True) + RMS_EPS)
    return y * gain


def _params(n_axes):
    return pltpu.CompilerParams(dimension_semantics=("arbitrary",) * n_axes,
                                vmem_limit_bytes=VMEM_LIMIT)


def _full(shape):
    nd = len(shape)
    return pl.BlockSpec(shape, lambda *_: (0,) * nd)


def _qkv_kernel(x_ref, g_ref, w_ref, qkv_ref, kmean_ref):
    hn = _rms(x_ref[...], g_ref[...]).astype(BF16)
    for c in range(3):
        y = _dot(hn, w_ref[:, c * D_MODEL:(c + 1) * D_MODEL])
        qkv_ref[:, c * D_MODEL:(c + 1) * D_MODEL] = y.astype(BF16)
        if c == 1:
            nb = y.shape[0] // MOBA_BLOCK
            kmean_ref[...] = jnp.mean(y.reshape(nb, MOBA_BLOCK, D_MODEL), axis=1, keepdims=True)


def _qkv_proj(x2, gain, w_bf16):
    t = x2.shape[0]
    tm = ROW_TILE
    return pl.pallas_call(
        _qkv_kernel,
        grid=(t // tm,),
        in_specs=[pl.BlockSpec((tm, D_MODEL), lambda i: (i, 0)),
                  _full((1, D_MODEL)),
                  _full((D_MODEL, 3 * D_MODEL))],
        out_specs=[pl.BlockSpec((tm, 3 * D_MODEL), lambda i: (i, 0)),
                   pl.BlockSpec((tm // MOBA_BLOCK, 1, D_MODEL), lambda i: (i, 0, 0))],
        out_shape=[jax.ShapeDtypeStruct((t, 3 * D_MODEL), BF16),
                   jax.ShapeDtypeStruct((t // MOBA_BLOCK, 1, D_MODEL), F32)],
        compiler_params=_params(1),
        name="qkv_proj",
    )(x2, gain.reshape(1, D_MODEL), w_bf16)


def _rel_bucket(dist):
    n = jnp.maximum(dist, 0)
    max_exact = REL_BUCKETS // 2
    nf = jnp.maximum(n, max_exact).astype(F32)
    large = max_exact + (jnp.log(nf / max_exact) / math.log(REL_MAX_DISTANCE / max_exact)
                         * (REL_BUCKETS - max_exact)).astype(jnp.int32)
    large = jnp.minimum(large, REL_BUCKETS - 1)
    return jnp.where(n < max_exact, n, large)


def _build_bias(tab_ref, bias_ref, col0):
    tq = MOBA_BLOCK
    rows = 64
    for r0 in range(0, 2 * tq, rows):
        key = lax.broadcasted_iota(jnp.int32, (rows, tq), 0) + r0
        qry = lax.broadcasted_iota(jnp.int32, (rows, tq), 1)
        dist = qry + tq - key
        bkt = _rel_bucket(dist)
        for h in range(2):
            acc = jnp.zeros((rows, tq), F32)
            for i in range(REL_BUCKETS):
                acc = jnp.where(bkt == i, tab_ref[i, col0 + h], acc)
            far = tab_ref[REL_BUCKETS - 1, col0 + h]
            bias_ref[h, r0:r0 + rows, :] = jnp.where(dist >= 0, (acc - far) * LOG2E, NEG)


def _build_kv(k_ref, v_ref, ka_ref, vt_ref, seq, mask_v):
    tq = MOBA_BLOCK
    nb = seq // tq
    lane = lax.broadcasted_iota(jnp.int32, (tq, LANES), 1)
    first = lane < HEAD_DIM
    ka_ref[0, 0:tq, :] = (lane == HEAD_DIM + nb).astype(F32).astype(BF16)
    ka_ref[1, 0:tq, :] = (lane == nb).astype(F32).astype(BF16)
    vt_ref[0] = jnp.zeros((LANES, tq), BF16)
    if mask_v:
        vt_ref[nb + 1] = jnp.zeros((LANES, tq), BF16)

    def body(j, _):
        r = pl.multiple_of(j * tq, tq)
        k = k_ref[0, pl.ds(r, tq), :].astype(F32)
        oh0 = (lane == HEAD_DIM + j).astype(F32)
        oh1 = (lane == j).astype(F32)
        ka_ref[0, pl.ds(r + tq, tq), :] = jnp.where(first, k, oh0).astype(BF16)
        ka_ref[1, pl.ds(r + tq, tq), :] = jnp.where(first, oh1, k).astype(BF16)
        vf = v_ref[0, pl.ds(r, tq), :].astype(F32)
        if mask_v:
            vt_ref[j + 1] = jnp.where(first, vf, 0.0).T.astype(BF16)
            vt_ref[nb + 2 + j] = jnp.where(first, 0.0, vf).T.astype(BF16)
        else:
            vt_ref[j + 1] = vf.T.astype(BF16)
        return 0

    lax.fori_loop(0, nb, body, 0)


def _moba_attend(gate, qi):
    n = lax.broadcasted_iota(jnp.int32, gate.shape, 0)
    rem = n < qi
    sel = n == qi
    for _ in range(MOBA_TOP_K):
        gm = jnp.where(rem, gate, -jnp.inf)
        mx = jnp.max(gm, axis=0, keepdims=True)
        cand = rem & (gm == mx)
        idx = jnp.min(jnp.where(cand, n, 1 << 20), axis=0, keepdims=True)
        pick = cand & (n == idx)
        sel = sel | pick
        rem = rem & jnp.logical_not(pick)
    return sel


def _augment_queries(qs_t, pens, nb):
    tq = qs_t.shape[1]
    out = []
    for h in range(2):
        if pens is not None:
            pen16 = pens[h]
        else:
            pen16 = jnp.where(lax.broadcasted_iota(jnp.int32, (16, tq), 0) >= nb, NEG, 0.0)
        tail = [pen16, jnp.full((8, tq), NEG, F32), jnp.zeros((HEAD_DIM - 24, tq), F32)]
        parts = [qs_t[0:HEAD_DIM]] + tail if h == 0 else tail + [qs_t[HEAD_DIM:]]
        out.append(jnp.concatenate(parts, axis=0).astype(BF16))
    return out


def _flash(qa, ka_ref, vt_ref, bias_ref, qi, v_base, bufs):
    tq = MOBA_BLOCK
    s_a, s_b, m_ref, l_ref, acc_ref = bufs
    n_past = lax.shift_right_logical(qi, 1)

    def pair_block(j):
        return qi - 2 * j

    def scores_into(buf, j, bias=False):
        r = pl.multiple_of(pair_block(j) * tq, tq)
        for h in range(2):
            s = _dot(ka_ref[h, pl.ds(r, 2 * tq), :], qa[h])
            buf[h] = s + bias_ref[h] if bias else s

    def update(buf, j):
        blk0 = pair_block(j)
        for h in range(2):
            s = buf[h]
            m = m_ref[h]
            m_new = jnp.maximum(m, jnp.max(s, axis=0, keepdims=True))
            alpha = jnp.exp2(m - m_new)
            p = jnp.exp2(s - m_new)
            pb = p.astype(BF16)
            pv = _dot(vt_ref[v_base[h] + blk0], pb[0:tq]) + _dot(vt_ref[v_base[h] + blk0 + 1], pb[tq:])
            m_ref[h] = m_new
            l_ref[h] = alpha * l_ref[h] + jnp.sum(p, axis=0, keepdims=True)
            acc_ref[h] = alpha * acc_ref[h] + pv

    m_ref[...] = jnp.full(m_ref.shape, 3.0 * NEG, F32)
    l_ref[...] = jnp.zeros(l_ref.shape, F32)
    acc_ref[...] = jnp.zeros(acc_ref.shape, F32)
    scores_into(s_a, 0, bias=True)

    def body(t, carry):
        scores_into(s_b, 2 * t + 1)
        update(s_a, 2 * t)
        scores_into(s_a, 2 * t + 2)
        update(s_b, 2 * t + 1)
        return carry

    n_double = lax.shift_right_logical(n_past, 1)
    lax.fori_loop(0, n_double, body, 0)
    odd = (n_past & 1) == 1

    @pl.when(odd)
    def _():
        scores_into(s_b, n_past)

    update(s_a, 2 * n_double)

    @pl.when(odd)
    def _():
        update(s_b, n_past)

    return [(acc_ref[h], l_ref[h]) for h in range(2)]


def _moba_kernel(tab_ref, q_ref, k_ref, v_ref, km_ref, o_ref, ka_ref, vt_ref, bias_ref, *bufs, seq):
    hp, b, qi = pl.program_id(0), pl.program_id(1), pl.program_id(2)
    nb = seq // MOBA_BLOCK

    @pl.when((b == 0) & (qi == 0))
    def _():
        _build_bias(tab_ref, bias_ref, 2 * hp)

    @pl.when(qi == 0)
    def _():
        _build_kv(k_ref, v_ref, ka_ref, vt_ref, seq, mask_v=True)

    q_t = q_ref[0].astype(F32).T
    km = km_ref[:, 0, :]
    lane_k = lax.broadcasted_iota(jnp.int32, (nb, LANES), 1)
    km0 = jnp.where(lane_k < HEAD_DIM, km, 0.0)
    km1 = jnp.where(lane_k < HEAD_DIM, 0.0, km)
    pad = jnp.zeros((HEAD_DIM - nb, LANES), F32)
    kmx = jnp.concatenate([km1, pad, km0, pad], axis=0)
    kmx_hi = kmx.astype(BF16)
    kmx_lo = (kmx - kmx_hi.astype(F32)).astype(BF16)
    q_bf = q_t.astype(BF16)
    gate = _dot(kmx_hi, q_bf) + _dot(kmx_lo, q_bf)
    pens = [jnp.where(_moba_attend(gate[base:base + 16], qi), 0.0, NEG) for base in (HEAD_DIM, 0)]

    qa = _augment_queries(q_t * (HEAD_DIM ** -0.5 * LOG2E), pens, nb)
    (a0, l0), (a1, l1) = _flash(qa, ka_ref, vt_ref, bias_ref, qi, (0, nb + 1), bufs)
    o_ref[0] = (a0 / l0 + a1 / l1).T.astype(o_ref.dtype)


def _diff_kernel(tab_ref, lam_ref, q_ref, k_ref, v_ref, sg_ref, o_ref, ka_ref, vt_ref, bias_ref, *bufs, seq,
                 lam_init):
    hd, b, qi = pl.program_id(0), pl.program_id(1), pl.program_id(2)

    @pl.when((b == 0) & (qi == 0))
    def _():
        _build_bias(tab_ref, bias_ref, 2 * hd)

    @pl.when(qi == 0)
    def _():
        _build_kv(k_ref, v_ref, ka_ref, vt_ref, seq, mask_v=False)

    lv = lam_ref[...]
    lam = (jnp.exp(jnp.sum(lv[0:1] * lv[1:2], axis=1, keepdims=True))
           - jnp.exp(jnp.sum(lv[2:3] * lv[3:4], axis=1, keepdims=True)) + lam_init)

    q_t = q_ref[0].astype(F32).T
    qa = _augment_queries(q_t * (HEAD_DIM ** -0.5 * LOG2E), None, seq // MOBA_BLOCK)
    (a0, l0), (a1, l1) = _flash(qa, ka_ref, vt_ref, bias_ref, qi, (0, 0), bufs)
    att = a0 / l0 - lam * (a1 / l1)
    y = att * lax.rsqrt(jnp.mean(att * att, axis=0, keepdims=True) + RMS_EPS)
    o_ref[0] = ((y.T * sg_ref[...]) * (1.0 - lam_init)).astype(o_ref.dtype)


def _attention(kind, q_src, kv_src, rel_bias, extra, *, batch, seq, q_col, k_col, v_col, lam_init=None):
    tq = MOBA_BLOCK
    nb = seq // tq
    n_hp = D_MODEL // LANES
    grid = (n_hp, batch, nb)
    smem = pl.BlockSpec(memory_space=pltpu.SMEM)
    q_spec = pl.BlockSpec((1, tq, LANES), lambda h, b, i: (b, i, q_col + h))
    k_spec = pl.BlockSpec((1, seq, LANES), lambda h, b, i: (b, 0, k_col + h))
    v_spec = pl.BlockSpec((1, seq, LANES), lambda h, b, i: (b, 0, v_col + h))
    o_spec = pl.BlockSpec((1, tq, LANES), lambda h, b, i: (b, i, h))
    scratch_k = pltpu.VMEM((2, seq + tq, LANES), BF16)
    bias_s = pltpu.VMEM((2, 2 * tq, tq), F32)
    flash_bufs = [pltpu.VMEM((2, 2 * tq, tq), F32), pltpu.VMEM((2, 2 * tq, tq), F32),
                  pltpu.VMEM((2, 1, tq), F32), pltpu.VMEM((2, 1, tq), F32), pltpu.VMEM((2, LANES, tq), F32)]
    out_shape = jax.ShapeDtypeStruct((batch, seq, D_MODEL), BF16)
    if kind == "moba":
        kmean = extra
        km_spec = pl.BlockSpec((nb, 1, LANES), lambda h, b, i: (b, 0, h))
        return pl.pallas_call(
            functools.partial(_moba_kernel, seq=seq),
            grid=grid,
            in_specs=[smem, q_spec, k_spec, v_spec, km_spec],
            out_specs=o_spec, out_shape=out_shape,
            scratch_shapes=[scratch_k, pltpu.VMEM((2 * (nb + 1), LANES, tq), BF16), bias_s] + flash_bufs,
            compiler_params=_params(3), name="moba_attention",
        )(rel_bias, q_src, kv_src, kv_src, kmean)
    lam_rows, sub_gain = extra
    return pl.pallas_call(
        functools.partial(_diff_kernel, seq=seq, lam_init=lam_init),
        grid=grid,
        in_specs=[smem, _full((8, LANES)), q_spec, k_spec, v_spec, _full((1, LANES))],
        out_specs=o_spec, out_shape=out_shape,
        scratch_shapes=[scratch_k, pltpu.VMEM((nb + 1, LANES, tq), BF16), bias_s] + flash_bufs,
        compiler_params=_params(3), name="diff_attention",
    )(rel_bias, lam_rows, q_src, kv_src, kv_src, sub_gain)


def _router(xn, wg_ref, bg_ref, we_ref, be_ref, tri_ref, count_ref):
    lane = lax.broadcasted_iota(jnp.int32, (xn.shape[0], LANES), 1)
    live = lane < MOE_EXPERTS
    gl = jnp.where(live, _dot_hi(xn, wg_ref[...]) + bg_ref[...], -jnp.inf)
    el = _dot_hi(xn, we_ref[...]) + be_ref[...]
    gmax = jnp.max(gl, axis=1, keepdims=True)
    gsum = jnp.sum(jnp.exp(gl - gmax), axis=1, keepdims=True) / MOE_EXPERTS_PER_GROUP
    g_gate = 1.0 / gsum
    first = jnp.min(jnp.where(gl == gmax, lane, 1 << 20), axis=1, keepdims=True)
    group_shift = MOE_EXPERTS_PER_GROUP.bit_length() - 1
    in_group = jnp.right_shift(lane, group_shift) == jnp.right_shift(first, group_shift)
    em = jnp.where(in_group & live, el, -jnp.inf)
    m1 = jnp.max(em, axis=1, keepdims=True)
    i1 = jnp.min(jnp.where(em == m1, lane, 1 << 20), axis=1, keepdims=True)
    em2 = jnp.where(lane == i1, -jnp.inf, em)
    m2 = jnp.max(em2, axis=1, keepdims=True)
    i2 = jnp.min(jnp.where(em2 == m2, lane, 1 << 20), axis=1, keepdims=True)
    p2 = jnp.exp(m2 - m1)
    w1 = g_gate * (1.0 / (1.0 + p2))
    w2 = g_gate * (p2 / (1.0 + p2))

    is1, is2 = lane == i1, lane == i2
    member = jnp.where(is1 | is2, 1.0, 0.0)
    before = _dot(tri_ref[...], member.astype(BF16)) + count_ref[...]
    rank1 = jnp.sum(jnp.where(is1, before, 0.0), axis=1, keepdims=True)
    rank2 = jnp.sum(jnp.where(is2, before, 0.0), axis=1, keepdims=True)
    count_ref[...] = count_ref[...] + jnp.sum(member, axis=0, keepdims=True)
    fields = (i1.astype(F32), i2.astype(F32), w1, w2, rank1, rank2)
    route = jnp.zeros(lane.shape, F32)
    for k, col in enumerate(fields):
        route = jnp.where(lane == k, col, route)
    return route


def _attn_out_kernel(a_ref, h_ref, wo_ref, fg_ref, wg_ref, bg_ref, we_ref, be_ref, tri_ref,
                     h1_ref, xn_ref, route_ref, counts_ref, count_acc):
    @pl.when(pl.program_id(0) == 0)
    def _():
        count_acc[...] = jnp.zeros_like(count_acc)

    h1 = h_ref[...] + _dot(a_ref[...], wo_ref[...])
    h1_ref[...] = h1
    xn = _rms(h1, fg_ref[...])
    xn_ref[...] = xn
    route_ref[...] = _router(xn, wg_ref, bg_ref, we_ref, be_ref, tri_ref, count_acc)
    counts_ref[...] = jnp.broadcast_to(count_acc[...], counts_ref.shape)


def _attn_out(att2, h2, wo_bf16, ffn_gain, wg_x, bg_x, we_x, be_x):
    t = h2.shape[0]
    tm = ROW_TILE
    row = lambda w: pl.BlockSpec((tm, w), lambda i: (i, 0))
    tri = jnp.tril(jnp.ones((tm, tm), F32), -1).astype(BF16)
    return pl.pallas_call(
        _attn_out_kernel,
        grid=(t // tm,),
        in_specs=[row(D_MODEL), row(D_MODEL), _full((D_MODEL, D_MODEL)), _full((1, D_MODEL)),
                  _full((D_MODEL, LANES)), _full((1, LANES)), _full((D_MODEL, LANES)), _full((1, LANES)),
                  _full((tm, tm))],
        out_specs=[row(D_MODEL), row(D_MODEL), row(LANES), _full((8, LANES))],
        out_shape=[jax.ShapeDtypeStruct((t, D_MODEL), F32), jax.ShapeDtypeStruct((t, D_MODEL), F32),
                   jax.ShapeDtypeStruct((t, LANES), F32), jax.ShapeDtypeStruct((8, LANES), F32)],
        scratch_shapes=[pltpu.VMEM((1, LANES), F32)],
        compiler_params=_params(1), name="attn_out_router",
    )(att2, h2, wo_bf16, ffn_gain.reshape(1, D_MODEL), wg_x, bg_x, we_x, be_x, tri)


def _scatter_kernel(pos1_ref, pos2_ref, xn_ref, init_hbm, out_hbm, sem):
    del init_hbm
    tm = pos1_ref.shape[2]

    def issue(r, carry):
        src = xn_ref.at[pl.ds(r, 1)]
        pltpu.make_async_copy(src, out_hbm.at[pl.ds(pos1_ref[0, 0, r], 1)], sem).start()
        pltpu.make_async_copy(src, out_hbm.at[pl.ds(pos2_ref[0, 0, r], 1)], sem).start()
        return carry

    lax.fori_loop(0, tm, issue, 0)
    for _ in range(2):
        pltpu.make_async_copy(xn_ref, out_hbm.at[pl.ds(0, tm)], sem).wait()


def _scatter_rows(xn, pos1, pos2, n_rows):
    t = xn.shape[0]
    tm = ROW_TILE
    idx = lambda: pl.BlockSpec((1, 1, tm), lambda i: (i, 0, 0), memory_space=pltpu.SMEM)
    anyspec = pl.BlockSpec(memory_space=pl.ANY)
    return pl.pallas_call(
        _scatter_kernel,
        grid=(t // tm,),
        in_specs=[idx(), idx(), pl.BlockSpec((tm, D_MODEL), lambda i: (i, 0)), anyspec],
        out_specs=anyspec,
        out_shape=jax.ShapeDtypeStruct((n_rows, D_MODEL), F32),
        scratch_shapes=[pltpu.SemaphoreType.DMA(())],
        input_output_aliases={3: 0},
        compiler_params=_params(1), name="moe_scatter",
    )(pos1.reshape(t // tm, 1, tm), pos2.reshape(t // tm, 1, tm), xn, jnp.zeros((n_rows, D_MODEL), F32))


def _moe_group_kernel(te_ref, x_ref, wg_ref, wu_ref, wd_ref, y_ref):
    del te_ref
    x = x_ref[...].astype(BF16)
    g = _dot(x, wg_ref[0])
    u = _dot(x, wu_ref[0])
    hh = (g * jax.nn.sigmoid(g)) * u
    y_ref[...] = _dot(hh.astype(BF16), wd_ref[0])


def _moe_group(x_sorted, tile_expert, wg, wu, wd):
    n_tiles = tile_expert.shape[0]
    tmg = GROUP_TILE
    wspec = lambda a, b: pl.BlockSpec((1, a, b), lambda w, te: (te[w], 0, 0))
    return pl.pallas_call(
        _moe_group_kernel,
        grid_spec=pltpu.PrefetchScalarGridSpec(
            num_scalar_prefetch=1, grid=(n_tiles,),
            in_specs=[pl.BlockSpec((tmg, D_MODEL), lambda w, te: (w, 0)),
                      wspec(D_MODEL, MOE_D_FF), wspec(D_MODEL, MOE_D_FF), wspec(MOE_D_FF, D_MODEL)],
            out_specs=pl.BlockSpec((tmg, D_MODEL), lambda w, te: (w, 0))),
        out_shape=jax.ShapeDtypeStruct((n_tiles * tmg, D_MODEL), F32),
        compiler_params=_params(1), name="moe_group_ffn",
    )(tile_expert, x_sorted, wg, wu, wd)


def _routing_tables(route, counts8, t):
    n_tiles = 2 * t // GROUP_TILE + MOE_EXPERTS
    counts = counts8[0, :MOE_EXPERTS].astype(jnp.int32)
    tiles_per = (counts + GROUP_TILE - 1) // GROUP_TILE
    tile_end = jnp.cumsum(tiles_per)
    offsets = (tile_end - tiles_per) * GROUP_TILE
    ids = route[:, 0:2].astype(jnp.int32)
    ranks = route[:, 4:6].astype(jnp.int32)
    expert_ids = jnp.arange(MOE_EXPERTS, dtype=jnp.int32)
    pos = jnp.sum(jnp.where(ids[:, :, None] == expert_ids, offsets, 0), axis=2) + ranks
    tile_ids = jnp.arange(n_tiles, dtype=jnp.int32)
    tile_expert = jnp.minimum(jnp.sum((tile_end[None, :] <= tile_ids[:, None]).astype(jnp.int32), axis=1),
                              MOE_EXPERTS - 1)
    return pos[:, 0], pos[:, 1], tile_expert, n_tiles


def _start_gather(pos1_ref, pos2_ref, y_hbm, ybuf, sem):
    tm = pos1_ref.shape[2]

    def issue(r, carry):
        pltpu.make_async_copy(y_hbm.at[pl.ds(pos1_ref[0, 0, r], 1)], ybuf.at[0, pl.ds(r, 1)], sem).start()
        pltpu.make_async_copy(y_hbm.at[pl.ds(pos2_ref[0, 0, r], 1)], ybuf.at[1, pl.ds(r, 1)], sem).start()
        return carry

    lax.fori_loop(0, tm, issue, 0)


def _wait_gather(y_hbm, ybuf, sem):
    tm = ybuf.shape[1]
    for slot in range(2):
        pltpu.make_async_copy(y_hbm.at[pl.ds(0, tm)], ybuf.at[slot], sem).wait()


def _moe_ple_update(pos1_ref, pos2_ref, y_hbm, route_ref, h_ref, p_ref, pg_ref, wgate_ref, wproj_ref, ybuf, sem):
    _start_gather(pos1_ref, pos2_ref, y_hbm, ybuf, sem)
    proj = _dot(p_ref[...].astype(BF16), wproj_ref[...])
    route = route_ref[...]
    _wait_gather(y_hbm, ybuf, sem)
    h = h_ref[...] + (route[:, 2:3] * ybuf[0] + route[:, 3:4] * ybuf[1])
    gate = jax.nn.sigmoid(_dot(_rms(h, pg_ref[...]).astype(BF16), wgate_ref[...]))
    return h + gate * proj


def _ple_mid_kernel(pos1_ref, pos2_ref, y_hbm, route_ref, h_ref, p_ref, pg_ref, wgate_ref, wproj_ref,
                    kvg_ref, wkv_ref, qg_ref, wq_ref, h3_ref, kv_ref, q_ref, ybuf, sem):
    h3 = _moe_ple_update(pos1_ref, pos2_ref, y_hbm, route_ref, h_ref, p_ref, pg_ref, wgate_ref, wproj_ref, ybuf, sem)
    h3_ref[...] = h3
    kn = _rms(h3, kvg_ref[...]).astype(BF16)
    for c in range(2):
        kv_ref[:, c * D_MODEL:(c + 1) * D_MODEL] = _dot(kn, wkv_ref[:, c * D_MODEL:(c + 1) * D_MODEL]).astype(BF16)
    q_ref[...] = _dot(_rms(h3, qg_ref[...]).astype(BF16), wq_ref[...]).astype(BF16)


def _ple_last_kernel(pos1_ref, pos2_ref, y_hbm, route_ref, h_ref, p_ref, pg_ref, wgate_ref, wproj_ref,
                     fg_ref, o_ref, ybuf, sem):
    h3 = _moe_ple_update(pos1_ref, pos2_ref, y_hbm, route_ref, h_ref, p_ref, pg_ref, wgate_ref, wproj_ref, ybuf, sem)
    o_ref[...] = _rms(h3, fg_ref[...])


def _ple_call(body, name, moe_in, h1, p2, vecs_and_weights, in_tail, out_specs, out_shape):
    pos1, pos2, y_sorted, route = moe_in
    t = h1.shape[0]
    tm = ROW_TILE
    row = lambda w: pl.BlockSpec((tm, w), lambda i: (i, 0))
    idx = lambda: pl.BlockSpec((1, 1, tm), lambda i: (i, 0, 0), memory_space=pltpu.SMEM)
    return pl.pallas_call(
        body,
        grid=(t // tm,),
        in_specs=[idx(), idx(), pl.BlockSpec(memory_space=pl.ANY), row(LANES), row(D_MODEL), row(PLE_DIM)] + in_tail,
        out_specs=out_specs, out_shape=out_shape,
        scratch_shapes=[pltpu.VMEM((2, tm, D_MODEL), F32), pltpu.SemaphoreType.DMA(())],
        compiler_params=_params(1), name=name,
    )(pos1.reshape(t // tm, 1, tm), pos2.reshape(t // tm, 1, tm), y_sorted, route, h1, p2, *vecs_and_weights)


def _ple_mid(moe_in, h1, p2, ple_gain, wgate, wproj, kv_gain, wkv, q_gain, wq):
    t = h1.shape[0]
    tm = ROW_TILE
    row = lambda w: pl.BlockSpec((tm, w), lambda i: (i, 0))
    vec = _full((1, D_MODEL))
    return _ple_call(
        _ple_mid_kernel, "moe_combine_ple_kv_q", moe_in, h1, p2,
        (ple_gain.reshape(1, -1), wgate, wproj, kv_gain.reshape(1, -1), wkv, q_gain.reshape(1, -1), wq),
        [vec, _full((D_MODEL, D_MODEL)), _full((PLE_DIM, D_MODEL)), vec, _full((D_MODEL, 2 * D_MODEL)), vec,
         _full((D_MODEL, D_MODEL))],
        [row(D_MODEL), row(2 * D_MODEL), row(D_MODEL)],
        [jax.ShapeDtypeStruct((t, D_MODEL), F32), jax.ShapeDtypeStruct((t, 2 * D_MODEL), BF16),
         jax.ShapeDtypeStruct((t, D_MODEL), BF16)])


def _ple_last(moe_in, h1, p2, ple_gain, wgate, wproj, final_gain):
    t = h1.shape[0]
    tm = ROW_TILE
    vec = _full((1, D_MODEL))
    return _ple_call(
        _ple_last_kernel, "moe_combine_ple_final_norm", moe_in, h1, p2,
        (ple_gain.reshape(1, -1), wgate, wproj, final_gain.reshape(1, -1)),
        [vec, _full((D_MODEL, D_MODEL)), _full((PLE_DIM, D_MODEL)), vec],
        pl.BlockSpec((tm, D_MODEL), lambda i: (i, 0)),
        jax.ShapeDtypeStruct((t, D_MODEL), F32))


def _router_operands(w_group, b_group, w_router, b_router):
    pad = LANES - MOE_EXPERTS
    wg = jnp.pad(jnp.repeat(w_group, MOE_EXPERTS_PER_GROUP, axis=1), ((0, 0), (0, pad)))
    bg = jnp.pad(jnp.repeat(b_group, MOE_EXPERTS_PER_GROUP), (0, pad)).reshape(1, LANES)
    we = jnp.pad(w_router, ((0, 0), (0, pad)))
    be = jnp.pad(b_router, (0, pad)).reshape(1, LANES)
    return wg, bg, we, be


def _moe_layer(att, h, wo, i, ffn_norm, w_group, b_group, w_router, b_router, w_gate, w_up, w_down):
    t = h.shape[0]
    h1, xn, route, counts8 = _attn_out(att, h, wo.astype(BF16), ffn_norm[i],
                                       *_router_operands(w_group[i], b_group[i], w_router[i], b_router[i]))
    pos1, pos2, tile_expert, n_tiles = _routing_tables(route, counts8, t)
    x_sorted = _scatter_rows(xn, pos1, pos2, n_tiles * GROUP_TILE)
    wg = w_gate[i].reshape(MOE_EXPERTS, D_MODEL, MOE_D_FF).astype(BF16)
    wu = w_up[i].reshape(MOE_EXPERTS, D_MODEL, MOE_D_FF).astype(BF16)
    wd = w_down[i].reshape(MOE_EXPERTS, MOE_D_FF, D_MODEL).astype(BF16)
    y_sorted = _moe_group(x_sorted, tile_expert, wg, wu, wd)
    return (pos1, pos2, y_sorted, route), h1


def kernel(x, p, rel_bias, attn_norm_a, w_qkv_a, w_o_a, kv_norm, w_kv, attn_norm_b, w_q_b, lambda_q1, lambda_k1,
           lambda_q2, lambda_k2, subln_b, w_o_b, ffn_norm, w_group, b_group, w_router, b_router, w_gate, w_up,
           w_down, ple_norm, w_ple_gate, w_ple_proj, final_norm):
    batch, seq, d = x.shape
    assert d == D_MODEL and seq % MOBA_BLOCK == 0 and seq // MOBA_BLOCK <= 16
    t = batch * seq
    assert t % ROW_TILE == 0
    n_hp = D_MODEL // LANES
    moe = (ffn_norm, w_group, b_group, w_router, b_router, w_gate, w_up, w_down)

    h = x.reshape(t, d)
    qkv, kmean = _qkv_proj(h, attn_norm_a[0], w_qkv_a[0].astype(BF16))
    qkv3 = qkv.reshape(batch, seq, 3 * d)
    att = _attention("moba", qkv3, qkv3, rel_bias, kmean, batch=batch, seq=seq,
                     q_col=0, k_col=n_hp, v_col=2 * n_hp)
    moe_out, h = _moe_layer(att.reshape(t, d), h, w_o_a[0], 0, *moe)
    h, kv, q = _ple_mid(moe_out, h, p[0].reshape(t, PLE_DIM), ple_norm[0], w_ple_gate[0].astype(BF16),
                        w_ple_proj[0].astype(BF16), kv_norm, w_kv.astype(BF16), attn_norm_b[0],
                        w_q_b[0].astype(BF16))

    lam_init = 0.8 - 0.6 * math.exp(-0.3 * 1)
    lam_rows = jnp.pad(jnp.stack([lambda_q1[0], lambda_k1[0], lambda_q2[0], lambda_k2[0]]).astype(F32),
                       ((0, 4), (0, LANES - HEAD_DIM)))
    att = _attention("diff", q.reshape(batch, seq, d), kv.reshape(batch, seq, 2 * d), rel_bias,
                     (lam_rows, subln_b[0].reshape(1, LANES)), batch=batch, seq=seq,
                     q_col=0, k_col=0, v_col=n_hp, lam_init=lam_init)
    moe_out, h = _moe_layer(att.reshape(t, d), h, w_o_b[0], 1, *moe)
    out = _ple_last(moe_out, h, p[1].reshape(t, PLE_DIM), ple_norm[1], w_ple_gate[1].astype(BF16),
                    w_ple_proj[1].astype(BF16), final_norm)
    return out.reshape(batch, seq, d)
```

```python
import functools
import math

import jax
import jax.numpy as jnp
from jax import lax
from jax.experimental import pallas as pl
from jax.experimental.pallas import tpu as pltpu

F32 = jnp.float32
BF16 = jnp.bfloat16

D_MODEL = 1024
DEPTH = 2
N_A_LAYERS = DEPTH // 2
HEAD_DIM = 64
LANES = 128
MOBA_BLOCK = 256
MOBA_TOP_K = 3
REL_BUCKETS = 32
REL_MAX_DISTANCE = 128
MOE_GROUPS = 4
MOE_EXPERTS_PER_GROUP = 8
MOE_EXPERTS = MOE_GROUPS * MOE_EXPERTS_PER_GROUP
MOE_D_FF = D_MODEL // 4
PLE_DIM = 256
RMS_EPS = 1e-6
NEG = -1e30
LOG2E = math.log2(math.e)

ROW_TILE = 512
GROUP_TILE = 512
VMEM_LIMIT = 52 * 1024 * 1024


def _dot(a, b):
    return jnp.dot(a, b, preferred_element_type=F32)


def _rms(x, gain):
    y = x * lax.rsqrt(jnp.mean(x * x, axis=-1, keepdims=True) + RMS_EPS)
    return y * gain


def _params(n_axes):
    return pltpu.CompilerParams(dimension_semantics=("arbitrary",) * n_axes,
                                vmem_limit_bytes=VMEM_LIMIT)


def _full(shape):
    nd = len(shape)
    return pl.BlockSpec(shape, lambda *_: (0,) * nd)


def _qkv_kernel(x_ref, g_ref, w_ref, qkv_ref, kmean_ref):
    hn = _rms(x_ref[...], g_ref[...]).astype(BF16)
    for c in range(3):
        y = _dot(hn, w_ref[:, c * D_MODEL:(c + 1) * D_MODEL])
        qkv_ref[:, c * D_MODEL:(c + 1) * D_MODEL] = y.astype(BF16)
        if c == 1:
            nb = y.shape[0] // MOBA_BLOCK
            kmean_ref[...] = jnp.mean(y.reshape(nb, MOBA_BLOCK, D_MODEL), axis=1, keepdims=True)


def _qkv_proj(x2, gain, w_bf16):
    t = x2.shape[0]
    tm = ROW_TILE
    return pl.pallas_call(
        _qkv_kernel,
        grid=(t // tm,),
        in_specs=[pl.BlockSpec((tm, D_MODEL), lambda i: (i, 0)),
                  _full((1, D_MODEL)),
                  _full((D_MODEL, 3 * D_MODEL))],
        out_specs=[pl.BlockSpec((tm, 3 * D_MODEL), lambda i: (i, 0)),
                   pl.BlockSpec((tm // MOBA_BLOCK, 1, D_MODEL), lambda i: (i, 0, 0))],
        out_shape=[jax.ShapeDtypeStruct((t, 3 * D_MODEL), BF16),
                   jax.ShapeDtypeStruct((t // MOBA_BLOCK, 1, D_MODEL), F32)],
        compiler_params=_params(1),
        name="qkv_proj",
    )(x2, gain.reshape(1, D_MODEL), w_bf16)


def _rel_bucket(dist):
    n = jnp.maximum(dist, 0)
    max_exact = REL_BUCKETS // 2
    nf = jnp.maximum(n, max_exact).astype(F32)
    large = max_exact + (jnp.log(nf / max_exact) / math.log(REL_MAX_DISTANCE / max_exact)
                         * (REL_BUCKETS - max_exact)).astype(jnp.int32)
    large = jnp.minimum(large, REL_BUCKETS - 1)
    return jnp.where(n < max_exact, n, large)


def _build_bias(tab_ref, bias_ref, col0):
    tq = MOBA_BLOCK
    rows = 64
    for r0 in range(0, 2 * tq, rows):
        key = lax.broadcasted_iota(jnp.int32, (rows, tq), 0) + r0
        qry = lax.broadcasted_iota(jnp.int32, (rows, tq), 1)
        dist = qry + tq - key
        bkt = _rel_bucket(dist)
        for h in range(2):
            acc = jnp.zeros((rows, tq), F32)
            for i in range(REL_BUCKETS):
                acc = jnp.where(bkt == i, tab_ref[i, col0 + h], acc)
            far = tab_ref[REL_BUCKETS - 1, col0 + h]
            bias_ref[h, r0:r0 + rows, :] = jnp.where(dist >= 0, (acc - far) * LOG2E, NEG)


def _build_kv(k_ref, v_ref, ka_ref, vt_ref, seq, mask_v):
    tq = MOBA_BLOCK
    nb = seq // tq
    lane = lax.broadcasted_iota(jnp.int32, (tq, LANES), 1)
    first = lane < HEAD_DIM
    ka_ref[0, 0:tq, :] = (lane == HEAD_DIM + nb).astype(F32).astype(BF16)
    ka_ref[1, 0:tq, :] = (lane == nb).astype(F32).astype(BF16)
    vt_ref[0] = jnp.zeros((LANES, tq), BF16)
    if mask_v:
        vt_ref[nb + 1] = jnp.zeros((LANES, tq), BF16)

    def body(j, _):
        r = pl.multiple_of(j * tq, tq)
        k = k_ref[0, pl.ds(r, tq), :].astype(F32)
        oh0 = (lane == HEAD_DIM + j).astype(F32)
        oh1 = (lane == j).astype(F32)
        ka_ref[0, pl.ds(r + tq, tq), :] = jnp.where(first, k, oh0).astype(BF16)
        ka_ref[1, pl.ds(r + tq, tq), :] = jnp.where(first, oh1, k).astype(BF16)
        vf = v_ref[0, pl.ds(r, tq), :].astype(F32)
        if mask_v:
            vt_ref[j + 1] = jnp.where(first, vf, 0.0).T.astype(BF16)
            vt_ref[nb + 2 + j] = jnp.where(first, 0.0, vf).T.astype(BF16)
        else:
            vt_ref[j + 1] = vf.T.astype(BF16)
        return 0

    lax.fori_loop(0, nb, body, 0)


def _moba_attend(gate, qi):
    n = lax.broadcasted_iota(jnp.int32, gate.shape, 0)
    rem = n < qi
    sel = n == qi
    for _ in range(MOBA_TOP_K):
        gm = jnp.where(rem, gate, -jnp.inf)
        mx = jnp.max(gm, axis=0, keepdims=True)
        cand = rem & (gm == mx)
        idx = jnp.min(jnp.where(cand, n, 1 << 20), axis=0, keepdims=True)
        pick = cand & (n == idx)
        sel = sel | pick
        rem = rem & jnp.logical_not(pick)
    return sel


def _augment_queries(qs_t, pens, nb):
    tq = qs_t.shape[1]
    out = []
    for h in range(2):
        if pens is not None:
            pen16 = pens[h]
        else:
            pen16 = jnp.where(lax.broadcasted_iota(jnp.int32, (16, tq), 0) >= nb, NEG, 0.0)
        tail = [pen16, jnp.full((8, tq), NEG, F32), jnp.zeros((HEAD_DIM - 24, tq), F32)]
        parts = [qs_t[0:HEAD_DIM]] + tail if h == 0 else tail + [qs_t[HEAD_DIM:]]
        out.append(jnp.concatenate(parts, axis=0).astype(BF16))
    return out


def _flash(qa, ka_ref, vt_ref, bias_ref, qi, v_base, bufs):
    tq = MOBA_BLOCK
    s_a, s_b, m_ref, l_ref, acc_ref = bufs
    n_past = lax.shift_right_logical(qi, 1)

    def pair_block(j):
        return qi - 2 * j

    def scores_into(buf, j, bias=False):
        r = pl.multiple_of(pair_block(j) * tq, tq)
        for h in range(2):
            s = _dot(ka_ref[h, pl.ds(r, 2 * tq), :], qa[h])
            buf[h] = s + bias_ref[h] if bias else s

    def update(buf, j):
        blk0 = pair_block(j)
        for h in range(2):
            s = buf[h]
            m = m_ref[h]
            m_new = jnp.maximum(m, jnp.max(s, axis=0, keepdims=True))
            alpha = jnp.exp2(m - m_new)
            p = jnp.exp2(s - m_new)
            pb = p.astype(BF16)
            pv = _dot(vt_ref[v_base[h] + blk0], pb[0:tq]) + _dot(vt_ref[v_base[h] + blk0 + 1], pb[tq:])
            m_ref[h] = m_new
            l_ref[h] = alpha * l_ref[h] + jnp.sum(p, axis=0, keepdims=True)
            acc_ref[h] = alpha * acc_ref[h] + pv

    m_ref[...] = jnp.full(m_ref.shape, 3.0 * NEG, F32)
    l_ref[...] = jnp.zeros(l_ref.shape, F32)
    acc_ref[...] = jnp.zeros(acc_ref.shape, F32)
    scores_into(s_a, 0, bias=True)

    def body(t, carry):
        scores_into(s_b, 2 * t + 1)
        update(s_a, 2 * t)
        scores_into(s_a, 2 * t + 2)
        update(s_b, 2 * t + 1)
        return carry

    n_double = lax.shift_right_logical(n_past, 1)
    lax.fori_loop(0, n_double, body, 0)
    odd = (n_past & 1) == 1

    @pl.when(odd)
    def _():
        scores_into(s_b, n_past)
        update(s_a, n_past - 1)
        update(s_b, n_past)

    @pl.when(jnp.logical_not(odd))
    def _():
        update(s_a, n_past)

    return [(acc_ref[h], l_ref[h]) for h in range(2)]


def _moba_kernel(tab_ref, q_ref, k_ref, v_ref, km_ref, o_ref, ka_ref, vt_ref, bias_ref, *bufs, seq):
    hp, b, qi = pl.program_id(0), pl.program_id(1), pl.program_id(2)
    nb = seq // MOBA_BLOCK

    @pl.when((b == 0) & (qi == 0))
    def _():
        _build_bias(tab_ref, bias_ref, 2 * hp)

    @pl.when(qi == 0)
    def _():
        _build_kv(k_ref, v_ref, ka_ref, vt_ref, seq, mask_v=True)

    q_t = q_ref[0].astype(F32).T
    km = km_ref[:, 0, :]
    lane_k = lax.broadcasted_iota(jnp.int32, (nb, LANES), 1)
    km0 = jnp.where(lane_k < HEAD_DIM, km, 0.0)
    km1 = jnp.where(lane_k < HEAD_DIM, 0.0, km)
    pad = jnp.zeros((HEAD_DIM - nb, LANES), F32)
    kmx = jnp.concatenate([km1, pad, km0, pad], axis=0)
    kmx_hi = kmx.astype(BF16)
    kmx_lo = (kmx - kmx_hi.astype(F32)).astype(BF16)
    q_bf = q_t.astype(BF16)
    gate = _dot(kmx_hi, q_bf) + _dot(kmx_lo, q_bf)
    pens = [jnp.where(_moba_attend(gate[base:base + 16], qi), 0.0, NEG) for base in (HEAD_DIM, 0)]

    qa = _augment_queries(q_t * (HEAD_DIM ** -0.5 * LOG2E), pens, nb)
    (a0, l0), (a1, l1) = _flash(qa, ka_ref, vt_ref, bias_ref, qi, (0, nb + 1), bufs)
    o_ref[0] = (a0 / l0 + a1 / l1).T.astype(o_ref.dtype)


def _diff_kernel(tab_ref, lam_ref, q_ref, k_ref, v_ref, sg_ref, o_ref, ka_ref, vt_ref, bias_ref, *bufs, seq,
                 lam_init):
    hd, b, qi = pl.program_id(0), pl.program_id(1), pl.program_id(2)

    @pl.when((b == 0) & (qi == 0))
    def _():
        _build_bias(tab_ref, bias_ref, 2 * hd)

    @pl.when(qi == 0)
    def _():
        _build_kv(k_ref, v_ref, ka_ref, vt_ref, seq, mask_v=False)

    lv = lam_ref[...]
    lam = (jnp.exp(jnp.sum(lv[0:1] * lv[1:2], axis=1, keepdims=True))
           - jnp.exp(jnp.sum(lv[2:3] * lv[3:4], axis=1, keepdims=True)) + lam_init)

    q_t = q_ref[0].astype(F32).T
    qa = _augment_queries(q_t * (HEAD_DIM ** -0.5 * LOG2E), None, seq // MOBA_BLOCK)
    (a0, l0), (a1, l1) = _flash(qa, ka_ref, vt_ref, bias_ref, qi, (0, 0), bufs)
    att = a0 / l0 - lam * (a1 / l1)
    y = att * lax.rsqrt(jnp.mean(att * att, axis=0, keepdims=True) + RMS_EPS)
    o_ref[0] = ((y.T * sg_ref[...]) * (1.0 - lam_init)).astype(o_ref.dtype)


def _attention(kind, q_src, kv_src, rel_bias, extra, *, batch, seq, q_col, k_col, v_col, lam_init=None):
    tq = MOBA_BLOCK
    nb = seq // tq
    n_hp = D_MODEL // LANES
    grid = (n_hp, batch, nb)
    smem = pl.BlockSpec(memory_space=pltpu.SMEM)
    q_spec = pl.BlockSpec((1, tq, LANES), lambda h, b, i: (b, i, q_col + h))
    k_spec = pl.BlockSpec((1, seq, LANES), lambda h, b, i: (b, 0, k_col + h))
    v_spec = pl.BlockSpec((1, seq, LANES), lambda h, b, i: (b, 0, v_col + h))
    o_spec = pl.BlockSpec((1, tq, LANES), lambda h, b, i: (b, i, h))
    scratch_k = pltpu.VMEM((2, seq + tq, LANES), BF16)
    bias_s = pltpu.VMEM((2, 2 * tq, tq), F32)
    flash_bufs = [pltpu.VMEM((2, 2 * tq, tq), F32), pltpu.VMEM((2, 2 * tq, tq), F32),
                  pltpu.VMEM((2, 1, tq), F32), pltpu.VMEM((2, 1, tq), F32), pltpu.VMEM((2, LANES, tq), F32)]
    out_shape = jax.ShapeDtypeStruct((batch, seq, D_MODEL), BF16)
    if kind == "moba":
        kmean = extra
        km_spec = pl.BlockSpec((nb, 1, LANES), lambda h, b, i: (b, 0, h))
        return pl.pallas_call(
            functools.partial(_moba_kernel, seq=seq),
            grid=grid,
            in_specs=[smem, q_spec, k_spec, v_spec, km_spec],
            out_specs=o_spec, out_shape=out_shape,
            scratch_shapes=[scratch_k, pltpu.VMEM((2 * (nb + 1), LANES, tq), BF16), bias_s] + flash_bufs,
            compiler_params=_params(3), name="moba_attention",
        )(rel_bias, q_src, kv_src, kv_src, kmean)
    lam_rows, sub_gain = extra
    return pl.pallas_call(
        functools.partial(_diff_kernel, seq=seq, lam_init=lam_init),
        grid=grid,
        in_specs=[smem, _full((8, LANES)), q_spec, k_spec, v_spec, _full((1, LANES))],
        out_specs=o_spec, out_shape=out_shape,
        scratch_shapes=[scratch_k, pltpu.VMEM((nb + 1, LANES, tq), BF16), bias_s] + flash_bufs,
        compiler_params=_params(3), name="diff_attention",
    )(rel_bias, lam_rows, q_src, kv_src, kv_src, sub_gain)


def _router(xn, w_hi_ref, w_lo_ref, b_ref, tri_ref, count_ref):
    lane = lax.broadcasted_iota(jnp.int32, (xn.shape[0], LANES), 1)
    live = lane < MOE_EXPERTS
    x_hi = xn.astype(BF16)
    x_lo = (xn - x_hi.astype(F32)).astype(BF16)
    logits = (_dot(x_hi, w_hi_ref[...]) + (_dot(x_hi, w_lo_ref[...]) + _dot(x_lo, w_hi_ref[...]))) + b_ref[...]
    gl = jnp.where(live, logits, -jnp.inf)
    el = pltpu.roll(logits, LANES // 2, axis=1)
    gmax = jnp.max(gl, axis=1, keepdims=True)
    gsum = jnp.sum(jnp.exp(gl - gmax), axis=1, keepdims=True) / MOE_EXPERTS_PER_GROUP
    g_gate = 1.0 / gsum
    first = jnp.min(jnp.where(gl == gmax, lane, 1 << 20), axis=1, keepdims=True)
    group_shift = MOE_EXPERTS_PER_GROUP.bit_length() - 1
    in_group = jnp.right_shift(lane, group_shift) == jnp.right_shift(first, group_shift)
    em = jnp.where(in_group & live, el, -jnp.inf)
    m1 = jnp.max(em, axis=1, keepdims=True)
    i1 = jnp.min(jnp.where(em == m1, lane, 1 << 20), axis=1, keepdims=True)
    em2 = jnp.where(lane == i1, -jnp.inf, em)
    m2 = jnp.max(em2, axis=1, keepdims=True)
    i2 = jnp.min(jnp.where(em2 == m2, lane, 1 << 20), axis=1, keepdims=True)
    p2 = jnp.exp(m2 - m1)
    w1 = g_gate * (1.0 / (1.0 + p2))
    w2 = g_gate * (p2 / (1.0 + p2))

    is1, is2 = lane == i1, lane == i2
    member = jnp.where(is1 | is2, 1.0, 0.0)
    before = _dot(tri_ref[...], member.astype(BF16)) + count_ref[...]
    rank1 = jnp.sum(jnp.where(is1, before, 0.0), axis=1, keepdims=True)
    rank2 = jnp.sum(jnp.where(is2, before, 0.0), axis=1, keepdims=True)
    count_ref[...] = count_ref[...] + jnp.sum(member, axis=0, keepdims=True)
    fields = (i1.astype(F32), i2.astype(F32), w1, w2, rank1, rank2)
    route = jnp.zeros(lane.shape, F32)
    for k, col in enumerate(fields):
        route = jnp.where(lane == k, col, route)
    return route


def _attn_out_kernel(a_ref, h_ref, wo_ref, fg_ref, w_hi_ref, w_lo_ref, b_ref, tri_ref,
                     h1_ref, xn_ref, route_ref, counts_ref, count_acc):
    @pl.when(pl.program_id(0) == 0)
    def _():
        count_acc[...] = jnp.zeros_like(count_acc)

    h1 = h_ref[...] + _dot(a_ref[...], wo_ref[...])
    h1_ref[...] = h1
    xn = _rms(h1, fg_ref[...])
    xn_ref[...] = xn
    route_ref[...] = _router(xn, w_hi_ref, w_lo_ref, b_ref, tri_ref, count_acc)
    counts_ref[...] = jnp.broadcast_to(count_acc[...], counts_ref.shape)


def _attn_out(att2, h2, wo_bf16, ffn_gain, w_hi, w_lo, b_x):
    t = h2.shape[0]
    tm = ROW_TILE
    row = lambda w: pl.BlockSpec((tm, w), lambda i: (i, 0))
    tri = jnp.tril(jnp.ones((tm, tm), F32), -1).astype(BF16)
    return pl.pallas_call(
        _attn_out_kernel,
        grid=(t // tm,),
        in_specs=[row(D_MODEL), row(D_MODEL), _full((D_MODEL, D_MODEL)), _full((1, D_MODEL)),
                  _full((D_MODEL, LANES)), _full((D_MODEL, LANES)), _full((1, LANES)), _full((tm, tm))],
        out_specs=[row(D_MODEL), row(D_MODEL), row(LANES), _full((8, LANES))],
        out_shape=[jax.ShapeDtypeStruct((t, D_MODEL), F32), jax.ShapeDtypeStruct((t, D_MODEL), F32),
                   jax.ShapeDtypeStruct((t, LANES), F32), jax.ShapeDtypeStruct((8, LANES), F32)],
        scratch_shapes=[pltpu.VMEM((1, LANES), F32)],
        compiler_params=_params(1), name="attn_out_router",
    )(att2, h2, wo_bf16, ffn_gain.reshape(1, D_MODEL), w_hi, w_lo, b_x, tri)


def _scatter_kernel(last_row_ref, tiles_ref, pos1_ref, pos2_ref, xn_ref, out_hbm, zeros_ref, sem):
    tm = pos1_ref.shape[2]

    @pl.when(pl.program_id(0) == 0)
    def _():
        zeros_ref[...] = jnp.zeros_like(zeros_ref)

        def tile_copy(e):
            row = pl.multiple_of(last_row_ref[e], GROUP_TILE)
            return pltpu.make_async_copy(zeros_ref, out_hbm.at[pl.ds(row, GROUP_TILE)], sem)

        for e in range(MOE_EXPERTS):
            @pl.when(tiles_ref[e] > 0)
            def _():
                tile_copy(e).start()

        for e in range(MOE_EXPERTS):
            @pl.when(tiles_ref[e] > 0)
            def _():
                tile_copy(e).wait()

    def issue(r, carry):
        src = xn_ref.at[pl.ds(r, 1)]
        pltpu.make_async_copy(src, out_hbm.at[pl.ds(pos1_ref[0, 0, r], 1)], sem).start()
        pltpu.make_async_copy(src, out_hbm.at[pl.ds(pos2_ref[0, 0, r], 1)], sem).start()
        return carry

    lax.fori_loop(0, tm, issue, 0)
    for _ in range(2):
        pltpu.make_async_copy(xn_ref, out_hbm.at[pl.ds(0, tm)], sem).wait()


def _scatter_rows(xn, pos1, pos2, last_row, tiles_per, n_rows):
    t = xn.shape[0]
    tm = ROW_TILE
    idx = lambda: pl.BlockSpec((1, 1, tm), lambda i, *_: (i, 0, 0), memory_space=pltpu.SMEM)
    return pl.pallas_call(
        _scatter_kernel,
        grid_spec=pltpu.PrefetchScalarGridSpec(
            num_scalar_prefetch=2, grid=(t // tm,),
            in_specs=[idx(), idx(), pl.BlockSpec((tm, D_MODEL), lambda i, *_: (i, 0))],
            out_specs=pl.BlockSpec(memory_space=pl.ANY),
            scratch_shapes=[pltpu.VMEM((GROUP_TILE, D_MODEL), F32), pltpu.SemaphoreType.DMA(())]),
        out_shape=jax.ShapeDtypeStruct((n_rows, D_MODEL), F32),
        compiler_params=_params(1), name="moe_scatter",
    )(last_row, tiles_per, pos1.reshape(t // tm, 1, tm), pos2.reshape(t // tm, 1, tm), xn)


def _moe_group_kernel(te_ref, nt_ref, x_ref, wg_ref, wu_ref, wd_ref, y_ref):
    del te_ref, nt_ref
    x = x_ref[...].astype(BF16)
    g = _dot(x, wg_ref[0])
    u = _dot(x, wu_ref[0])
    hh = (g * jax.nn.sigmoid(g)) * u
    y_ref[...] = _dot(hh.astype(BF16), wd_ref[0])


def _moe_group(x_sorted, tile_expert, n_used, wg, wu, wd):
    n_tiles = tile_expert.shape[0]
    tmg = GROUP_TILE
    wspec = lambda a, b: pl.BlockSpec((1, a, b), lambda w, te, nt: (te[w], 0, 0))
    rows = lambda: pl.BlockSpec((tmg, D_MODEL), lambda w, te, nt: (jnp.minimum(w, nt[0] - 1), 0))
    return pl.pallas_call(
        _moe_group_kernel,
        grid_spec=pltpu.PrefetchScalarGridSpec(
            num_scalar_prefetch=2, grid=(n_tiles,),
            in_specs=[rows(), wspec(D_MODEL, MOE_D_FF), wspec(D_MODEL, MOE_D_FF), wspec(MOE_D_FF, D_MODEL)],
            out_specs=rows()),
        out_shape=jax.ShapeDtypeStruct((n_tiles * tmg, D_MODEL), F32),
        compiler_params=_params(1), name="moe_group_ffn",
    )(tile_expert, n_used, x_sorted, wg, wu, wd)


def _routing_tables(route, counts8, t):
    n_tiles = 2 * t // GROUP_TILE + MOE_EXPERTS
    counts = counts8[0, :MOE_EXPERTS].astype(jnp.int32)
    tiles_per = (counts + GROUP_TILE - 1) // GROUP_TILE
    tile_end = jnp.cumsum(tiles_per)
    offsets = (tile_end - tiles_per) * GROUP_TILE
    last_row = jnp.maximum(tile_end - 1, 0) * GROUP_TILE
    ids = route[:, 0:2].astype(jnp.int32)
    ranks = route[:, 4:6].astype(jnp.int32)
    expert_ids = jnp.arange(MOE_EXPERTS, dtype=jnp.int32)
    pos = jnp.sum(jnp.where(ids[:, :, None] == expert_ids, offsets, 0), axis=2) + ranks
    tile_ids = jnp.arange(n_tiles, dtype=jnp.int32)
    tile_expert = jnp.minimum(jnp.sum((tile_end[None, :] <= tile_ids[:, None]).astype(jnp.int32), axis=1),
                              MOE_EXPERTS - 1)
    return pos[:, 0], pos[:, 1], tile_expert, tile_end[-1:], last_row, tiles_per, n_tiles


def _start_gather(pos1_ref, pos2_ref, y_hbm, ybuf, sem):
    tm = pos1_ref.shape[2]

    def issue(r, carry):
        pltpu.make_async_copy(y_hbm.at[pl.ds(pos1_ref[0, 0, r], 1)], ybuf.at[0, pl.ds(r, 1)], sem).start()
        pltpu.make_async_copy(y_hbm.at[pl.ds(pos2_ref[0, 0, r], 1)], ybuf.at[1, pl.ds(r, 1)], sem).start()
        return carry

    lax.fori_loop(0, tm, issue, 0)


def _wait_gather(y_hbm, ybuf, sem):
    tm = ybuf.shape[1]
    for slot in range(2):
        pltpu.make_async_copy(y_hbm.at[pl.ds(0, tm)], ybuf.at[slot], sem).wait()


def _moe_ple_update(pos1_ref, pos2_ref, y_hbm, route_ref, h_ref, p_ref, pg_ref, wgate_ref, wproj_ref, ybuf, sem):
    _start_gather(pos1_ref, pos2_ref, y_hbm, ybuf, sem)
    proj = _dot(p_ref[...].astype(BF16), wproj_ref[...])
    route = route_ref[...]
    _wait_gather(y_hbm, ybuf, sem)
    h = h_ref[...] + (route[:, 2:3] * ybuf[0] + route[:, 3:4] * ybuf[1])
    gate = jax.nn.sigmoid(_dot(_rms(h, pg_ref[...]).astype(BF16), wgate_ref[...]))
    return h + gate * proj


def _ple_mid_kernel(pos1_ref, pos2_ref, y_hbm, route_ref, h_ref, p_ref, pg_ref, wgate_ref, wproj_ref,
                    kvg_ref, wkv_ref, qg_ref, wq_ref, h3_ref, kv_ref, q_ref, ybuf, sem):
    h3 = _moe_ple_update(pos1_ref, pos2_ref, y_hbm, route_ref, h_ref, p_ref, pg_ref, wgate_ref, wproj_ref, ybuf, sem)
    h3_ref[...] = h3
    kn = _rms(h3, kvg_ref[...]).astype(BF16)
    for c in range(2):
        kv_ref[:, c * D_MODEL:(c + 1) * D_MODEL] = _dot(kn, wkv_ref[:, c * D_MODEL:(c + 1) * D_MODEL]).astype(BF16)
    q_ref[...] = _dot(_rms(h3, qg_ref[...]).astype(BF16), wq_ref[...]).astype(BF16)


def _ple_last_kernel(pos1_ref, pos2_ref, y_hbm, route_ref, h_ref, p_ref, pg_ref, wgate_ref, wproj_ref,
                     fg_ref, o_ref, ybuf, sem):
    h3 = _moe_ple_update(pos1_ref, pos2_ref, y_hbm, route_ref, h_ref, p_ref, pg_ref, wgate_ref, wproj_ref, ybuf, sem)
    o_ref[...] = _rms(h3, fg_ref[...])


def _ple_call(body, name, moe_in, h1, p2, vecs_and_weights, in_tail, out_specs, out_shape):
    pos1, pos2, y_sorted, route = moe_in
    t = h1.shape[0]
    tm = ROW_TILE
    row = lambda w: pl.BlockSpec((tm, w), lambda i: (i, 0))
    idx = lambda: pl.BlockSpec((1, 1, tm), lambda i: (i, 0, 0), memory_space=pltpu.SMEM)
    return pl.pallas_call(
        body,
        grid=(t // tm,),
        in_specs=[idx(), idx(), pl.BlockSpec(memory_space=pl.ANY), row(LANES), row(D_MODEL), row(PLE_DIM)] + in_tail,
        out_specs=out_specs, out_shape=out_shape,
        scratch_shapes=[pltpu.VMEM((2, tm, D_MODEL), F32), pltpu.SemaphoreType.DMA(())],
        compiler_params=_params(1), name=name,
    )(pos1.reshape(t // tm, 1, tm), pos2.reshape(t // tm, 1, tm), y_sorted, route, h1, p2, *vecs_and_weights)


def _ple_mid(moe_in, h1, p2, ple_gain, wgate, wproj, kv_gain, wkv, q_gain, wq):
    t = h1.shape[0]
    tm = ROW_TILE
    row = lambda w: pl.BlockSpec((tm, w), lambda i: (i, 0))
    vec = _full((1, D_MODEL))
    return _ple_call(
        _ple_mid_kernel, "moe_combine_ple_kv_q", moe_in, h1, p2,
        (ple_gain.reshape(1, -1), wgate, wproj, kv_gain.reshape(1, -1), wkv, q_gain.reshape(1, -1), wq),
        [vec, _full((D_MODEL, D_MODEL)), _full((PLE_DIM, D_MODEL)), vec, _full((D_MODEL, 2 * D_MODEL)), vec,
         _full((D_MODEL, D_MODEL))],
        [row(D_MODEL), row(2 * D_MODEL), row(D_MODEL)],
        [jax.ShapeDtypeStruct((t, D_MODEL), F32), jax.ShapeDtypeStruct((t, 2 * D_MODEL), BF16),
         jax.ShapeDtypeStruct((t, D_MODEL), BF16)])


def _ple_last(moe_in, h1, p2, ple_gain, wgate, wproj, final_gain):
    t = h1.shape[0]
    tm = ROW_TILE
    vec = _full((1, D_MODEL))
    return _ple_call(
        _ple_last_kernel, "moe_combine_ple_final_norm", moe_in, h1, p2,
        (ple_gain.reshape(1, -1), wgate, wproj, final_gain.reshape(1, -1)),
        [vec, _full((D_MODEL, D_MODEL)), _full((PLE_DIM, D_MODEL)), vec],
        pl.BlockSpec((tm, D_MODEL), lambda i: (i, 0)),
        jax.ShapeDtypeStruct((t, D_MODEL), F32))


def _router_operands(w_group, b_group, w_router, b_router):
    gap = LANES // 2 - MOE_EXPERTS
    w = jnp.pad(jnp.concatenate([jnp.repeat(w_group, MOE_EXPERTS_PER_GROUP, axis=1),
                                 jnp.zeros((D_MODEL, gap), F32), w_router], axis=1), ((0, 0), (0, gap)))
    b = jnp.pad(jnp.concatenate([jnp.repeat(b_group, MOE_EXPERTS_PER_GROUP), jnp.zeros((gap,), F32), b_router]),
                (0, gap)).reshape(1, LANES)
    w_hi = w.astype(BF16)
    w_lo = (w - w_hi.astype(F32)).astype(BF16)
    return w_hi, w_lo, b


def _moe_layer(att, h, wo, i, ffn_norm, w_group, b_group, w_router, b_router, w_gate, w_up, w_down):
    t = h.shape[0]
    h1, xn, route, counts8 = _attn_out(att, h, wo.astype(BF16), ffn_norm[i],
                                       *_router_operands(w_group[i], b_group[i], w_router[i], b_router[i]))
    pos1, pos2, tile_expert, n_used, last_row, tiles_per, n_tiles = _routing_tables(route, counts8, t)
    x_sorted = _scatter_rows(xn, pos1, pos2, last_row, tiles_per, n_tiles * GROUP_TILE)
    wg = w_gate[i].reshape(MOE_EXPERTS, D_MODEL, MOE_D_FF).astype(BF16)
    wu = w_up[i].reshape(MOE_EXPERTS, D_MODEL, MOE_D_FF).astype(BF16)
    wd = w_down[i].reshape(MOE_EXPERTS, MOE_D_FF, D_MODEL).astype(BF16)
    y_sorted = _moe_group(x_sorted, tile_expert, n_used, wg, wu, wd)
    return (pos1, pos2, y_sorted, route), h1


def kernel(x, p, rel_bias, attn_norm_a, w_qkv_a, w_o_a, kv_norm, w_kv, attn_norm_b, w_q_b, lambda_q1, lambda_k1,
           lambda_q2, lambda_k2, subln_b, w_o_b, ffn_norm, w_group, b_group, w_router, b_router, w_gate, w_up,
           w_down, ple_norm, w_ple_gate, w_ple_proj, final_norm):
    batch, seq, d = x.shape
    assert d == D_MODEL and seq % MOBA_BLOCK == 0 and seq // MOBA_BLOCK <= 16
    t = batch * seq
    assert t % ROW_TILE == 0
    n_hp = D_MODEL // LANES
    moe = (ffn_norm, w_group, b_group, w_router, b_router, w_gate, w_up, w_down)

    h = x.reshape(t, d)
    qkv, kmean = _qkv_proj(h, attn_norm_a[0], w_qkv_a[0].astype(BF16))
    qkv3 = qkv.reshape(batch, seq, 3 * d)
    att = _attention("moba", qkv3, qkv3, rel_bias, kmean, batch=batch, seq=seq,
                     q_col=0, k_col=n_hp, v_col=2 * n_hp)
    moe_out, h = _moe_layer(att.reshape(t, d), h, w_o_a[0], 0, *moe)
    h, kv, q = _ple_mid(moe_out, h, p[0].reshape(t, PLE_DIM), ple_norm[0], w_ple_gate[0].astype(BF16),
                        w_ple_proj[0].astype(BF16), kv_norm, w_kv.astype(BF16), attn_norm_b[0],
                        w_q_b[0].astype(BF16))

    lam_init = 0.8 - 0.6 * math.exp(-0.3 * 1)
    lam_rows = jnp.pad(jnp.stack([lambda_q1[0], lambda_k1[0], lambda_q2[0], lambda_k2[0]]).astype(F32),
                       ((0, 4), (0, LANES - HEAD_DIM)))
    att = _attention("diff", q.reshape(batch, seq, d), kv.reshape(batch, seq, 2 * d), rel_bias,
                     (lam_rows, subln_b[0].reshape(1, LANES)), batch=batch, seq=seq,
                     q_col=0, k_col=0, v_col=n_hp, lam_init=lam_init)
    moe_out, h = _moe_layer(att.reshape(t, d), h, w_o_b[0], 1, *moe)
    out = _ple_last(moe_out, h, p[1].reshape(t, PLE_DIM), ple_norm[1], w_ple_gate[1].astype(BF16),
                    w_ple_proj[1].astype(BF16), final_norm)
    return out.reshape(batch, seq, d)
```

```python
import functools
import math

import jax
import jax.numpy as jnp
from jax import lax
from jax.experimental import pallas as pl
from jax.experimental.pallas import tpu as pltpu

F32 = jnp.float32
BF16 = jnp.bfloat16

D_MODEL = 1024
DEPTH = 2
N_A_LAYERS = DEPTH // 2
HEAD_DIM = 64
LANES = 128
MOBA_BLOCK = 256
MOBA_TOP_K = 3
REL_BUCKETS = 32
REL_MAX_DISTANCE = 128
MOE_GROUPS = 4
MOE_EXPERTS_PER_GROUP = 8
MOE_EXPERTS = MOE_GROUPS * MOE_EXPERTS_PER_GROUP
MOE_D_FF = D_MODEL // 4
PLE_DIM = 256
RMS_EPS = 1e-6
NEG = -1e30
LOG2E = math.log2(math.e)

ROW_TILE = 512
GROUP_TILE = 512
GATHER_CHUNKS = 8
VMEM_LIMIT = 52 * 1024 * 1024


def _dot(a, b):
    return jnp.dot(a, b, preferred_element_type=F32)


def _rms(x, gain):
    y = x * lax.rsqrt(jnp.mean(x * x, axis=-1, keepdims=True) + RMS_EPS)
    return y * gain


def _params(n_axes):
    return pltpu.CompilerParams(dimension_semantics=("arbitrary",) * n_axes,
                                vmem_limit_bytes=VMEM_LIMIT)


def _full(shape):
    nd = len(shape)
    return pl.BlockSpec(shape, lambda *_: (0,) * nd)


def _qkv_kernel(x_ref, g_ref, w_ref, qkv_ref, kmean_ref):
    hn = _rms(x_ref[...], g_ref[...]).astype(BF16)
    for c in range(3):
        y = _dot(hn, w_ref[:, c * D_MODEL:(c + 1) * D_MODEL])
        qkv_ref[:, c * D_MODEL:(c + 1) * D_MODEL] = y.astype(BF16)
        if c == 1:
            nb = y.shape[0] // MOBA_BLOCK
            kmean_ref[...] = jnp.mean(y.reshape(nb, MOBA_BLOCK, D_MODEL), axis=1, keepdims=True)


def _qkv_proj(x2, gain, w_bf16):
    t = x2.shape[0]
    tm = ROW_TILE
    return pl.pallas_call(
        _qkv_kernel,
        grid=(t // tm,),
        in_specs=[pl.BlockSpec((tm, D_MODEL), lambda i: (i, 0)),
                  _full((1, D_MODEL)),
                  _full((D_MODEL, 3 * D_MODEL))],
        out_specs=[pl.BlockSpec((tm, 3 * D_MODEL), lambda i: (i, 0)),
                   pl.BlockSpec((tm // MOBA_BLOCK, 1, D_MODEL), lambda i: (i, 0, 0))],
        out_shape=[jax.ShapeDtypeStruct((t, 3 * D_MODEL), BF16),
                   jax.ShapeDtypeStruct((t // MOBA_BLOCK, 1, D_MODEL), F32)],
        compiler_params=_params(1),
        name="qkv_proj",
    )(x2, gain.reshape(1, D_MODEL), w_bf16)


def _rel_bucket(dist):
    n = jnp.maximum(dist, 0)
    max_exact = REL_BUCKETS // 2
    nf = jnp.maximum(n, max_exact).astype(F32)
    large = max_exact + (jnp.log(nf / max_exact) / math.log(REL_MAX_DISTANCE / max_exact)
                         * (REL_BUCKETS - max_exact)).astype(jnp.int32)
    large = jnp.minimum(large, REL_BUCKETS - 1)
    return jnp.where(n < max_exact, n, large)


def _build_bias(tab_ref, bias_ref, col0):
    tq = MOBA_BLOCK
    rows = 64
    for r0 in range(0, 2 * tq, rows):
        key = lax.broadcasted_iota(jnp.int32, (rows, tq), 0) + r0
        qry = lax.broadcasted_iota(jnp.int32, (rows, tq), 1)
        dist = qry + tq - key
        bkt = _rel_bucket(dist)
        for h in range(2):
            acc = jnp.zeros((rows, tq), F32)
            for i in range(REL_BUCKETS):
                acc = jnp.where(bkt == i, tab_ref[i, col0 + h], acc)
            far = tab_ref[REL_BUCKETS - 1, col0 + h]
            bias_ref[h, r0:r0 + rows, :] = jnp.where(dist >= 0, (acc - far) * LOG2E, NEG)


def _build_kv(k_ref, v_ref, ka_ref, vt_ref, seq, mask_v):
    tq = MOBA_BLOCK
    nb = seq // tq
    lane = lax.broadcasted_iota(jnp.int32, (tq, LANES), 1)
    first = lane < HEAD_DIM
    ka_ref[0, 0:tq, :] = (lane == HEAD_DIM + nb).astype(F32).astype(BF16)
    ka_ref[1, 0:tq, :] = (lane == nb).astype(F32).astype(BF16)
    vt_ref[0] = jnp.zeros((LANES, tq), BF16)
    if mask_v:
        vt_ref[nb + 1] = jnp.zeros((LANES, tq), BF16)

    def body(j, _):
        r = pl.multiple_of(j * tq, tq)
        k = k_ref[0, pl.ds(r, tq), :].astype(F32)
        oh0 = (lane == HEAD_DIM + j).astype(F32)
        oh1 = (lane == j).astype(F32)
        ka_ref[0, pl.ds(r + tq, tq), :] = jnp.where(first, k, oh0).astype(BF16)
        ka_ref[1, pl.ds(r + tq, tq), :] = jnp.where(first, oh1, k).astype(BF16)
        vf = v_ref[0, pl.ds(r, tq), :].astype(F32)
        if mask_v:
            vt_ref[j + 1] = jnp.where(first, vf, 0.0).T.astype(BF16)
            vt_ref[nb + 2 + j] = jnp.where(first, 0.0, vf).T.astype(BF16)
        else:
            vt_ref[j + 1] = vf.T.astype(BF16)
        return 0

    lax.fori_loop(0, nb, body, 0)


def _moba_attend(gate, qi):
    n = lax.broadcasted_iota(jnp.int32, gate.shape, 0)
    rem = n < qi
    sel = n == qi
    for _ in range(MOBA_TOP_K):
        gm = jnp.where(rem, gate, -jnp.inf)
        mx = jnp.max(gm, axis=0, keepdims=True)
        cand = rem & (gm == mx)
        idx = jnp.min(jnp.where(cand, n, 1 << 20), axis=0, keepdims=True)
        pick = cand & (n == idx)
        sel = sel | pick
        rem = rem & jnp.logical_not(pick)
    return sel


def _augment_queries(qs_t, pens, nb):
    tq = qs_t.shape[1]
    out = []
    for h in range(2):
        if pens is not None:
            pen16 = pens[h]
        else:
            pen16 = jnp.where(lax.broadcasted_iota(jnp.int32, (16, tq), 0) >= nb, NEG, 0.0)
        tail = [pen16, jnp.full((8, tq), NEG, F32), jnp.zeros((HEAD_DIM - 24, tq), F32)]
        parts = [qs_t[0:HEAD_DIM]] + tail if h == 0 else tail + [qs_t[HEAD_DIM:]]
        out.append(jnp.concatenate(parts, axis=0).astype(BF16))
    return out


def _flash(qa, ka_ref, vt_ref, bias_ref, qi, v_base, bufs):
    tq = MOBA_BLOCK
    s_a, s_b, m_ref, l_ref, acc_ref = bufs
    n_past = lax.shift_right_logical(qi, 1)

    def pair_block(j):
        return qi - 2 * j

    def scores_into(buf, j, bias=False):
        r = pl.multiple_of(pair_block(j) * tq, tq)
        for h in range(2):
            s = _dot(ka_ref[h, pl.ds(r, 2 * tq), :], qa[h])
            buf[h] = s + bias_ref[h] if bias else s

    def update(buf, j):
        blk0 = pair_block(j)
        for h in range(2):
            s = buf[h]
            m = m_ref[h]
            m_new = jnp.maximum(m, jnp.max(s, axis=0, keepdims=True))
            alpha = jnp.exp2(m - m_new)
            p = jnp.exp2(s - m_new)
            pb = p.astype(BF16)
            pv = _dot(vt_ref[v_base[h] + blk0], pb[0:tq]) + _dot(vt_ref[v_base[h] + blk0 + 1], pb[tq:])
            m_ref[h] = m_new
            l_ref[h] = alpha * l_ref[h] + jnp.sum(p, axis=0, keepdims=True)
            acc_ref[h] = alpha * acc_ref[h] + pv

    m_ref[...] = jnp.full(m_ref.shape, 3.0 * NEG, F32)
    l_ref[...] = jnp.zeros(l_ref.shape, F32)
    acc_ref[...] = jnp.zeros(acc_ref.shape, F32)
    scores_into(s_a, 0, bias=True)

    def body(t, carry):
        scores_into(s_b, 2 * t + 1)
        update(s_a, 2 * t)
        scores_into(s_a, 2 * t + 2)
        update(s_b, 2 * t + 1)
        return carry

    n_double = lax.shift_right_logical(n_past, 1)
    lax.fori_loop(0, n_double, body, 0)
    odd = (n_past & 1) == 1

    @pl.when(odd)
    def _():
        scores_into(s_b, n_past)
        update(s_a, n_past - 1)
        update(s_b, n_past)

    @pl.when(jnp.logical_not(odd))
    def _():
        update(s_a, n_past)

    return [(acc_ref[h], l_ref[h]) for h in range(2)]


def _moba_kernel(tab_ref, q_ref, k_ref, v_ref, km_ref, o_ref, ka_ref, vt_ref, bias_ref, *bufs, seq):
    hp, b, qi = pl.program_id(0), pl.program_id(1), pl.program_id(2)
    nb = seq // MOBA_BLOCK

    @pl.when((b == 0) & (qi == 0))
    def _():
        _build_bias(tab_ref, bias_ref, 2 * hp)

    @pl.when(qi == 0)
    def _():
        _build_kv(k_ref, v_ref, ka_ref, vt_ref, seq, mask_v=True)

    q_t = q_ref[0].astype(F32).T
    km = km_ref[:, 0, :]
    lane_k = lax.broadcasted_iota(jnp.int32, (nb, LANES), 1)
    km0 = jnp.where(lane_k < HEAD_DIM, km, 0.0)
    km1 = jnp.where(lane_k < HEAD_DIM, 0.0, km)
    pad = jnp.zeros((HEAD_DIM - nb, LANES), F32)
    kmx = jnp.concatenate([km1, pad, km0, pad], axis=0)
    kmx_hi = kmx.astype(BF16)
    kmx_lo = (kmx - kmx_hi.astype(F32)).astype(BF16)
    q_bf = q_t.astype(BF16)
    gate = _dot(kmx_hi, q_bf) + _dot(kmx_lo, q_bf)
    pens = [jnp.where(_moba_attend(gate[base:base + 16], qi), 0.0, NEG) for base in (HEAD_DIM, 0)]

    qa = _augment_queries(q_t * (HEAD_DIM ** -0.5 * LOG2E), pens, nb)
    (a0, l0), (a1, l1) = _flash(qa, ka_ref, vt_ref, bias_ref, qi, (0, nb + 1), bufs)
    o_ref[0] = (a0 / l0 + a1 / l1).T.astype(o_ref.dtype)


def _diff_kernel(tab_ref, lam_ref, q_ref, k_ref, v_ref, sg_ref, o_ref, ka_ref, vt_ref, bias_ref, *bufs, seq,
                 lam_init):
    hd, b, qi = pl.program_id(0), pl.program_id(1), pl.program_id(2)

    @pl.when((b == 0) & (qi == 0))
    def _():
        _build_bias(tab_ref, bias_ref, 2 * hd)

    @pl.when(qi == 0)
    def _():
        _build_kv(k_ref, v_ref, ka_ref, vt_ref, seq, mask_v=False)

    lv = lam_ref[...]
    lam = (jnp.exp(jnp.sum(lv[0:1] * lv[1:2], axis=1, keepdims=True))
           - jnp.exp(jnp.sum(lv[2:3] * lv[3:4], axis=1, keepdims=True)) + lam_init)

    q_t = q_ref[0].astype(F32).T
    qa = _augment_queries(q_t * (HEAD_DIM ** -0.5 * LOG2E), None, seq // MOBA_BLOCK)
    (a0, l0), (a1, l1) = _flash(qa, ka_ref, vt_ref, bias_ref, qi, (0, 0), bufs)
    att = a0 / l0 - lam * (a1 / l1)
    y = att * lax.rsqrt(jnp.mean(att * att, axis=0, keepdims=True) + RMS_EPS)
    o_ref[0] = ((y.T * sg_ref[...]) * (1.0 - lam_init)).astype(o_ref.dtype)


def _attention(kind, q_src, kv_src, rel_bias, extra, *, batch, seq, q_col, k_col, v_col, lam_init=None):
    tq = MOBA_BLOCK
    nb = seq // tq
    n_hp = D_MODEL // LANES
    grid = (n_hp, batch, nb)
    smem = pl.BlockSpec(memory_space=pltpu.SMEM)
    q_spec = pl.BlockSpec((1, tq, LANES), lambda h, b, i: (b, i, q_col + h))
    k_spec = pl.BlockSpec((1, seq, LANES), lambda h, b, i: (b, 0, k_col + h))
    v_spec = pl.BlockSpec((1, seq, LANES), lambda h, b, i: (b, 0, v_col + h))
    o_spec = pl.BlockSpec((1, tq, LANES), lambda h, b, i: (b, i, h))
    scratch_k = pltpu.VMEM((2, seq + tq, LANES), BF16)
    bias_s = pltpu.VMEM((2, 2 * tq, tq), F32)
    flash_bufs = [pltpu.VMEM((2, 2 * tq, tq), F32), pltpu.VMEM((2, 2 * tq, tq), F32),
                  pltpu.VMEM((2, 1, tq), F32), pltpu.VMEM((2, 1, tq), F32), pltpu.VMEM((2, LANES, tq), F32)]
    out_shape = jax.ShapeDtypeStruct((batch, seq, D_MODEL), BF16)
    if kind == "moba":
        kmean = extra
        km_spec = pl.BlockSpec((nb, 1, LANES), lambda h, b, i: (b, 0, h))
        return pl.pallas_call(
            functools.partial(_moba_kernel, seq=seq),
            grid=grid,
            in_specs=[smem, q_spec, k_spec, v_spec, km_spec],
            out_specs=o_spec, out_shape=out_shape,
            scratch_shapes=[scratch_k, pltpu.VMEM((2 * (nb + 1), LANES, tq), BF16), bias_s] + flash_bufs,
            compiler_params=_params(3), name="moba_attention",
        )(rel_bias, q_src, kv_src, kv_src, kmean)
    lam_rows, sub_gain = extra
    return pl.pallas_call(
        functools.partial(_diff_kernel, seq=seq, lam_init=lam_init),
        grid=grid,
        in_specs=[smem, _full((8, LANES)), q_spec, k_spec, v_spec, _full((1, LANES))],
        out_specs=o_spec, out_shape=out_shape,
        scratch_shapes=[scratch_k, pltpu.VMEM((nb + 1, LANES, tq), BF16), bias_s] + flash_bufs,
        compiler_params=_params(3), name="diff_attention",
    )(rel_bias, lam_rows, q_src, kv_src, kv_src, sub_gain)


def _router(xn, w_hi_ref, w_lo_ref, b_ref, tri_ref, count_ref):
    lane = lax.broadcasted_iota(jnp.int32, (xn.shape[0], LANES), 1)
    live = lane < MOE_EXPERTS
    x_hi = xn.astype(BF16)
    x_lo = (xn - x_hi.astype(F32)).astype(BF16)
    logits = (_dot(x_hi, w_hi_ref[...]) + (_dot(x_hi, w_lo_ref[...]) + _dot(x_lo, w_hi_ref[...]))) + b_ref[...]
    gl = jnp.where(live, logits, -jnp.inf)
    el = pltpu.roll(logits, LANES // 2, axis=1)
    gmax = jnp.max(gl, axis=1, keepdims=True)
    gsum = jnp.sum(jnp.exp(gl - gmax), axis=1, keepdims=True) / MOE_EXPERTS_PER_GROUP
    g_gate = 1.0 / gsum
    first = jnp.min(jnp.where(gl == gmax, lane, 1 << 20), axis=1, keepdims=True)
    group_shift = MOE_EXPERTS_PER_GROUP.bit_length() - 1
    in_group = jnp.right_shift(lane, group_shift) == jnp.right_shift(first, group_shift)
    em = jnp.where(in_group & live, el, -jnp.inf)
    m1 = jnp.max(em, axis=1, keepdims=True)
    i1 = jnp.min(jnp.where(em == m1, lane, 1 << 20), axis=1, keepdims=True)
    em2 = jnp.where(lane == i1, -jnp.inf, em)
    m2 = jnp.max(em2, axis=1, keepdims=True)
    i2 = jnp.min(jnp.where(em2 == m2, lane, 1 << 20), axis=1, keepdims=True)
    p2 = jnp.exp(m2 - m1)
    w1 = g_gate * (1.0 / (1.0 + p2))
    w2 = g_gate * (p2 / (1.0 + p2))

    is1, is2 = lane == i1, lane == i2
    member = jnp.where(is1 | is2, 1.0, 0.0)
    before = _dot(tri_ref[...], member.astype(BF16)) + count_ref[...]
    rank1 = jnp.sum(jnp.where(is1, before, 0.0), axis=1, keepdims=True)
    rank2 = jnp.sum(jnp.where(is2, before, 0.0), axis=1, keepdims=True)
    count_ref[...] = count_ref[...] + jnp.sum(member, axis=0, keepdims=True)
    fields = (i1.astype(F32), i2.astype(F32), w1, w2, rank1, rank2)
    route = jnp.zeros(lane.shape, F32)
    for k, col in enumerate(fields):
        route = jnp.where(lane == k, col, route)
    return route


def _attn_out_kernel(a_ref, h_ref, wo_ref, fg_ref, w_hi_ref, w_lo_ref, b_ref, tri_ref,
                     h1_ref, xn_ref, route_ref, counts_ref, count_acc):
    @pl.when(pl.program_id(0) == 0)
    def _():
        count_acc[...] = jnp.zeros_like(count_acc)

    h1 = h_ref[...] + _dot(a_ref[...], wo_ref[...])
    h1_ref[...] = h1
    xn = _rms(h1, fg_ref[...])
    xn_ref[...] = xn
    route_ref[...] = _router(xn, w_hi_ref, w_lo_ref, b_ref, tri_ref, count_acc)
    counts_ref[...] = jnp.broadcast_to(count_acc[...], counts_ref.shape)


def _attn_out(att2, h2, wo_bf16, ffn_gain, w_hi, w_lo, b_x):
    t = h2.shape[0]
    tm = ROW_TILE
    row = lambda w: pl.BlockSpec((tm, w), lambda i: (i, 0))
    tri = jnp.tril(jnp.ones((tm, tm), F32), -1).astype(BF16)
    return pl.pallas_call(
        _attn_out_kernel,
        grid=(t // tm,),
        in_specs=[row(D_MODEL), row(D_MODEL), _full((D_MODEL, D_MODEL)), _full((1, D_MODEL)),
                  _full((D_MODEL, LANES)), _full((D_MODEL, LANES)), _full((1, LANES)), _full((tm, tm))],
        out_specs=[row(D_MODEL), row(D_MODEL), row(LANES), _full((8, LANES))],
        out_shape=[jax.ShapeDtypeStruct((t, D_MODEL), F32), jax.ShapeDtypeStruct((t, D_MODEL), F32),
                   jax.ShapeDtypeStruct((t, LANES), F32), jax.ShapeDtypeStruct((8, LANES), F32)],
        scratch_shapes=[pltpu.VMEM((1, LANES), F32)],
        compiler_params=_params(1), name="attn_out_router",
    )(att2, h2, wo_bf16, ffn_gain.reshape(1, D_MODEL), w_hi, w_lo, b_x, tri)


def _scatter_kernel(last_row_ref, tiles_ref, used_ref, pos1_ref, pos2_ref, xn_ref, out_hbm, zeros_ref, sem):
    tm = pos1_ref.shape[2]

    @pl.when(pl.program_id(0) == 0)
    def _():
        zeros_ref[...] = jnp.zeros_like(zeros_ref)

        def tile_copy(e):
            row = pl.multiple_of(last_row_ref[e], GROUP_TILE)
            return pltpu.make_async_copy(zeros_ref, out_hbm.at[pl.ds(row, GROUP_TILE)], sem)

        for e in range(MOE_EXPERTS):
            @pl.when(tiles_ref[e] > 0)
            def _():
                tile_copy(e).start()

        for e in range(MOE_EXPERTS):
            @pl.when(tiles_ref[e] > 0)
            def _():
                tile_copy(e).wait()

        def spare_tile(w, carry):
            row = pl.multiple_of(w * GROUP_TILE, GROUP_TILE)
            cp = pltpu.make_async_copy(zeros_ref, out_hbm.at[pl.ds(row, GROUP_TILE)], sem)
            cp.start()
            cp.wait()
            return carry

        lax.fori_loop(used_ref[0], out_hbm.shape[0] // GROUP_TILE, spare_tile, 0)

    def issue(r, carry):
        src = xn_ref.at[pl.ds(r, 1)]
        pltpu.make_async_copy(src, out_hbm.at[pl.ds(pos1_ref[0, 0, r], 1)], sem).start()
        pltpu.make_async_copy(src, out_hbm.at[pl.ds(pos2_ref[0, 0, r], 1)], sem).start()
        return carry

    lax.fori_loop(0, tm, issue, 0)
    for _ in range(2):
        pltpu.make_async_copy(xn_ref, out_hbm.at[pl.ds(0, tm)], sem).wait()


def _scatter_rows(xn, pos1, pos2, last_row, tiles_per, n_used, n_rows):
    t = xn.shape[0]
    tm = ROW_TILE
    idx = lambda: pl.BlockSpec((1, 1, tm), lambda i, *_: (i, 0, 0), memory_space=pltpu.SMEM)
    return pl.pallas_call(
        _scatter_kernel,
        grid_spec=pltpu.PrefetchScalarGridSpec(
            num_scalar_prefetch=3, grid=(t // tm,),
            in_specs=[idx(), idx(), pl.BlockSpec((tm, D_MODEL), lambda i, *_: (i, 0))],
            out_specs=pl.BlockSpec(memory_space=pl.ANY),
            scratch_shapes=[pltpu.VMEM((GROUP_TILE, D_MODEL), F32), pltpu.SemaphoreType.DMA(())]),
        out_shape=jax.ShapeDtypeStruct((n_rows, D_MODEL), F32),
        compiler_params=_params(1), name="moe_scatter",
    )(last_row, tiles_per, n_used, pos1.reshape(t // tm, 1, tm), pos2.reshape(t // tm, 1, tm), xn)


def _moe_group_kernel(te_ref, x_ref, wg_ref, wu_ref, wd_ref, y_ref):
    del te_ref
    x = x_ref[...].astype(BF16)
    g = _dot(x, wg_ref[0])
    u = _dot(x, wu_ref[0])
    hh = (g * jax.nn.sigmoid(g)) * u
    y_ref[...] = _dot(hh.astype(BF16), wd_ref[0])


def _moe_group(x_sorted, tile_expert, wg, wu, wd):
    n_tiles = tile_expert.shape[0]
    tmg = GROUP_TILE
    wspec = lambda a, b: pl.BlockSpec((1, a, b), lambda w, te: (te[w], 0, 0))
    rows = lambda: pl.BlockSpec((tmg, D_MODEL), lambda w, te: (w, 0))
    return pl.pallas_call(
        _moe_group_kernel,
        grid_spec=pltpu.PrefetchScalarGridSpec(
            num_scalar_prefetch=1, grid=(n_tiles,),
            in_specs=[rows(), wspec(D_MODEL, MOE_D_FF), wspec(D_MODEL, MOE_D_FF), wspec(MOE_D_FF, D_MODEL)],
            out_specs=rows()),
        out_shape=jax.ShapeDtypeStruct((n_tiles * tmg, D_MODEL), F32),
        compiler_params=_params(1), name="moe_group_ffn",
    )(tile_expert, x_sorted, wg, wu, wd)


def _routing_tables(route, counts8, t):
    n_tiles = 2 * t // GROUP_TILE + MOE_EXPERTS
    counts = counts8[0, :MOE_EXPERTS].astype(jnp.int32)
    tiles_per = (counts + GROUP_TILE - 1) // GROUP_TILE
    tile_end = jnp.cumsum(tiles_per)
    offsets = (tile_end - tiles_per) * GROUP_TILE
    last_row = jnp.maximum(tile_end - 1, 0) * GROUP_TILE
    ids = route[:, 0:2].astype(jnp.int32)
    ranks = route[:, 4:6].astype(jnp.int32)
    expert_ids = jnp.arange(MOE_EXPERTS, dtype=jnp.int32)
    pos = jnp.sum(jnp.where(ids[:, :, None] == expert_ids, offsets, 0), axis=2) + ranks
    tile_ids = jnp.arange(n_tiles, dtype=jnp.int32)
    tile_expert = jnp.minimum(jnp.sum((tile_end[None, :] <= tile_ids[:, None]).astype(jnp.int32), axis=1),
                              MOE_EXPERTS - 1)
    return pos[:, 0], pos[:, 1], tile_expert, tile_end[-1:], last_row, tiles_per, n_tiles


def _gather_copy(pos_ref, y_hbm, ybuf, sem, slot, k, r):
    return pltpu.make_async_copy(y_hbm.at[pl.ds(pos_ref[0, 0, r], 1)], ybuf.at[slot, k, pl.ds(r, 1)], sem.at[slot])


def _wait_tile(y_hbm, ybuf, sem, slot):
    tm = ybuf.shape[2]
    for k in range(2):
        pltpu.make_async_copy(y_hbm.at[pl.ds(0, tm)], ybuf.at[slot, k], sem.at[slot]).wait()


def _moe_ple_update(refs, tail):
    (pos1_ref, pos2_ref, nxt1_ref, nxt2_ref, y_hbm, route_ref, h_ref, p_ref, pg_ref, wgate_ref, wproj_ref,
     ybuf, sem) = refs
    i = pl.program_id(0)
    slot = i & 1
    nslot = 1 - slot
    tm = ybuf.shape[2]

    @pl.when(i == 0)
    def _():
        def issue(r, carry):
            _gather_copy(pos1_ref, y_hbm, ybuf, sem, 0, 0, r).start()
            _gather_copy(pos2_ref, y_hbm, ybuf, sem, 0, 1, r).start()
            return carry

        lax.fori_loop(0, tm, issue, 0)

    _wait_tile(y_hbm, ybuf, sem, slot)
    rows_per = tm // GATHER_CHUNKS

    def prefetch(c):
        for r in range(c * rows_per, (c + 1) * rows_per):
            _gather_copy(nxt1_ref, y_hbm, ybuf, sem, nslot, 0, r).start()
            _gather_copy(nxt2_ref, y_hbm, ybuf, sem, nslot, 1, r).start()

    proj = _dot(p_ref[...].astype(BF16), wproj_ref[...])
    prefetch(0)
    route = route_ref[...]
    h = h_ref[...] + (route[:, 2:3] * ybuf[slot, 0] + route[:, 3:4] * ybuf[slot, 1])
    prefetch(1)
    gate = jax.nn.sigmoid(_dot(_rms(h, pg_ref[...]).astype(BF16), wgate_ref[...]))
    prefetch(2)
    tail(h + gate * proj, prefetch)

    @pl.when(i == pl.num_programs(0) - 1)
    def _():
        _wait_tile(y_hbm, ybuf, sem, nslot)


def _ple_mid_kernel(*refs):
    kvg_ref, wkv_ref, qg_ref, wq_ref, h3_ref, kv_ref, q_ref = refs[11:18]

    def tail(h3, prefetch):
        h3_ref[...] = h3
        prefetch(3)
        kn = _rms(h3, kvg_ref[...]).astype(BF16)
        for c in range(2):
            kv_ref[:, c * D_MODEL:(c + 1) * D_MODEL] = _dot(kn, wkv_ref[:, c * D_MODEL:(c + 1) * D_MODEL]).astype(BF16)
            prefetch(4 + c)
        q_ref[...] = _dot(_rms(h3, qg_ref[...]).astype(BF16), wq_ref[...]).astype(BF16)
        prefetch(6)
        prefetch(7)

    _moe_ple_update(refs[:11] + refs[18:], tail)


def _ple_last_kernel(*refs):
    fg_ref, o_ref = refs[11:13]

    def tail(h3, prefetch):
        o_ref[...] = _rms(h3, fg_ref[...])
        for c in range(3, GATHER_CHUNKS):
            prefetch(c)

    _moe_ple_update(refs[:11] + refs[13:], tail)


def _ple_call(body, name, moe_in, h1, p2, vecs_and_weights, in_tail, out_specs, out_shape):
    pos1, pos2, y_sorted, route = moe_in
    t = h1.shape[0]
    tm = ROW_TILE
    n = t // tm
    row = lambda w: pl.BlockSpec((tm, w), lambda i: (i, 0))
    idx = lambda: pl.BlockSpec((1, 1, tm), lambda i: (i, 0, 0), memory_space=pltpu.SMEM)
    nxt = lambda: pl.BlockSpec((1, 1, tm), lambda i: (jnp.minimum(i + 1, n - 1), 0, 0), memory_space=pltpu.SMEM)
    pos1, pos2 = pos1.reshape(n, 1, tm), pos2.reshape(n, 1, tm)
    return pl.pallas_call(
        body,
        grid=(n,),
        in_specs=[idx(), idx(), nxt(), nxt(), pl.BlockSpec(memory_space=pl.ANY), row(LANES), row(D_MODEL),
                  row(PLE_DIM)] + in_tail,
        out_specs=out_specs, out_shape=out_shape,
        scratch_shapes=[pltpu.VMEM((2, 2, tm, D_MODEL), F32), pltpu.SemaphoreType.DMA((2,))],
        compiler_params=_params(1), name=name,
    )(pos1, pos2, pos1, pos2, y_sorted, route, h1, p2, *vecs_and_weights)


def _ple_mid(moe_in, h1, p2, ple_gain, wgate, wproj, kv_gain, wkv, q_gain, wq):
    t = h1.shape[0]
    tm = ROW_TILE
    row = lambda w: pl.BlockSpec((tm, w), lambda i: (i, 0))
    vec = _full((1, D_MODEL))
    return _ple_call(
        _ple_mid_kernel, "moe_combine_ple_kv_q", moe_in, h1, p2,
        (ple_gain.reshape(1, -1), wgate, wproj, kv_gain.reshape(1, -1), wkv, q_gain.reshape(1, -1), wq),
        [vec, _full((D_MODEL, D_MODEL)), _full((PLE_DIM, D_MODEL)), vec, _full((D_MODEL, 2 * D_MODEL)), vec,
         _full((D_MODEL, D_MODEL))],
        [row(D_MODEL), row(2 * D_MODEL), row(D_MODEL)],
        [jax.ShapeDtypeStruct((t, D_MODEL), F32), jax.ShapeDtypeStruct((t, 2 * D_MODEL), BF16),
         jax.ShapeDtypeStruct((t, D_MODEL), BF16)])


def _ple_last(moe_in, h1, p2, ple_gain, wgate, wproj, final_gain):
    t = h1.shape[0]
    tm = ROW_TILE
    vec = _full((1, D_MODEL))
    return _ple_call(
        _ple_last_kernel, "moe_combine_ple_final_norm", moe_in, h1, p2,
        (ple_gain.reshape(1, -1), wgate, wproj, final_gain.reshape(1, -1)),
        [vec, _full((D_MODEL, D_MODEL)), _full((PLE_DIM, D_MODEL)), vec],
        pl.BlockSpec((tm, D_MODEL), lambda i: (i, 0)),
        jax.ShapeDtypeStruct((t, D_MODEL), F32))


def _router_operands(w_group, b_group, w_router, b_router):
    gap = LANES // 2 - MOE_EXPERTS
    w = jnp.pad(jnp.concatenate([jnp.repeat(w_group, MOE_EXPERTS_PER_GROUP, axis=1),
                                 jnp.zeros((D_MODEL, gap), F32), w_router], axis=1), ((0, 0), (0, gap)))
    b = jnp.pad(jnp.concatenate([jnp.repeat(b_group, MOE_EXPERTS_PER_GROUP), jnp.zeros((gap,), F32), b_router]),
                (0, gap)).reshape(1, LANES)
    w_hi = w.astype(BF16)
    w_lo = (w - w_hi.astype(F32)).astype(BF16)
    return w_hi, w_lo, b


def _moe_layer(att, h, wo, i, ffn_norm, w_group, b_group, w_router, b_router, w_gate, w_up, w_down):
    t = h.shape[0]
    h1, xn, route, counts8 = _attn_out(att, h, wo.astype(BF16), ffn_norm[i],
                                       *_router_operands(w_group[i], b_group[i], w_router[i], b_router[i]))
    pos1, pos2, tile_expert, n_used, last_row, tiles_per, n_tiles = _routing_tables(route, counts8, t)
    x_sorted = _scatter_rows(xn, pos1, pos2, last_row, tiles_per, n_used, n_tiles * GROUP_TILE)
    wg = w_gate[i].reshape(MOE_EXPERTS, D_MODEL, MOE_D_FF).astype(BF16)
    wu = w_up[i].reshape(MOE_EXPERTS, D_MODEL, MOE_D_FF).astype(BF16)
    wd = w_down[i].reshape(MOE_EXPERTS, MOE_D_FF, D_MODEL).astype(BF16)
    y_sorted = _moe_group(x_sorted, tile_expert, wg, wu, wd)
    return (pos1, pos2, y_sorted, route), h1


def kernel(x, p, rel_bias, attn_norm_a, w_qkv_a, w_o_a, kv_norm, w_kv, attn_norm_b, w_q_b, lambda_q1, lambda_k1,
           lambda_q2, lambda_k2, subln_b, w_o_b, ffn_norm, w_group, b_group, w_router, b_router, w_gate, w_up,
           w_down, ple_norm, w_ple_gate, w_ple_proj, final_norm):
    batch, seq, d = x.shape
    assert d == D_MODEL and seq % MOBA_BLOCK == 0 and seq // MOBA_BLOCK <= 16
    t = batch * seq
    assert t % ROW_TILE == 0
    n_hp = D_MODEL // LANES
    moe = (ffn_norm, w_group, b_group, w_router, b_router, w_gate, w_up, w_down)

    h = x.reshape(t, d)
    qkv, kmean = _qkv_proj(h, attn_norm_a[0], w_qkv_a[0].astype(BF16))
    qkv3 = qkv.reshape(batch, seq, 3 * d)
    att = _attention("moba", qkv3, qkv3, rel_bias, kmean, batch=batch, seq=seq,
                     q_col=0, k_col=n_hp, v_col=2 * n_hp)
    moe_out, h = _moe_layer(att.reshape(t, d), h, w_o_a[0], 0, *moe)
    h, kv, q = _ple_mid(moe_out, h, p[0].reshape(t, PLE_DIM), ple_norm[0], w_ple_gate[0].astype(BF16),
                        w_ple_proj[0].astype(BF16), kv_norm, w_kv.astype(BF16), attn_norm_b[0],
                        w_q_b[0].astype(BF16))

    lam_init = 0.8 - 0.6 * math.exp(-0.3 * 1)
    lam_rows = jnp.pad(jnp.stack([lambda_q1[0], lambda_k1[0], lambda_q2[0], lambda_k2[0]]).astype(F32),
                       ((0, 4), (0, LANES - HEAD_DIM)))
    att = _attention("diff", q.reshape(batch, seq, d), kv.reshape(batch, seq, 2 * d), rel_bias,
                     (lam_rows, subln_b[0].reshape(1, LANES)), batch=batch, seq=seq,
                     q_col=0, k_col=0, v_col=n_hp, lam_init=lam_init)
    moe_out, h = _moe_layer(att.reshape(t, d), h, w_o_b[0], 1, *moe)
    out = _ple_last(moe_out, h, p[1].reshape(t, PLE_DIM), ple_norm[1], w_ple_gate[1].astype(BF16),
                    w_ple_proj[1].astype(BF16), final_norm)
    return out.reshape(batch, seq, d)
```

```python
import functools
import math

import jax
import jax.numpy as jnp
from jax import lax
from jax.experimental import pallas as pl
from jax.experimental.pallas import tpu as pltpu

F32 = jnp.float32
BF16 = jnp.bfloat16

D_MODEL = 1024
DEPTH = 2
N_A_LAYERS = DEPTH // 2
HEAD_DIM = 64
LANES = 128
MOBA_BLOCK = 256
ATT_TQ = 2 * MOBA_BLOCK
MOBA_TOP_K = 3
REL_BUCKETS = 32
REL_MAX_DISTANCE = 128
MOE_GROUPS = 4
MOE_EXPERTS_PER_GROUP = 8
MOE_EXPERTS = MOE_GROUPS * MOE_EXPERTS_PER_GROUP
MOE_D_FF = D_MODEL // 4
PLE_DIM = 256
RMS_EPS = 1e-6
NEG = -1e30
LOG2E = math.log2(math.e)

ROW_TILE = 512
GROUP_TILE = 512
GATHER_CHUNKS = 8
VMEM_LIMIT = 52 * 1024 * 1024


def _dot(a, b):
    return jnp.dot(a, b, preferred_element_type=F32)


def _rms(x, gain):
    y = x * lax.rsqrt(jnp.mean(x * x, axis=-1, keepdims=True) + RMS_EPS)
    return y * gain


def _params(n_axes):
    return pltpu.CompilerParams(dimension_semantics=("arbitrary",) * n_axes,
                                vmem_limit_bytes=VMEM_LIMIT)


def _full(shape):
    nd = len(shape)
    return pl.BlockSpec(shape, lambda *_: (0,) * nd)


def _qkv_kernel(x_ref, g_ref, w_ref, qkv_ref, kmean_ref):
    hn = _rms(x_ref[...], g_ref[...]).astype(BF16)
    for c in range(3):
        y = _dot(hn, w_ref[:, c * D_MODEL:(c + 1) * D_MODEL])
        qkv_ref[:, c * D_MODEL:(c + 1) * D_MODEL] = y.astype(BF16)
        if c == 1:
            nb = y.shape[0] // MOBA_BLOCK
            kmean_ref[...] = jnp.mean(y.reshape(nb, MOBA_BLOCK, D_MODEL), axis=1, keepdims=True)


def _qkv_proj(x2, gain, w_bf16):
    t = x2.shape[0]
    tm = ROW_TILE
    return pl.pallas_call(
        _qkv_kernel,
        grid=(t // tm,),
        in_specs=[pl.BlockSpec((tm, D_MODEL), lambda i: (i, 0)),
                  _full((1, D_MODEL)),
                  _full((D_MODEL, 3 * D_MODEL))],
        out_specs=[pl.BlockSpec((tm, 3 * D_MODEL), lambda i: (i, 0)),
                   pl.BlockSpec((tm // MOBA_BLOCK, 1, D_MODEL), lambda i: (i, 0, 0))],
        out_shape=[jax.ShapeDtypeStruct((t, 3 * D_MODEL), BF16),
                   jax.ShapeDtypeStruct((t // MOBA_BLOCK, 1, D_MODEL), F32)],
        compiler_params=_params(1),
        name="qkv_proj",
    )(x2, gain.reshape(1, D_MODEL), w_bf16)


def _rel_bucket(dist):
    n = jnp.maximum(dist, 0)
    max_exact = REL_BUCKETS // 2
    nf = jnp.maximum(n, max_exact).astype(F32)
    large = max_exact + (jnp.log(nf / max_exact) / math.log(REL_MAX_DISTANCE / max_exact)
                         * (REL_BUCKETS - max_exact)).astype(jnp.int32)
    large = jnp.minimum(large, REL_BUCKETS - 1)
    return jnp.where(n < max_exact, n, large)


def _shifted_bias(tab_ref, col, dist):
    bkt = _rel_bucket(dist)
    acc = jnp.zeros(dist.shape, F32)
    for i in range(REL_BUCKETS):
        acc = jnp.where(bkt == i, tab_ref[i, col], acc)
    return (acc - tab_ref[REL_BUCKETS - 1, col]) * LOG2E


def _build_bias(tab_ref, bias_ref, corner_ref, col0):
    rows = 64
    for r0 in range(0, ATT_TQ, rows):
        key = lax.broadcasted_iota(jnp.int32, (rows, ATT_TQ), 0) + r0
        qry = lax.broadcasted_iota(jnp.int32, (rows, ATT_TQ), 1)
        dist = qry - key
        for h in range(2):
            bias_ref[h, r0:r0 + rows, :] = jnp.where(dist >= 0, _shifted_bias(tab_ref, col0 + h, dist), NEG)
    key = lax.broadcasted_iota(jnp.int32, (REL_MAX_DISTANCE, REL_MAX_DISTANCE), 0) - REL_MAX_DISTANCE
    qry = lax.broadcasted_iota(jnp.int32, (REL_MAX_DISTANCE, REL_MAX_DISTANCE), 1)
    for h in range(2):
        corner_ref[h] = _shifted_bias(tab_ref, col0 + h, qry - key)


def _build_kv(k_ref, v_ref, ka_ref, vt_ref, seq, mask_v):
    tb = MOBA_BLOCK
    nb = seq // tb
    lane = lax.broadcasted_iota(jnp.int32, (tb, LANES), 1)
    first = lane < HEAD_DIM

    def body(j, _):
        r = pl.multiple_of(j * tb, tb)
        k = k_ref[0, pl.ds(r, tb), :].astype(F32)
        oh0 = (lane == HEAD_DIM + j).astype(F32)
        oh1 = (lane == j).astype(F32)
        ka_ref[0, pl.ds(r, tb), :] = jnp.where(first, k, oh0).astype(BF16)
        ka_ref[1, pl.ds(r, tb), :] = jnp.where(first, oh1, k).astype(BF16)
        vf = v_ref[0, pl.ds(r, tb), :].astype(F32)
        if mask_v:
            vt_ref[j] = jnp.where(first, vf, 0.0).T.astype(BF16)
            vt_ref[nb + j] = jnp.where(first, 0.0, vf).T.astype(BF16)
        else:
            vt_ref[j] = vf.T.astype(BF16)
        return 0

    lax.fori_loop(0, nb, body, 0)


def _moba_attend(gate, own):
    n = lax.broadcasted_iota(jnp.int32, gate.shape, 0)
    rem = n < own
    sel = n == own
    for _ in range(MOBA_TOP_K):
        gm = jnp.where(rem, gate, -jnp.inf)
        mx = jnp.max(gm, axis=0, keepdims=True)
        cand = rem & (gm == mx)
        idx = jnp.min(jnp.where(cand, n, 1 << 20), axis=0, keepdims=True)
        pick = cand & (n == idx)
        sel = sel | pick
        rem = rem & jnp.logical_not(pick)
    return sel


def _augment_queries(qs_t, pens):
    nq = qs_t.shape[1]
    out = []
    for h in range(2):
        pen16 = pens[h] if pens is not None else jnp.zeros((16, nq), F32)
        tail = [pen16, jnp.zeros((HEAD_DIM - 16, nq), F32)]
        parts = [qs_t[0:HEAD_DIM]] + tail if h == 0 else tail + [qs_t[HEAD_DIM:]]
        out.append(jnp.concatenate(parts, axis=0).astype(BF16))
    return out


def _flash(qa, ka_ref, vt_ref, bias_ref, corner_ref, qi, v_base, bufs):
    tb = MOBA_BLOCK
    near = REL_MAX_DISTANCE
    s_a, s_b, m_ref, l_ref, acc_ref = bufs

    def scores_into(buf, j, own=False):
        r = pl.multiple_of((qi - j) * ATT_TQ, ATT_TQ)
        prev = jnp.where(j == 1, 1.0, 0.0).astype(F32)
        for h in range(2):
            s = _dot(ka_ref[h, pl.ds(r, ATT_TQ), :], qa[h])
            buf[h] = s + bias_ref[h] if own else s
            if not own:
                buf[h, ATT_TQ - near:ATT_TQ, 0:near] = s[ATT_TQ - near:, 0:near] + prev * corner_ref[h]

    def update(buf, j):
        blk0 = (qi - j) * (ATT_TQ // tb)
        for h in range(2):
            s = buf[h]
            m = m_ref[h]
            m_new = jnp.maximum(m, jnp.max(s, axis=0, keepdims=True))
            alpha = jnp.exp2(m - m_new)
            p = jnp.exp2(s - m_new)
            pb = p.astype(BF16)
            pv = _dot(vt_ref[v_base[h] + blk0], pb[0:tb]) + _dot(vt_ref[v_base[h] + blk0 + 1], pb[tb:])
            m_ref[h] = m_new
            l_ref[h] = alpha * l_ref[h] + jnp.sum(p, axis=0, keepdims=True)
            acc_ref[h] = alpha * acc_ref[h] + pv

    m_ref[...] = jnp.full(m_ref.shape, 3.0 * NEG, F32)
    l_ref[...] = jnp.zeros(l_ref.shape, F32)
    acc_ref[...] = jnp.zeros(acc_ref.shape, F32)
    scores_into(s_a, 0, own=True)

    def body(t, carry):
        scores_into(s_b, 2 * t + 1)
        update(s_a, 2 * t)
        scores_into(s_a, 2 * t + 2)
        update(s_b, 2 * t + 1)
        return carry

    lax.fori_loop(0, lax.shift_right_logical(qi, 1), body, 0)
    odd = (qi & 1) == 1

    @pl.when(odd)
    def _():
        scores_into(s_b, qi)
        update(s_a, qi - 1)
        update(s_b, qi)

    @pl.when(jnp.logical_not(odd))
    def _():
        update(s_a, qi)

    return [(acc_ref[h], l_ref[h]) for h in range(2)]


def _moba_kernel(tab_ref, q_ref, k_ref, v_ref, km_ref, o_ref, ka_ref, vt_ref, bias_ref, corner_ref, *bufs, seq):
    hp, b, qi = pl.program_id(0), pl.program_id(1), pl.program_id(2)
    nb = seq // MOBA_BLOCK

    @pl.when((b == 0) & (qi == 0))
    def _():
        _build_bias(tab_ref, bias_ref, corner_ref, 2 * hp)

    @pl.when(qi == 0)
    def _():
        _build_kv(k_ref, v_ref, ka_ref, vt_ref, seq, mask_v=True)

    q_t = q_ref[0].astype(F32).T
    km = km_ref[:, 0, :]
    lane_k = lax.broadcasted_iota(jnp.int32, (nb, LANES), 1)
    km0 = jnp.where(lane_k < HEAD_DIM, km, 0.0)
    km1 = jnp.where(lane_k < HEAD_DIM, 0.0, km)
    pad = jnp.zeros((HEAD_DIM - nb, LANES), F32)
    kmx = jnp.concatenate([km1, pad, km0, pad], axis=0)
    kmx_hi = kmx.astype(BF16)
    kmx_lo = (kmx - kmx_hi.astype(F32)).astype(BF16)
    q_bf = q_t.astype(BF16)
    gate = _dot(kmx_hi, q_bf) + _dot(kmx_lo, q_bf)
    col = lax.broadcasted_iota(jnp.int32, (1, ATT_TQ), 1)
    own = qi * (ATT_TQ // MOBA_BLOCK) + col // MOBA_BLOCK
    pens = [jnp.where(_moba_attend(gate[base:base + 16], own), 0.0, NEG) for base in (HEAD_DIM, 0)]

    qa = _augment_queries(q_t * (HEAD_DIM ** -0.5 * LOG2E), pens)
    (a0, l0), (a1, l1) = _flash(qa, ka_ref, vt_ref, bias_ref, corner_ref, qi, (0, nb), bufs)
    o_ref[0] = (a0 / l0 + a1 / l1).T.astype(o_ref.dtype)


def _diff_kernel(tab_ref, lam_ref, q_ref, k_ref, v_ref, sg_ref, o_ref, ka_ref, vt_ref, bias_ref, corner_ref, *bufs,
                 seq, lam_init):
    hd, b, qi = pl.program_id(0), pl.program_id(1), pl.program_id(2)

    @pl.when((b == 0) & (qi == 0))
    def _():
        _build_bias(tab_ref, bias_ref, corner_ref, 2 * hd)

    @pl.when(qi == 0)
    def _():
        _build_kv(k_ref, v_ref, ka_ref, vt_ref, seq, mask_v=False)

    lv = lam_ref[...]
    lam = (jnp.exp(jnp.sum(lv[0:1] * lv[1:2], axis=1, keepdims=True))
           - jnp.exp(jnp.sum(lv[2:3] * lv[3:4], axis=1, keepdims=True)) + lam_init)

    q_t = q_ref[0].astype(F32).T
    qa = _augment_queries(q_t * (HEAD_DIM ** -0.5 * LOG2E), None)
    (a0, l0), (a1, l1) = _flash(qa, ka_ref, vt_ref, bias_ref, corner_ref, qi, (0, 0), bufs)
    att = a0 / l0 - lam * (a1 / l1)
    y = att * lax.rsqrt(jnp.mean(att * att, axis=0, keepdims=True) + RMS_EPS)
    o_ref[0] = ((y.T * sg_ref[...]) * (1.0 - lam_init)).astype(o_ref.dtype)


def _attention(kind, q_src, kv_src, rel_bias, extra, *, batch, seq, q_col, k_col, v_col, lam_init=None):
    tq = ATT_TQ
    nb = seq // MOBA_BLOCK
    n_hp = D_MODEL // LANES
    grid = (n_hp, batch, seq // tq)
    smem = pl.BlockSpec(memory_space=pltpu.SMEM)
    q_spec = pl.BlockSpec((1, tq, LANES), lambda h, b, i: (b, i, q_col + h))
    k_spec = pl.BlockSpec((1, seq, LANES), lambda h, b, i: (b, 0, k_col + h))
    v_spec = pl.BlockSpec((1, seq, LANES), lambda h, b, i: (b, 0, v_col + h))
    o_spec = pl.BlockSpec((1, tq, LANES), lambda h, b, i: (b, i, h))
    n_vt = 2 * nb if kind == "moba" else nb
    scratch = [pltpu.VMEM((2, seq, LANES), BF16),
               pltpu.VMEM((n_vt, LANES, MOBA_BLOCK), BF16),
               pltpu.VMEM((2, tq, tq), F32),
               pltpu.VMEM((2, REL_MAX_DISTANCE, REL_MAX_DISTANCE), F32),
               pltpu.VMEM((2, tq, tq), F32), pltpu.VMEM((2, tq, tq), F32),
               pltpu.VMEM((2, 1, tq), F32), pltpu.VMEM((2, 1, tq), F32), pltpu.VMEM((2, LANES, tq), F32)]
    out_shape = jax.ShapeDtypeStruct((batch, seq, D_MODEL), BF16)
    if kind == "moba":
        kmean = extra
        km_spec = pl.BlockSpec((nb, 1, LANES), lambda h, b, i: (b, 0, h))
        return pl.pallas_call(
            functools.partial(_moba_kernel, seq=seq),
            grid=grid,
            in_specs=[smem, q_spec, k_spec, v_spec, km_spec],
            out_specs=o_spec, out_shape=out_shape, scratch_shapes=scratch,
            compiler_params=_params(3), name="moba_attention",
        )(rel_bias, q_src, kv_src, kv_src, kmean)
    lam_rows, sub_gain = extra
    return pl.pallas_call(
        functools.partial(_diff_kernel, seq=seq, lam_init=lam_init),
        grid=grid,
        in_specs=[smem, _full((8, LANES)), q_spec, k_spec, v_spec, _full((1, LANES))],
        out_specs=o_spec, out_shape=out_shape, scratch_shapes=scratch,
        compiler_params=_params(3), name="diff_attention",
    )(rel_bias, lam_rows, q_src, kv_src, kv_src, sub_gain)


def _router(xn, w_hi_ref, w_lo_ref, b_ref, tri_ref, count_ref):
    lane = lax.broadcasted_iota(jnp.int32, (xn.shape[0], LANES), 1)
    live = lane < MOE_EXPERTS
    x_hi = xn.astype(BF16)
    x_lo = (xn - x_hi.astype(F32)).astype(BF16)
    logits = (_dot(x_hi, w_hi_ref[...]) + (_dot(x_hi, w_lo_ref[...]) + _dot(x_lo, w_hi_ref[...]))) + b_ref[...]
    gl = jnp.where(live, logits, -jnp.inf)
    el = pltpu.roll(logits, LANES // 2, axis=1)
    gmax = jnp.max(gl, axis=1, keepdims=True)
    gsum = jnp.sum(jnp.exp(gl - gmax), axis=1, keepdims=True) / MOE_EXPERTS_PER_GROUP
    g_gate = 1.0 / gsum
    first = jnp.min(jnp.where(gl == gmax, lane, 1 << 20), axis=1, keepdims=True)
    group_shift = MOE_EXPERTS_PER_GROUP.bit_length() - 1
    in_group = jnp.right_shift(lane, group_shift) == jnp.right_shift(first, group_shift)
    em = jnp.where(in_group & live, el, -jnp.inf)
    m1 = jnp.max(em, axis=1, keepdims=True)
    i1 = jnp.min(jnp.where(em == m1, lane, 1 << 20), axis=1, keepdims=True)
    em2 = jnp.where(lane == i1, -jnp.inf, em)
    m2 = jnp.max(em2, axis=1, keepdims=True)
    i2 = jnp.min(jnp.where(em2 == m2, lane, 1 << 20), axis=1, keepdims=True)
    p2 = jnp.exp(m2 - m1)
    w1 = g_gate * (1.0 / (1.0 + p2))
    w2 = g_gate * (p2 / (1.0 + p2))

    is1, is2 = lane == i1, lane == i2
    member = jnp.where(is1 | is2, 1.0, 0.0)
    before = _dot(tri_ref[...], member.astype(BF16)) + count_ref[...]
    rank1 = jnp.sum(jnp.where(is1, before, 0.0), axis=1, keepdims=True)
    rank2 = jnp.sum(jnp.where(is2, before, 0.0), axis=1, keepdims=True)
    count_ref[...] = count_ref[...] + jnp.sum(member, axis=0, keepdims=True)
    fields = (i1.astype(F32), i2.astype(F32), w1, w2, rank1, rank2)
    route = jnp.zeros(lane.shape, F32)
    for k, col in enumerate(fields):
        route = jnp.where(lane == k, col, route)
    return route


def _attn_out_kernel(a_ref, h_ref, wo_ref, fg_ref, w_hi_ref, w_lo_ref, b_ref, tri_ref,
                     h1_ref, xn_ref, route_ref, counts_ref, count_acc):
    @pl.when(pl.program_id(0) == 0)
    def _():
        count_acc[...] = jnp.zeros_like(count_acc)

    h1 = h_ref[...] + _dot(a_ref[...], wo_ref[...])
    h1_ref[...] = h1
    xn = _rms(h1, fg_ref[...])
    xn_ref[...] = xn
    route_ref[...] = _router(xn, w_hi_ref, w_lo_ref, b_ref, tri_ref, count_acc)
    counts_ref[...] = jnp.broadcast_to(count_acc[...], counts_ref.shape)


def _attn_out(att2, h2, wo_bf16, ffn_gain, w_hi, w_lo, b_x):
    t = h2.shape[0]
    tm = ROW_TILE
    row = lambda w: pl.BlockSpec((tm, w), lambda i: (i, 0))
    tri = jnp.tril(jnp.ones((tm, tm), F32), -1).astype(BF16)
    return pl.pallas_call(
        _attn_out_kernel,
        grid=(t // tm,),
        in_specs=[row(D_MODEL), row(D_MODEL), _full((D_MODEL, D_MODEL)), _full((1, D_MODEL)),
                  _full((D_MODEL, LANES)), _full((D_MODEL, LANES)), _full((1, LANES)), _full((tm, tm))],
        out_specs=[row(D_MODEL), row(D_MODEL), row(LANES), _full((8, LANES))],
        out_shape=[jax.ShapeDtypeStruct((t, D_MODEL), F32), jax.ShapeDtypeStruct((t, D_MODEL), F32),
                   jax.ShapeDtypeStruct((t, LANES), F32), jax.ShapeDtypeStruct((8, LANES), F32)],
        scratch_shapes=[pltpu.VMEM((1, LANES), F32)],
        compiler_params=_params(1), name="attn_out_router",
    )(att2, h2, wo_bf16, ffn_gain.reshape(1, D_MODEL), w_hi, w_lo, b_x, tri)


def _scatter_kernel(last_row_ref, tiles_ref, used_ref, pos1_ref, pos2_ref, xn_ref, out_hbm, zeros_ref, sem):
    tm = pos1_ref.shape[2]

    @pl.when(pl.program_id(0) == 0)
    def _():
        zeros_ref[...] = jnp.zeros_like(zeros_ref)

        def tile_copy(e):
            row = pl.multiple_of(last_row_ref[e], GROUP_TILE)
            return pltpu.make_async_copy(zeros_ref, out_hbm.at[pl.ds(row, GROUP_TILE)], sem)

        for e in range(MOE_EXPERTS):
            @pl.when(tiles_ref[e] > 0)
            def _():
                tile_copy(e).start()

        for e in range(MOE_EXPERTS):
            @pl.when(tiles_ref[e] > 0)
            def _():
                tile_copy(e).wait()

        def spare_tile(w, carry):
            row = pl.multiple_of(w * GROUP_TILE, GROUP_TILE)
            cp = pltpu.make_async_copy(zeros_ref, out_hbm.at[pl.ds(row, GROUP_TILE)], sem)
            cp.start()
            cp.wait()
            return carry

        lax.fori_loop(used_ref[0], out_hbm.shape[0] // GROUP_TILE, spare_tile, 0)

    def issue(r, carry):
        src = xn_ref.at[pl.ds(r, 1)]
        pltpu.make_async_copy(src, out_hbm.at[pl.ds(pos1_ref[0, 0, r], 1)], sem).start()
        pltpu.make_async_copy(src, out_hbm.at[pl.ds(pos2_ref[0, 0, r], 1)], sem).start()
        return carry

    lax.fori_loop(0, tm, issue, 0)
    for _ in range(2):
        pltpu.make_async_copy(xn_ref, out_hbm.at[pl.ds(0, tm)], sem).wait()


def _scatter_rows(xn, pos1, pos2, last_row, tiles_per, n_used, n_rows):
    t = xn.shape[0]
    tm = ROW_TILE
    idx = lambda: pl.BlockSpec((1, 1, tm), lambda i, *_: (i, 0, 0), memory_space=pltpu.SMEM)
    return pl.pallas_call(
        _scatter_kernel,
        grid_spec=pltpu.PrefetchScalarGridSpec(
            num_scalar_prefetch=3, grid=(t // tm,),
            in_specs=[idx(), idx(), pl.BlockSpec((tm, D_MODEL), lambda i, *_: (i, 0))],
            out_specs=pl.BlockSpec(memory_space=pl.ANY),
            scratch_shapes=[pltpu.VMEM((GROUP_TILE, D_MODEL), F32), pltpu.SemaphoreType.DMA(())]),
        out_shape=jax.ShapeDtypeStruct((n_rows, D_MODEL), F32),
        compiler_params=_params(1), name="moe_scatter",
    )(last_row, tiles_per, n_used, pos1.reshape(t // tm, 1, tm), pos2.reshape(t // tm, 1, tm), xn)


def _moe_group_kernel(te_ref, x_ref, wg_ref, wu_ref, wd_ref, y_ref):
    del te_ref
    x = x_ref[...].astype(BF16)
    g = _dot(x, wg_ref[0])
    u = _dot(x, wu_ref[0])
    hh = (g * jax.nn.sigmoid(g)) * u
    y_ref[...] = _dot(hh.astype(BF16), wd_ref[0])


def _moe_group(x_sorted, tile_expert, wg, wu, wd):
    n_tiles = tile_expert.shape[0]
    tmg = GROUP_TILE
    wspec = lambda a, b: pl.BlockSpec((1, a, b), lambda w, te: (te[w], 0, 0))
    rows = lambda: pl.BlockSpec((tmg, D_MODEL), lambda w, te: (w, 0))
    return pl.pallas_call(
        _moe_group_kernel,
        grid_spec=pltpu.PrefetchScalarGridSpec(
            num_scalar_prefetch=1, grid=(n_tiles,),
            in_specs=[rows(), wspec(D_MODEL, MOE_D_FF), wspec(D_MODEL, MOE_D_FF), wspec(MOE_D_FF, D_MODEL)],
            out_specs=rows()),
        out_shape=jax.ShapeDtypeStruct((n_tiles * tmg, D_MODEL), F32),
        compiler_params=_params(1), name="moe_group_ffn",
    )(tile_expert, x_sorted, wg, wu, wd)


def _routing_tables(route, counts8, t):
    n_tiles = 2 * t // GROUP_TILE + MOE_EXPERTS
    counts = counts8[0, :MOE_EXPERTS].astype(jnp.int32)
    tiles_per = (counts + GROUP_TILE - 1) // GROUP_TILE
    tile_end = jnp.cumsum(tiles_per)
    offsets = (tile_end - tiles_per) * GROUP_TILE
    last_row = jnp.maximum(tile_end - 1, 0) * GROUP_TILE
    ids = route[:, 0:2].astype(jnp.int32)
    ranks = route[:, 4:6].astype(jnp.int32)
    expert_ids = jnp.arange(MOE_EXPERTS, dtype=jnp.int32)
    pos = jnp.sum(jnp.where(ids[:, :, None] == expert_ids, offsets, 0), axis=2) + ranks
    tile_ids = jnp.arange(n_tiles, dtype=jnp.int32)
    tile_expert = jnp.minimum(jnp.sum((tile_end[None, :] <= tile_ids[:, None]).astype(jnp.int32), axis=1),
                              MOE_EXPERTS - 1)
    return pos[:, 0], pos[:, 1], tile_expert, tile_end[-1:], last_row, tiles_per, n_tiles


def _gather_copy(pos_ref, y_hbm, ybuf, sem, slot, k, r):
    return pltpu.make_async_copy(y_hbm.at[pl.ds(pos_ref[0, 0, r], 1)], ybuf.at[slot, k, pl.ds(r, 1)], sem.at[slot])


def _wait_tile(y_hbm, ybuf, sem, slot):
    tm = ybuf.shape[2]
    for k in range(2):
        pltpu.make_async_copy(y_hbm.at[pl.ds(0, tm)], ybuf.at[slot, k], sem.at[slot]).wait()


def _moe_ple_update(refs, tail):
    (pos1_ref, pos2_ref, nxt1_ref, nxt2_ref, y_hbm, route_ref, h_ref, p_ref, pg_ref, wgate_ref, wproj_ref,
     ybuf, sem) = refs
    i = pl.program_id(0)
    slot = i & 1
    nslot = 1 - slot
    tm = ybuf.shape[2]

    @pl.when(i == 0)
    def _():
        def issue(r, carry):
            _gather_copy(pos1_ref, y_hbm, ybuf, sem, 0, 0, r).start()
            _gather_copy(pos2_ref, y_hbm, ybuf, sem, 0, 1, r).start()
            return carry

        lax.fori_loop(0, tm, issue, 0)

    _wait_tile(y_hbm, ybuf, sem, slot)
    rows_per = tm // GATHER_CHUNKS

    def prefetch(c):
        for r in range(c * rows_per, (c + 1) * rows_per):
            _gather_copy(nxt1_ref, y_hbm, ybuf, sem, nslot, 0, r).start()
            _gather_copy(nxt2_ref, y_hbm, ybuf, sem, nslot, 1, r).start()

    proj = _dot(p_ref[...].astype(BF16), wproj_ref[...])
    prefetch(0)
    route = route_ref[...]
    h = h_ref[...] + (route[:, 2:3] * ybuf[slot, 0] + route[:, 3:4] * ybuf[slot, 1])
    prefetch(1)
    gate = jax.nn.sigmoid(_dot(_rms(h, pg_ref[...]).astype(BF16), wgate_ref[...]))
    prefetch(2)
    tail(h + gate * proj, prefetch)

    @pl.when(i == pl.num_programs(0) - 1)
    def _():
        _wait_tile(y_hbm, ybuf, sem, nslot)


def _ple_mid_kernel(*refs):
    kvg_ref, wkv_ref, qg_ref, wq_ref, h3_ref, kv_ref, q_ref = refs[11:18]

    def tail(h3, prefetch):
        h3_ref[...] = h3
        prefetch(3)
        kn = _rms(h3, kvg_ref[...]).astype(BF16)
        for c in range(2):
            kv_ref[:, c * D_MODEL:(c + 1) * D_MODEL] = _dot(kn, wkv_ref[:, c * D_MODEL:(c + 1) * D_MODEL]).astype(BF16)
            prefetch(4 + c)
        q_ref[...] = _dot(_rms(h3, qg_ref[...]).astype(BF16), wq_ref[...]).astype(BF16)
        prefetch(6)
        prefetch(7)

    _moe_ple_update(refs[:11] + refs[18:], tail)


def _ple_last_kernel(*refs):
    fg_ref, o_ref = refs[11:13]

    def tail(h3, prefetch):
        o_ref[...] = _rms(h3, fg_ref[...])
        for c in range(3, GATHER_CHUNKS):
            prefetch(c)

    _moe_ple_update(refs[:11] + refs[13:], tail)


def _ple_call(body, name, moe_in, h1, p2, vecs_and_weights, in_tail, out_specs, out_shape):
    pos1, pos2, y_sorted, route = moe_in
    t = h1.shape[0]
    tm = ROW_TILE
    n = t // tm
    row = lambda w: pl.BlockSpec((tm, w), lambda i: (i, 0))
    idx = lambda: pl.BlockSpec((1, 1, tm), lambda i: (i, 0, 0), memory_space=pltpu.SMEM)
    nxt = lambda: pl.BlockSpec((1, 1, tm), lambda i: (jnp.minimum(i + 1, n - 1), 0, 0), memory_space=pltpu.SMEM)
    pos1, pos2 = pos1.reshape(n, 1, tm), pos2.reshape(n, 1, tm)
    return pl.pallas_call(
        body,
        grid=(n,),
        in_specs=[idx(), idx(), nxt(), nxt(), pl.BlockSpec(memory_space=pl.ANY), row(LANES), row(D_MODEL),
                  row(PLE_DIM)] + in_tail,
        out_specs=out_specs, out_shape=out_shape,
        scratch_shapes=[pltpu.VMEM((2, 2, tm, D_MODEL), F32), pltpu.SemaphoreType.DMA((2,))],
        compiler_params=_params(1), name=name,
    )(pos1, pos2, pos1, pos2, y_sorted, route, h1, p2, *vecs_and_weights)


def _ple_mid(moe_in, h1, p2, ple_gain, wgate, wproj, kv_gain, wkv, q_gain, wq):
    t = h1.shape[0]
    tm = ROW_TILE
    row = lambda w: pl.BlockSpec((tm, w), lambda i: (i, 0))
    vec = _full((1, D_MODEL))
    return _ple_call(
        _ple_mid_kernel, "moe_combine_ple_kv_q", moe_in, h1, p2,
        (ple_gain.reshape(1, -1), wgate, wproj, kv_gain.reshape(1, -1), wkv, q_gain.reshape(1, -1), wq),
        [vec, _full((D_MODEL, D_MODEL)), _full((PLE_DIM, D_MODEL)), vec, _full((D_MODEL, 2 * D_MODEL)), vec,
         _full((D_MODEL, D_MODEL))],
        [row(D_MODEL), row(2 * D_MODEL), row(D_MODEL)],
        [jax.ShapeDtypeStruct((t, D_MODEL), F32), jax.ShapeDtypeStruct((t, 2 * D_MODEL), BF16),
         jax.ShapeDtypeStruct((t, D_MODEL), BF16)])


def _ple_last(moe_in, h1, p2, ple_gain, wgate, wproj, final_gain):
    t = h1.shape[0]
    tm = ROW_TILE
    vec = _full((1, D_MODEL))
    return _ple_call(
        _ple_last_kernel, "moe_combine_ple_final_norm", moe_in, h1, p2,
        (ple_gain.reshape(1, -1), wgate, wproj, final_gain.reshape(1, -1)),
        [vec, _full((D_MODEL, D_MODEL)), _full((PLE_DIM, D_MODEL)), vec],
        pl.BlockSpec((tm, D_MODEL), lambda i: (i, 0)),
        jax.ShapeDtypeStruct((t, D_MODEL), F32))


def _router_operands(w_group, b_group, w_router, b_router):
    gap = LANES // 2 - MOE_EXPERTS
    w = jnp.pad(jnp.concatenate([jnp.repeat(w_group, MOE_EXPERTS_PER_GROUP, axis=1),
                                 jnp.zeros((D_MODEL, gap), F32), w_router], axis=1), ((0, 0), (0, gap)))
    b = jnp.pad(jnp.concatenate([jnp.repeat(b_group, MOE_EXPERTS_PER_GROUP), jnp.zeros((gap,), F32), b_router]),
                (0, gap)).reshape(1, LANES)
    w_hi = w.astype(BF16)
    w_lo = (w - w_hi.astype(F32)).astype(BF16)
    return w_hi, w_lo, b


def _moe_layer(att, h, wo, i, ffn_norm, w_group, b_group, w_router, b_router, w_gate, w_up, w_down):
    t = h.shape[0]
    h1, xn, route, counts8 = _attn_out(att, h, wo.astype(BF16), ffn_norm[i],
                                       *_router_operands(w_group[i], b_group[i], w_router[i], b_router[i]))
    pos1, pos2, tile_expert, n_used, last_row, tiles_per, n_tiles = _routing_tables(route, counts8, t)
    x_sorted = _scatter_rows(xn, pos1, pos2, last_row, tiles_per, n_used, n_tiles * GROUP_TILE)
    wg = w_gate[i].reshape(MOE_EXPERTS, D_MODEL, MOE_D_FF).astype(BF16)
    wu = w_up[i].reshape(MOE_EXPERTS, D_MODEL, MOE_D_FF).astype(BF16)
    wd = w_down[i].reshape(MOE_EXPERTS, MOE_D_FF, D_MODEL).astype(BF16)
    y_sorted = _moe_group(x_sorted, tile_expert, wg, wu, wd)
    return (pos1, pos2, y_sorted, route), h1


def kernel(x, p, rel_bias, attn_norm_a, w_qkv_a, w_o_a, kv_norm, w_kv, attn_norm_b, w_q_b, lambda_q1, lambda_k1,
           lambda_q2, lambda_k2, subln_b, w_o_b, ffn_norm, w_group, b_group, w_router, b_router, w_gate, w_up,
           w_down, ple_norm, w_ple_gate, w_ple_proj, final_norm):
    batch, seq, d = x.shape
    assert d == D_MODEL and seq % ATT_TQ == 0 and seq // MOBA_BLOCK <= 16
    t = batch * seq
    assert t % ROW_TILE == 0
    n_hp = D_MODEL // LANES
    moe = (ffn_norm, w_group, b_group, w_router, b_router, w_gate, w_up, w_down)

    h = x.reshape(t, d)
    qkv, kmean = _qkv_proj(h, attn_norm_a[0], w_qkv_a[0].astype(BF16))
    qkv3 = qkv.reshape(batch, seq, 3 * d)
    att = _attention("moba", qkv3, qkv3, rel_bias, kmean, batch=batch, seq=seq,
                     q_col=0, k_col=n_hp, v_col=2 * n_hp)
    moe_out, h = _moe_layer(att.reshape(t, d), h, w_o_a[0], 0, *moe)
    h, kv, q = _ple_mid(moe_out, h, p[0].reshape(t, PLE_DIM), ple_norm[0], w_ple_gate[0].astype(BF16),
                        w_ple_proj[0].astype(BF16), kv_norm, w_kv.astype(BF16), attn_norm_b[0],
                        w_q_b[0].astype(BF16))

    lam_init = 0.8 - 0.6 * math.exp(-0.3 * 1)
    lam_rows = jnp.pad(jnp.stack([lambda_q1[0], lambda_k1[0], lambda_q2[0], lambda_k2[0]]).astype(F32),
                       ((0, 4), (0, LANES - HEAD_DIM)))
    att = _attention("diff", q.reshape(batch, seq, d), kv.reshape(batch, seq, 2 * d), rel_bias,
                     (lam_rows, subln_b[0].reshape(1, LANES)), batch=batch, seq=seq,
                     q_col=0, k_col=0, v_col=n_hp, lam_init=lam_init)
    moe_out, h = _moe_layer(att.reshape(t, d), h, w_o_b[0], 1, *moe)
    out = _ple_last(moe_out, h, p[1].reshape(t, PLE_DIM), ple_norm[1], w_ple_gate[1].astype(BF16),
                    w_ple_proj[1].astype(BF16), final_norm)
    return out.reshape(batch, seq, d)
```

```python
import functools
import math

import jax
import jax.numpy as jnp
from jax import lax
from jax.experimental import pallas as pl
from jax.experimental.pallas import tpu as pltpu

F32 = jnp.float32
BF16 = jnp.bfloat16

D_MODEL = 1024
DEPTH = 2
N_A_LAYERS = DEPTH // 2
HEAD_DIM = 64
LANES = 128
MOBA_BLOCK = 256
ATT_TQ = 2 * MOBA_BLOCK
MOBA_TOP_K = 3
REL_BUCKETS = 32
REL_MAX_DISTANCE = 128
MOE_GROUPS = 4
MOE_EXPERTS_PER_GROUP = 8
MOE_EXPERTS = MOE_GROUPS * MOE_EXPERTS_PER_GROUP
MOE_D_FF = D_MODEL // 4
PLE_DIM = 256
RMS_EPS = 1e-6
NEG = -1e30
LOG2E = math.log2(math.e)

ROW_TILE = 512
GROUP_TILE = 512
GATHER_CHUNKS = 8
VMEM_LIMIT = 52 * 1024 * 1024


def _dot(a, b):
    return jnp.dot(a, b, preferred_element_type=F32)


def _rms(x, gain):
    y = x * lax.rsqrt(jnp.mean(x * x, axis=-1, keepdims=True) + RMS_EPS)
    return y * gain


def _params(n_axes):
    return pltpu.CompilerParams(dimension_semantics=("arbitrary",) * n_axes,
                                vmem_limit_bytes=VMEM_LIMIT)


def _full(shape):
    nd = len(shape)
    return pl.BlockSpec(shape, lambda *_: (0,) * nd)


def _qkv_kernel(x_ref, g_ref, w_ref, qkv_ref, kmean_ref):
    hn = _rms(x_ref[...], g_ref[...]).astype(BF16)
    for c in range(3):
        y = _dot(hn, w_ref[:, c * D_MODEL:(c + 1) * D_MODEL])
        qkv_ref[:, c * D_MODEL:(c + 1) * D_MODEL] = y.astype(BF16)
        if c == 1:
            nb = y.shape[0] // MOBA_BLOCK
            kmean_ref[...] = jnp.mean(y.reshape(nb, MOBA_BLOCK, D_MODEL), axis=1, keepdims=True)


def _qkv_proj(x2, gain, w_bf16):
    t = x2.shape[0]
    tm = ROW_TILE
    return pl.pallas_call(
        _qkv_kernel,
        grid=(t // tm,),
        in_specs=[pl.BlockSpec((tm, D_MODEL), lambda i: (i, 0)),
                  _full((1, D_MODEL)),
                  _full((D_MODEL, 3 * D_MODEL))],
        out_specs=[pl.BlockSpec((tm, 3 * D_MODEL), lambda i: (i, 0)),
                   pl.BlockSpec((tm // MOBA_BLOCK, 1, D_MODEL), lambda i: (i, 0, 0))],
        out_shape=[jax.ShapeDtypeStruct((t, 3 * D_MODEL), BF16),
                   jax.ShapeDtypeStruct((t // MOBA_BLOCK, 1, D_MODEL), F32)],
        compiler_params=_params(1),
        name="qkv_proj",
    )(x2, gain.reshape(1, D_MODEL), w_bf16)


def _rel_bucket(dist):
    n = jnp.maximum(dist, 0)
    max_exact = REL_BUCKETS // 2
    nf = jnp.maximum(n, max_exact).astype(F32)
    large = max_exact + (jnp.log(nf / max_exact) / math.log(REL_MAX_DISTANCE / max_exact)
                         * (REL_BUCKETS - max_exact)).astype(jnp.int32)
    large = jnp.minimum(large, REL_BUCKETS - 1)
    return jnp.where(n < max_exact, n, large)


def _shifted_bias(tab_ref, col, dist):
    bkt = _rel_bucket(dist)
    acc = jnp.zeros(dist.shape, F32)
    for i in range(REL_BUCKETS):
        acc = jnp.where(bkt == i, tab_ref[i, col], acc)
    return (acc - tab_ref[REL_BUCKETS - 1, col]) * LOG2E


def _build_bias(tab_ref, bias_ref, corner_ref, col0):
    rows = 64
    for r0 in range(0, ATT_TQ, rows):
        key = lax.broadcasted_iota(jnp.int32, (rows, ATT_TQ), 0) + r0
        qry = lax.broadcasted_iota(jnp.int32, (rows, ATT_TQ), 1)
        dist = qry - key
        for h in range(2):
            bias_ref[h, r0:r0 + rows, :] = jnp.where(dist >= 0, _shifted_bias(tab_ref, col0 + h, dist), NEG)
    key = lax.broadcasted_iota(jnp.int32, (REL_MAX_DISTANCE, REL_MAX_DISTANCE), 0) - REL_MAX_DISTANCE
    qry = lax.broadcasted_iota(jnp.int32, (REL_MAX_DISTANCE, REL_MAX_DISTANCE), 1)
    for h in range(2):
        corner_ref[h] = _shifted_bias(tab_ref, col0 + h, qry - key)


def _build_kv(k_ref, v_ref, ka_ref, vt_ref, seq, mask_v):
    tb = MOBA_BLOCK
    nb = seq // tb
    lane = lax.broadcasted_iota(jnp.int32, (tb, LANES), 1)
    first = lane < HEAD_DIM

    def body(j, _):
        r = pl.multiple_of(j * tb, tb)
        k = k_ref[0, pl.ds(r, tb), :].astype(F32)
        oh0 = (lane == HEAD_DIM + j).astype(F32)
        oh1 = (lane == j).astype(F32)
        ka_ref[0, pl.ds(r, tb), :] = jnp.where(first, k, oh0).astype(BF16)
        ka_ref[1, pl.ds(r, tb), :] = jnp.where(first, oh1, k).astype(BF16)
        vf = v_ref[0, pl.ds(r, tb), :].astype(F32)
        if mask_v:
            vt_ref[j] = jnp.where(first, vf, 0.0).T.astype(BF16)
            vt_ref[nb + j] = jnp.where(first, 0.0, vf).T.astype(BF16)
        else:
            vt_ref[j] = vf.T.astype(BF16)
        return 0

    lax.fori_loop(0, nb, body, 0)


def _moba_attend(gate, own):
    n = lax.broadcasted_iota(jnp.int32, gate.shape, 0)
    rem = n < own
    sel = n == own
    for _ in range(MOBA_TOP_K):
        gm = jnp.where(rem, gate, -jnp.inf)
        mx = jnp.max(gm, axis=0, keepdims=True)
        cand = rem & (gm == mx)
        idx = jnp.min(jnp.where(cand, n, 1 << 20), axis=0, keepdims=True)
        pick = cand & (n == idx)
        sel = sel | pick
        rem = rem & jnp.logical_not(pick)
    return sel


def _augment_queries(qs_t, pens):
    nq = qs_t.shape[1]
    out = []
    for h in range(2):
        pen16 = pens[h] if pens is not None else jnp.zeros((16, nq), F32)
        tail = [pen16, jnp.zeros((HEAD_DIM - 16, nq), F32)]
        parts = [qs_t[0:HEAD_DIM]] + tail if h == 0 else tail + [qs_t[HEAD_DIM:]]
        out.append(jnp.concatenate(parts, axis=0).astype(BF16))
    return out


def _flash(qa, ka_ref, vt_ref, bias_ref, corner_ref, qi, v_base, bufs):
    tb = MOBA_BLOCK
    near = REL_MAX_DISTANCE
    s_a, s_b, m_ref, l_ref, acc_ref = bufs

    def scores_into(buf, j, own=False):
        r = pl.multiple_of((qi - j) * ATT_TQ, ATT_TQ)
        prev = jnp.where(j == 1, 1.0, 0.0).astype(F32)
        for h in range(2):
            s = _dot(ka_ref[h, pl.ds(r, ATT_TQ), :], qa[h])
            buf[h] = s + bias_ref[h] if own else s
            if not own:
                buf[h, ATT_TQ - near:ATT_TQ, 0:near] = s[ATT_TQ - near:, 0:near] + prev * corner_ref[h]

    def update(buf, j):
        blk0 = (qi - j) * (ATT_TQ // tb)
        for h in range(2):
            s = buf[h]
            m = m_ref[h]
            m_new = jnp.maximum(m, jnp.max(s, axis=0, keepdims=True))
            alpha = jnp.exp2(m - m_new)
            p = jnp.exp2(s - m_new)
            pb = p.astype(BF16)
            pv = _dot(vt_ref[v_base[h] + blk0], pb[0:tb]) + _dot(vt_ref[v_base[h] + blk0 + 1], pb[tb:])
            m_ref[h] = m_new
            l_ref[h] = alpha * l_ref[h] + jnp.sum(p, axis=0, keepdims=True)
            acc_ref[h] = alpha * acc_ref[h] + pv

    m_ref[...] = jnp.full(m_ref.shape, 3.0 * NEG, F32)
    l_ref[...] = jnp.zeros(l_ref.shape, F32)
    acc_ref[...] = jnp.zeros(acc_ref.shape, F32)
    scores_into(s_a, 0, own=True)

    def body(t, carry):
        scores_into(s_b, 2 * t + 1)
        update(s_a, 2 * t)
        scores_into(s_a, 2 * t + 2)
        update(s_b, 2 * t + 1)
        return carry

    lax.fori_loop(0, lax.shift_right_logical(qi, 1), body, 0)
    odd = (qi & 1) == 1

    @pl.when(odd)
    def _():
        scores_into(s_b, qi)
        update(s_a, qi - 1)
        update(s_b, qi)

    @pl.when(jnp.logical_not(odd))
    def _():
        update(s_a, qi)

    return [(acc_ref[h], l_ref[h]) for h in range(2)]


def _moba_kernel(tab_ref, q_ref, k_ref, v_ref, km_ref, o_ref, ka_ref, vt_ref, bias_ref, corner_ref, *bufs, seq):
    hp, b, qi = pl.program_id(0), pl.program_id(1), pl.program_id(2)
    nb = seq // MOBA_BLOCK

    @pl.when((b == 0) & (qi == 0))
    def _():
        _build_bias(tab_ref, bias_ref, corner_ref, 2 * hp)

    @pl.when(qi == 0)
    def _():
        _build_kv(k_ref, v_ref, ka_ref, vt_ref, seq, mask_v=True)

    q_t = q_ref[0].astype(F32).T
    km = km_ref[:, 0, :]
    lane_k = lax.broadcasted_iota(jnp.int32, (nb, LANES), 1)
    km0 = jnp.where(lane_k < HEAD_DIM, km, 0.0)
    km1 = jnp.where(lane_k < HEAD_DIM, 0.0, km)
    pad = jnp.zeros((HEAD_DIM - nb, LANES), F32)
    kmx = jnp.concatenate([km1, pad, km0, pad], axis=0)
    kmx_hi = kmx.astype(BF16)
    kmx_lo = (kmx - kmx_hi.astype(F32)).astype(BF16)
    q_bf = q_t.astype(BF16)
    gate = _dot(kmx_hi, q_bf) + _dot(kmx_lo, q_bf)
    col = lax.broadcasted_iota(jnp.int32, (1, ATT_TQ), 1)
    own = qi * (ATT_TQ // MOBA_BLOCK) + col // MOBA_BLOCK
    pens = [jnp.where(_moba_attend(gate[base:base + 16], own), 0.0, NEG) for base in (HEAD_DIM, 0)]

    qa = _augment_queries(q_t * (HEAD_DIM ** -0.5 * LOG2E), pens)
    (a0, l0), (a1, l1) = _flash(qa, ka_ref, vt_ref, bias_ref, corner_ref, qi, (0, nb), bufs)
    o_ref[0] = (a0 / l0 + a1 / l1).T.astype(o_ref.dtype)


def _diff_kernel(tab_ref, lam_ref, q_ref, k_ref, v_ref, sg_ref, o_ref, ka_ref, vt_ref, bias_ref, corner_ref, *bufs,
                 seq, lam_init):
    hd, b, qi = pl.program_id(0), pl.program_id(1), pl.program_id(2)

    @pl.when((b == 0) & (qi == 0))
    def _():
        _build_bias(tab_ref, bias_ref, corner_ref, 2 * hd)

    @pl.when(qi == 0)
    def _():
        _build_kv(k_ref, v_ref, ka_ref, vt_ref, seq, mask_v=False)

    lv = lam_ref[...]
    lam = (jnp.exp(jnp.sum(lv[0:1] * lv[1:2], axis=1, keepdims=True))
           - jnp.exp(jnp.sum(lv[2:3] * lv[3:4], axis=1, keepdims=True)) + lam_init)

    q_t = q_ref[0].astype(F32).T
    qa = _augment_queries(q_t * (HEAD_DIM ** -0.5 * LOG2E), None)
    (a0, l0), (a1, l1) = _flash(qa, ka_ref, vt_ref, bias_ref, corner_ref, qi, (0, 0), bufs)
    att = a0 / l0 - lam * (a1 / l1)
    y = att * lax.rsqrt(jnp.mean(att * att, axis=0, keepdims=True) + RMS_EPS)
    o_ref[0] = ((y.T * sg_ref[...]) * (1.0 - lam_init)).astype(o_ref.dtype)


def _attention(kind, q_src, kv_src, rel_bias, extra, *, batch, seq, q_col, k_col, v_col, lam_init=None):
    tq = ATT_TQ
    nb = seq // MOBA_BLOCK
    n_hp = D_MODEL // LANES
    grid = (n_hp, batch, seq // tq)
    smem = pl.BlockSpec(memory_space=pltpu.SMEM)
    q_spec = pl.BlockSpec((1, tq, LANES), lambda h, b, i: (b, i, q_col + h))
    k_spec = pl.BlockSpec((1, seq, LANES), lambda h, b, i: (b, 0, k_col + h))
    v_spec = pl.BlockSpec((1, seq, LANES), lambda h, b, i: (b, 0, v_col + h))
    o_spec = pl.BlockSpec((1, tq, LANES), lambda h, b, i: (b, i, h))
    n_vt = 2 * nb if kind == "moba" else nb
    scratch = [pltpu.VMEM((2, seq, LANES), BF16),
               pltpu.VMEM((n_vt, LANES, MOBA_BLOCK), BF16),
               pltpu.VMEM((2, tq, tq), F32),
               pltpu.VMEM((2, REL_MAX_DISTANCE, REL_MAX_DISTANCE), F32),
               pltpu.VMEM((2, tq, tq), F32), pltpu.VMEM((2, tq, tq), F32),
               pltpu.VMEM((2, 1, tq), F32), pltpu.VMEM((2, 1, tq), F32), pltpu.VMEM((2, LANES, tq), F32)]
    out_shape = jax.ShapeDtypeStruct((batch, seq, D_MODEL), BF16)
    if kind == "moba":
        kmean = extra
        km_spec = pl.BlockSpec((nb, 1, LANES), lambda h, b, i: (b, 0, h))
        return pl.pallas_call(
            functools.partial(_moba_kernel, seq=seq),
            grid=grid,
            in_specs=[smem, q_spec, k_spec, v_spec, km_spec],
            out_specs=o_spec, out_shape=out_shape, scratch_shapes=scratch,
            compiler_params=_params(3), name="moba_attention",
        )(rel_bias, q_src, kv_src, kv_src, kmean)
    lam_rows, sub_gain = extra
    return pl.pallas_call(
        functools.partial(_diff_kernel, seq=seq, lam_init=lam_init),
        grid=grid,
        in_specs=[smem, _full((8, LANES)), q_spec, k_spec, v_spec, _full((1, LANES))],
        out_specs=o_spec, out_shape=out_shape, scratch_shapes=scratch,
        compiler_params=_params(3), name="diff_attention",
    )(rel_bias, lam_rows, q_src, kv_src, kv_src, sub_gain)


def _router(xn, w_hi_ref, w_lo_ref, b_ref, tri_ref, count_ref):
    lane = lax.broadcasted_iota(jnp.int32, (xn.shape[0], LANES), 1)
    live = lane < MOE_EXPERTS
    x_hi = xn.astype(BF16)
    x_lo = (xn - x_hi.astype(F32)).astype(BF16)
    logits = (_dot(x_hi, w_hi_ref[...]) + (_dot(x_hi, w_lo_ref[...]) + _dot(x_lo, w_hi_ref[...]))) + b_ref[...]
    gl = jnp.where(live, logits, -jnp.inf)
    el = pltpu.roll(logits, LANES // 2, axis=1)
    gmax = jnp.max(gl, axis=1, keepdims=True)
    gsum = jnp.sum(jnp.exp(gl - gmax), axis=1, keepdims=True) / MOE_EXPERTS_PER_GROUP
    g_gate = 1.0 / gsum
    first = jnp.min(jnp.where(gl == gmax, lane, 1 << 20), axis=1, keepdims=True)
    group_shift = MOE_EXPERTS_PER_GROUP.bit_length() - 1
    in_group = jnp.right_shift(lane, group_shift) == jnp.right_shift(first, group_shift)
    em = jnp.where(in_group & live, el, -jnp.inf)
    m1 = jnp.max(em, axis=1, keepdims=True)
    i1 = jnp.min(jnp.where(em == m1, lane, 1 << 20), axis=1, keepdims=True)
    em2 = jnp.where(lane == i1, -jnp.inf, em)
    m2 = jnp.max(em2, axis=1, keepdims=True)
    i2 = jnp.min(jnp.where(em2 == m2, lane, 1 << 20), axis=1, keepdims=True)
    p2 = jnp.exp(m2 - m1)
    w1 = g_gate * (1.0 / (1.0 + p2))
    w2 = g_gate * (p2 / (1.0 + p2))

    is1, is2 = lane == i1, lane == i2
    member = jnp.where(is1 | is2, 1.0, 0.0)
    before = _dot(tri_ref[...], member.astype(BF16)) + count_ref[...]
    rank1 = jnp.sum(jnp.where(is1, before, 0.0), axis=1, keepdims=True)
    rank2 = jnp.sum(jnp.where(is2, before, 0.0), axis=1, keepdims=True)
    count_ref[...] = count_ref[...] + jnp.sum(member, axis=0, keepdims=True)
    fields = (i1.astype(F32), i2.astype(F32), w1, w2, rank1, rank2)
    route = jnp.zeros(lane.shape, F32)
    for k, col in enumerate(fields):
        route = jnp.where(lane == k, col, route)
    return route


def _attn_out_kernel(a_ref, h_ref, wo_ref, fg_ref, w_hi_ref, w_lo_ref, b_ref, tri_ref,
                     h1_ref, xn_ref, route_ref, counts_ref, count_acc):
    @pl.when(pl.program_id(0) == 0)
    def _():
        count_acc[...] = jnp.zeros_like(count_acc)

    h1 = h_ref[...] + _dot(a_ref[...], wo_ref[...])
    h1_ref[...] = h1
    xn = _rms(h1, fg_ref[...])
    xn_ref[...] = xn
    route_ref[...] = _router(xn, w_hi_ref, w_lo_ref, b_ref, tri_ref, count_acc)
    counts_ref[...] = jnp.broadcast_to(count_acc[...], counts_ref.shape)


def _attn_out(att2, h2, wo_bf16, ffn_gain, w_hi, w_lo, b_x):
    t = h2.shape[0]
    tm = ROW_TILE
    row = lambda w: pl.BlockSpec((tm, w), lambda i: (i, 0))
    tri = jnp.tril(jnp.ones((tm, tm), F32), -1).astype(BF16)
    return pl.pallas_call(
        _attn_out_kernel,
        grid=(t // tm,),
        in_specs=[row(D_MODEL), row(D_MODEL), _full((D_MODEL, D_MODEL)), _full((1, D_MODEL)),
                  _full((D_MODEL, LANES)), _full((D_MODEL, LANES)), _full((1, LANES)), _full((tm, tm))],
        out_specs=[row(D_MODEL), row(D_MODEL), row(LANES), _full((8, LANES))],
        out_shape=[jax.ShapeDtypeStruct((t, D_MODEL), F32), jax.ShapeDtypeStruct((t, D_MODEL), F32),
                   jax.ShapeDtypeStruct((t, LANES), F32), jax.ShapeDtypeStruct((8, LANES), F32)],
        scratch_shapes=[pltpu.VMEM((1, LANES), F32)],
        compiler_params=_params(1), name="attn_out_router",
    )(att2, h2, wo_bf16, ffn_gain.reshape(1, D_MODEL), w_hi, w_lo, b_x, tri)


def _scatter_kernel(last_row_ref, tiles_ref, used_ref, pos1_ref, pos2_ref, xn_ref, out_hbm, zeros_ref, sems):
    tm = pos1_ref.shape[2]
    row_sem, zero_sem = sems.at[0], sems.at[1]
    n_tiles = out_hbm.shape[0] // GROUP_TILE

    def zero_tile(row):
        return pltpu.make_async_copy(zeros_ref, out_hbm.at[pl.ds(pl.multiple_of(row, GROUP_TILE), GROUP_TILE)], zero_sem)

    def spare_tiles(fn):
        def body(w, carry):
            fn(zero_tile(w * GROUP_TILE))
            return carry

        lax.fori_loop(used_ref[0], n_tiles, body, 0)

    @pl.when(pl.program_id(0) == 0)
    def _():
        zeros_ref[...] = jnp.zeros_like(zeros_ref)
        for e in range(MOE_EXPERTS):
            @pl.when(tiles_ref[e] > 0)
            def _():
                zero_tile(last_row_ref[e]).start()

        for e in range(MOE_EXPERTS):
            @pl.when(tiles_ref[e] > 0)
            def _():
                zero_tile(last_row_ref[e]).wait()

        spare_tiles(lambda cp: cp.start())

    def issue(r, carry):
        src = xn_ref.at[pl.ds(r, 1)]
        pltpu.make_async_copy(src, out_hbm.at[pl.ds(pos1_ref[0, 0, r], 1)], row_sem).start(priority=0)
        pltpu.make_async_copy(src, out_hbm.at[pl.ds(pos2_ref[0, 0, r], 1)], row_sem).start(priority=1)
        return carry

    lax.fori_loop(0, tm, issue, 0, unroll=8)
    for _ in range(2):
        pltpu.make_async_copy(xn_ref, out_hbm.at[pl.ds(0, tm)], row_sem).wait()

    @pl.when(pl.program_id(0) == pl.num_programs(0) - 1)
    def _():
        spare_tiles(lambda cp: cp.wait())


def _scatter_rows(xn, pos1, pos2, last_row, tiles_per, n_used, n_rows):
    t = xn.shape[0]
    tm = ROW_TILE
    idx = lambda: pl.BlockSpec((1, 1, tm), lambda i, *_: (i, 0, 0), memory_space=pltpu.SMEM)
    return pl.pallas_call(
        _scatter_kernel,
        grid_spec=pltpu.PrefetchScalarGridSpec(
            num_scalar_prefetch=3, grid=(t // tm,),
            in_specs=[idx(), idx(), pl.BlockSpec((tm, D_MODEL), lambda i, *_: (i, 0))],
            out_specs=pl.BlockSpec(memory_space=pl.ANY),
            scratch_shapes=[pltpu.VMEM((GROUP_TILE, D_MODEL), F32), pltpu.SemaphoreType.DMA((2,))]),
        out_shape=jax.ShapeDtypeStruct((n_rows, D_MODEL), F32),
        compiler_params=_params(1), name="moe_scatter",
    )(last_row, tiles_per, n_used, pos1.reshape(t // tm, 1, tm), pos2.reshape(t // tm, 1, tm), xn)


def _moe_group_kernel(te_ref, x_ref, wg_ref, wu_ref, wd_ref, y_ref):
    del te_ref
    x = x_ref[...].astype(BF16)
    g = _dot(x, wg_ref[0])
    u = _dot(x, wu_ref[0])
    hh = (g * jax.nn.sigmoid(g)) * u
    y_ref[...] = _dot(hh.astype(BF16), wd_ref[0])


def _moe_group(x_sorted, tile_expert, wg, wu, wd):
    n_tiles = tile_expert.shape[0]
    tmg = GROUP_TILE
    wspec = lambda a, b: pl.BlockSpec((1, a, b), lambda w, te: (te[w], 0, 0))
    rows = lambda: pl.BlockSpec((tmg, D_MODEL), lambda w, te: (w, 0))
    return pl.pallas_call(
        _moe_group_kernel,
        grid_spec=pltpu.PrefetchScalarGridSpec(
            num_scalar_prefetch=1, grid=(n_tiles,),
            in_specs=[rows(), wspec(D_MODEL, MOE_D_FF), wspec(D_MODEL, MOE_D_FF), wspec(MOE_D_FF, D_MODEL)],
            out_specs=rows()),
        out_shape=jax.ShapeDtypeStruct((n_tiles * tmg, D_MODEL), F32),
        compiler_params=_params(1), name="moe_group_ffn",
    )(tile_expert, x_sorted, wg, wu, wd)


def _routing_tables(route, counts8, t):
    n_tiles = 2 * t // GROUP_TILE + MOE_EXPERTS
    counts = counts8[0, :MOE_EXPERTS].astype(jnp.int32)
    tiles_per = (counts + GROUP_TILE - 1) // GROUP_TILE
    tile_end = jnp.cumsum(tiles_per)
    offsets = (tile_end - tiles_per) * GROUP_TILE
    last_row = jnp.maximum(tile_end - 1, 0) * GROUP_TILE
    ids = route[:, 0:2].astype(jnp.int32)
    ranks = route[:, 4:6].astype(jnp.int32)
    expert_ids = jnp.arange(MOE_EXPERTS, dtype=jnp.int32)
    pos = jnp.sum(jnp.where(ids[:, :, None] == expert_ids, offsets, 0), axis=2) + ranks
    tile_ids = jnp.arange(n_tiles, dtype=jnp.int32)
    tile_expert = jnp.minimum(jnp.sum((tile_end[None, :] <= tile_ids[:, None]).astype(jnp.int32), axis=1),
                              MOE_EXPERTS - 1)
    return pos[:, 0], pos[:, 1], tile_expert, tile_end[-1:], last_row, tiles_per, n_tiles


def _gather_copy(pos_ref, y_hbm, ybuf, sem, slot, k, r):
    return pltpu.make_async_copy(y_hbm.at[pl.ds(pos_ref[0, 0, r], 1)], ybuf.at[slot, k, pl.ds(r, 1)], sem.at[slot])


def _wait_tile(y_hbm, ybuf, sem, slot):
    tm = ybuf.shape[2]
    for k in range(2):
        pltpu.make_async_copy(y_hbm.at[pl.ds(0, tm)], ybuf.at[slot, k], sem.at[slot]).wait()


def _moe_ple_update(refs, tail):
    (pos1_ref, pos2_ref, nxt1_ref, nxt2_ref, y_hbm, route_ref, h_ref, p_ref, pg_ref, wgate_ref, wproj_ref,
     ybuf, sem) = refs
    i = pl.program_id(0)
    slot = i & 1
    nslot = 1 - slot
    tm = ybuf.shape[2]

    @pl.when(i == 0)
    def _():
        def issue(r, carry):
            _gather_copy(pos1_ref, y_hbm, ybuf, sem, 0, 0, r).start(priority=0)
            _gather_copy(pos2_ref, y_hbm, ybuf, sem, 0, 1, r).start(priority=1)
            return carry

        lax.fori_loop(0, tm, issue, 0)

    _wait_tile(y_hbm, ybuf, sem, slot)
    rows_per = tm // GATHER_CHUNKS

    def prefetch(c):
        for r in range(c * rows_per, (c + 1) * rows_per):
            _gather_copy(nxt1_ref, y_hbm, ybuf, sem, nslot, 0, r).start(priority=0)
            _gather_copy(nxt2_ref, y_hbm, ybuf, sem, nslot, 1, r).start(priority=1)

    proj = _dot(p_ref[...].astype(BF16), wproj_ref[...])
    prefetch(0)
    route = route_ref[...]
    h = h_ref[...] + (route[:, 2:3] * ybuf[slot, 0] + route[:, 3:4] * ybuf[slot, 1])
    prefetch(1)
    gate = jax.nn.sigmoid(_dot(_rms(h, pg_ref[...]).astype(BF16), wgate_ref[...]))
    prefetch(2)
    tail(h + gate * proj, prefetch)

    @pl.when(i == pl.num_programs(0) - 1)
    def _():
        _wait_tile(y_hbm, ybuf, sem, nslot)


def _ple_mid_kernel(*refs):
    kvg_ref, wkv_ref, qg_ref, wq_ref, h3_ref, kv_ref, q_ref = refs[11:18]

    def tail(h3, prefetch):
        h3_ref[...] = h3
        prefetch(3)
        kn = _rms(h3, kvg_ref[...]).astype(BF16)
        for c in range(2):
            kv_ref[:, c * D_MODEL:(c + 1) * D_MODEL] = _dot(kn, wkv_ref[:, c * D_MODEL:(c + 1) * D_MODEL]).astype(BF16)
            prefetch(4 + c)
        q_ref[...] = _dot(_rms(h3, qg_ref[...]).astype(BF16), wq_ref[...]).astype(BF16)
        prefetch(6)
        prefetch(7)

    _moe_ple_update(refs[:11] + refs[18:], tail)


def _ple_last_kernel(*refs):
    fg_ref, o_ref = refs[11:13]

    def tail(h3, prefetch):
        o_ref[...] = _rms(h3, fg_ref[...])
        for c in range(3, GATHER_CHUNKS):
            prefetch(c)

    _moe_ple_update(refs[:11] + refs[13:], tail)


def _ple_call(body, name, moe_in, h1, p2, vecs_and_weights, in_tail, out_specs, out_shape):
    pos1, pos2, y_sorted, route = moe_in
    t = h1.shape[0]
    tm = ROW_TILE
    n = t // tm
    row = lambda w: pl.BlockSpec((tm, w), lambda i: (i, 0))
    idx = lambda: pl.BlockSpec((1, 1, tm), lambda i: (i, 0, 0), memory_space=pltpu.SMEM)
    nxt = lambda: pl.BlockSpec((1, 1, tm), lambda i: (jnp.minimum(i + 1, n - 1), 0, 0), memory_space=pltpu.SMEM)
    pos1, pos2 = pos1.reshape(n, 1, tm), pos2.reshape(n, 1, tm)
    return pl.pallas_call(
        body,
        grid=(n,),
        in_specs=[idx(), idx(), nxt(), nxt(), pl.BlockSpec(memory_space=pl.ANY), row(LANES), row(D_MODEL),
                  row(PLE_DIM)] + in_tail,
        out_specs=out_specs, out_shape=out_shape,
        scratch_shapes=[pltpu.VMEM((2, 2, tm, D_MODEL), F32), pltpu.SemaphoreType.DMA((2,))],
        compiler_params=_params(1), name=name,
    )(pos1, pos2, pos1, pos2, y_sorted, route, h1, p2, *vecs_and_weights)


def _ple_mid(moe_in, h1, p2, ple_gain, wgate, wproj, kv_gain, wkv, q_gain, wq):
    t = h1.shape[0]
    tm = ROW_TILE
    row = lambda w: pl.BlockSpec((tm, w), lambda i: (i, 0))
    vec = _full((1, D_MODEL))
    return _ple_call(
        _ple_mid_kernel, "moe_combine_ple_kv_q", moe_in, h1, p2,
        (ple_gain.reshape(1, -1), wgate, wproj, kv_gain.reshape(1, -1), wkv, q_gain.reshape(1, -1), wq),
        [vec, _full((D_MODEL, D_MODEL)), _full((PLE_DIM, D_MODEL)), vec, _full((D_MODEL, 2 * D_MODEL)), vec,
         _full((D_MODEL, D_MODEL))],
        [row(D_MODEL), row(2 * D_MODEL), row(D_MODEL)],
        [jax.ShapeDtypeStruct((t, D_MODEL), F32), jax.ShapeDtypeStruct((t, 2 * D_MODEL), BF16),
         jax.ShapeDtypeStruct((t, D_MODEL), BF16)])


def _ple_last(moe_in, h1, p2, ple_gain, wgate, wproj, final_gain):
    t = h1.shape[0]
    tm = ROW_TILE
    vec = _full((1, D_MODEL))
    return _ple_call(
        _ple_last_kernel, "moe_combine_ple_final_norm", moe_in, h1, p2,
        (ple_gain.reshape(1, -1), wgate, wproj, final_gain.reshape(1, -1)),
        [vec, _full((D_MODEL, D_MODEL)), _full((PLE_DIM, D_MODEL)), vec],
        pl.BlockSpec((tm, D_MODEL), lambda i: (i, 0)),
        jax.ShapeDtypeStruct((t, D_MODEL), F32))


def _router_operands(w_group, b_group, w_router, b_router):
    gap = LANES // 2 - MOE_EXPERTS
    w = jnp.pad(jnp.concatenate([jnp.repeat(w_group, MOE_EXPERTS_PER_GROUP, axis=1),
                                 jnp.zeros((D_MODEL, gap), F32), w_router], axis=1), ((0, 0), (0, gap)))
    b = jnp.pad(jnp.concatenate([jnp.repeat(b_group, MOE_EXPERTS_PER_GROUP), jnp.zeros((gap,), F32), b_router]),
                (0, gap)).reshape(1, LANES)
    w_hi = w.astype(BF16)
    w_lo = (w - w_hi.astype(F32)).astype(BF16)
    return w_hi, w_lo, b


def _moe_layer(att, h, wo, i, ffn_norm, w_group, b_group, w_router, b_router, w_gate, w_up, w_down):
    t = h.shape[0]
    h1, xn, route, counts8 = _attn_out(att, h, wo.astype(BF16), ffn_norm[i],
                                       *_router_operands(w_group[i], b_group[i], w_router[i], b_router[i]))
    pos1, pos2, tile_expert, n_used, last_row, tiles_per, n_tiles = _routing_tables(route, counts8, t)
    x_sorted = _scatter_rows(xn, pos1, pos2, last_row, tiles_per, n_used, n_tiles * GROUP_TILE)
    wg = w_gate[i].reshape(MOE_EXPERTS, D_MODEL, MOE_D_FF).astype(BF16)
    wu = w_up[i].reshape(MOE_EXPERTS, D_MODEL, MOE_D_FF).astype(BF16)
    wd = w_down[i].reshape(MOE_EXPERTS, MOE_D_FF, D_MODEL).astype(BF16)
    y_sorted = _moe_group(x_sorted, tile_expert, wg, wu, wd)
    return (pos1, pos2, y_sorted, route), h1


def kernel(x, p, rel_bias, attn_norm_a, w_qkv_a, w_o_a, kv_norm, w_kv, attn_norm_b, w_q_b, lambda_q1, lambda_k1,
           lambda_q2, lambda_k2, subln_b, w_o_b, ffn_norm, w_group, b_group, w_router, b_router, w_gate, w_up,
           w_down, ple_norm, w_ple_gate, w_ple_proj, final_norm):
    batch, seq, d = x.shape
    assert d == D_MODEL and seq % ATT_TQ == 0 and seq // MOBA_BLOCK <= 16
    t = batch * seq
    assert t % ROW_TILE == 0
    n_hp = D_MODEL // LANES
    moe = (ffn_norm, w_group, b_group, w_router, b_router, w_gate, w_up, w_down)

    h = x.reshape(t, d)
    qkv, kmean = _qkv_proj(h, attn_norm_a[0], w_qkv_a[0].astype(BF16))
    qkv3 = qkv.reshape(batch, seq, 3 * d)
    att = _attention("moba", qkv3, qkv3, rel_bias, kmean, batch=batch, seq=seq,
                     q_col=0, k_col=n_hp, v_col=2 * n_hp)
    moe_out, h = _moe_layer(att.reshape(t, d), h, w_o_a[0], 0, *moe)
    h, kv, q = _ple_mid(moe_out, h, p[0].reshape(t, PLE_DIM), ple_norm[0], w_ple_gate[0].astype(BF16),
                        w_ple_proj[0].astype(BF16), kv_norm, w_kv.astype(BF16), attn_norm_b[0],
                        w_q_b[0].astype(BF16))

    lam_init = 0.8 - 0.6 * math.exp(-0.3 * 1)
    lam_rows = jnp.pad(jnp.stack([lambda_q1[0], lambda_k1[0], lambda_q2[0], lambda_k2[0]]).astype(F32),
                       ((0, 4), (0, LANES - HEAD_DIM)))
    att = _attention("diff", q.reshape(batch, seq, d), kv.reshape(batch, seq, 2 * d), rel_bias,
                     (lam_rows, subln_b[0].reshape(1, LANES)), batch=batch, seq=seq,
                     q_col=0, k_col=0, v_col=n_hp, lam_init=lam_init)
    moe_out, h = _moe_layer(att.reshape(t, d), h, w_o_b[0], 1, *moe)
    out = _ple_last(moe_out, h, p[1].reshape(t, PLE_DIM), ple_norm[1], w_ple_gate[1].astype(BF16),
                    w_ple_proj[1].astype(BF16), final_norm)
    return out.reshape(batch, seq, d)
```

```python
import functools
import math

import jax
import jax.numpy as jnp
from jax import lax
from jax.experimental import pallas as pl
from jax.experimental.pallas import tpu as pltpu

F32 = jnp.float32
BF16 = jnp.bfloat16

D_MODEL = 1024
DEPTH = 2
N_A_LAYERS = DEPTH // 2
HEAD_DIM = 64
LANES = 128
MOBA_BLOCK = 256
ATT_TQ = 2 * MOBA_BLOCK
MOBA_TOP_K = 3
REL_BUCKETS = 32
REL_MAX_DISTANCE = 128
MOE_GROUPS = 4
MOE_EXPERTS_PER_GROUP = 8
MOE_EXPERTS = MOE_GROUPS * MOE_EXPERTS_PER_GROUP
MOE_D_FF = D_MODEL // 4
PLE_DIM = 256
RMS_EPS = 1e-6
NEG = -1e30
LOG2E = math.log2(math.e)

ROW_TILE = 512
GROUP_TILE = 512
GATHER_CHUNKS = 8
VMEM_LIMIT = 52 * 1024 * 1024


def _dot(a, b):
    return jnp.dot(a, b, preferred_element_type=F32)


def _rms(x, gain):
    y = x * lax.rsqrt(jnp.mean(x * x, axis=-1, keepdims=True) + RMS_EPS)
    return y * gain


def _params(n_axes):
    return pltpu.CompilerParams(dimension_semantics=("arbitrary",) * n_axes,
                                vmem_limit_bytes=VMEM_LIMIT)


def _full(shape):
    nd = len(shape)
    return pl.BlockSpec(shape, lambda *_: (0,) * nd)


def _qkv_kernel(x_ref, g_ref, w_ref, qkv_ref, kmean_ref):
    hn = _rms(x_ref[...], g_ref[...]).astype(BF16)
    for c in range(3):
        y = _dot(hn, w_ref[:, c * D_MODEL:(c + 1) * D_MODEL])
        qkv_ref[:, c * D_MODEL:(c + 1) * D_MODEL] = y.astype(BF16)
        if c == 1:
            nb = y.shape[0] // MOBA_BLOCK
            kmean_ref[...] = jnp.mean(y.reshape(nb, MOBA_BLOCK, D_MODEL), axis=1, keepdims=True)


def _qkv_proj(x2, gain, w_bf16):
    t = x2.shape[0]
    tm = ROW_TILE
    return pl.pallas_call(
        _qkv_kernel,
        grid=(t // tm,),
        in_specs=[pl.BlockSpec((tm, D_MODEL), lambda i: (i, 0)),
                  _full((1, D_MODEL)),
                  _full((D_MODEL, 3 * D_MODEL))],
        out_specs=[pl.BlockSpec((tm, 3 * D_MODEL), lambda i: (i, 0)),
                   pl.BlockSpec((tm // MOBA_BLOCK, 1, D_MODEL), lambda i: (i, 0, 0))],
        out_shape=[jax.ShapeDtypeStruct((t, 3 * D_MODEL), BF16),
                   jax.ShapeDtypeStruct((t // MOBA_BLOCK, 1, D_MODEL), F32)],
        compiler_params=_params(1),
        name="qkv_proj",
    )(x2, gain.reshape(1, D_MODEL), w_bf16)


def _rel_bucket(dist):
    n = jnp.maximum(dist, 0)
    max_exact = REL_BUCKETS // 2
    nf = jnp.maximum(n, max_exact).astype(F32)
    large = max_exact + (jnp.log(nf / max_exact) / math.log(REL_MAX_DISTANCE / max_exact)
                         * (REL_BUCKETS - max_exact)).astype(jnp.int32)
    large = jnp.minimum(large, REL_BUCKETS - 1)
    return jnp.where(n < max_exact, n, large)


def _shifted_bias(tab_ref, col, dist):
    bkt = _rel_bucket(dist)
    acc = jnp.zeros(dist.shape, F32)
    for i in range(REL_BUCKETS):
        acc = jnp.where(bkt == i, tab_ref[i, col], acc)
    return (acc - tab_ref[REL_BUCKETS - 1, col]) * LOG2E


def _build_bias(tab_ref, bias_ref, corner_ref, col0):
    tb = MOBA_BLOCK
    rows = 64
    for r0 in range(0, tb, rows):
        key = lax.broadcasted_iota(jnp.int32, (rows, tb), 0) + r0
        qry = lax.broadcasted_iota(jnp.int32, (rows, tb), 1)
        dist = qry - key
        for h in range(2):
            bias_ref[h, r0:r0 + rows, :] = jnp.where(dist >= 0, _shifted_bias(tab_ref, col0 + h, dist), NEG)
    key = lax.broadcasted_iota(jnp.int32, (REL_MAX_DISTANCE, REL_MAX_DISTANCE), 0) - REL_MAX_DISTANCE
    qry = lax.broadcasted_iota(jnp.int32, (REL_MAX_DISTANCE, REL_MAX_DISTANCE), 1)
    for h in range(2):
        corner_ref[h] = _shifted_bias(tab_ref, col0 + h, qry - key)


def _build_kv(k_ref, v_ref, ka_ref, vt_ref, seq, mask_v):
    tb = MOBA_BLOCK
    nb = seq // tb
    lane = lax.broadcasted_iota(jnp.int32, (tb, LANES), 1)
    first = lane < HEAD_DIM

    def body(j, _):
        r = pl.multiple_of(j * tb, tb)
        k = k_ref[0, pl.ds(r, tb), :].astype(F32)
        oh0 = (lane == HEAD_DIM + j).astype(F32)
        oh1 = (lane == j).astype(F32)
        ka_ref[0, pl.ds(r, tb), :] = jnp.where(first, k, oh0).astype(BF16)
        ka_ref[1, pl.ds(r, tb), :] = jnp.where(first, oh1, k).astype(BF16)
        v_t = v_ref[0, pl.ds(r, tb), :].astype(F32).T
        if mask_v:
            none = jnp.zeros((HEAD_DIM, tb), F32)
            vt_ref[j] = jnp.concatenate([v_t[0:HEAD_DIM], none], axis=0).astype(BF16)
            vt_ref[nb + j] = jnp.concatenate([none, v_t[HEAD_DIM:]], axis=0).astype(BF16)
        else:
            vt_ref[j] = v_t.astype(BF16)
        return 0

    lax.fori_loop(0, nb, body, 0)


def _block_penalties(gate, own):
    n = lax.broadcasted_iota(jnp.int32, (16, own.shape[1]), 0)
    pen = jnp.where(n == own, 0.0, NEG)
    if gate is None:
        return jnp.where(n < own, 0.0, pen)
    g = jnp.where(n < own, gate, -jnp.inf)
    for _ in range(MOBA_TOP_K):
        mx = jnp.max(g, axis=0, keepdims=True)
        idx = jnp.min(jnp.where(g == mx, n, 1 << 20), axis=0, keepdims=True)
        pick = (n == idx) & (mx > -jnp.inf)
        pen = jnp.where(pick, 0.0, pen)
        g = jnp.where(pick, -jnp.inf, g)
    return pen


def _own_block(qi):
    col = lax.broadcasted_iota(jnp.int32, (1, ATT_TQ), 1)
    return qi * (ATT_TQ // MOBA_BLOCK) + col // MOBA_BLOCK


def _augment_queries(qs_t, pens):
    nq = qs_t.shape[1]
    out = []
    for h in range(2):
        tail = [pens[h], jnp.zeros((HEAD_DIM - 16, nq), F32)]
        parts = [qs_t[0:HEAD_DIM]] + tail if h == 0 else tail + [qs_t[HEAD_DIM:]]
        out.append(jnp.concatenate(parts, axis=0).astype(BF16))
    return out


def _flash(qa, ka_ref, vt_ref, bias_ref, corner_ref, qi, v_base, bufs):
    tb = MOBA_BLOCK
    near = REL_MAX_DISTANCE
    s_a, s_b, m_ref, l_ref, acc_ref = bufs

    def scores_into(buf, j, own=False):
        r = pl.multiple_of((qi - j) * ATT_TQ, ATT_TQ)
        prev = jnp.where(j == 1, 1.0, 0.0).astype(F32)
        for h in range(2):
            s = _dot(ka_ref[h, pl.ds(r, ATT_TQ), :], qa[h])
            buf[h] = s
            if own:
                for lo in (0, tb):
                    buf[h, lo:lo + tb, lo:lo + tb] = s[lo:lo + tb, lo:lo + tb] + bias_ref[h]
                buf[h, tb - near:tb, tb:tb + near] = s[tb - near:tb, tb:tb + near] + corner_ref[h]
            else:
                buf[h, ATT_TQ - near:ATT_TQ, 0:near] = s[ATT_TQ - near:, 0:near] + prev * corner_ref[h]

    def update(buf, j):
        blk0 = (qi - j) * (ATT_TQ // tb)
        for h in range(2):
            s = buf[h]
            m = m_ref[h]
            m_new = jnp.maximum(m, jnp.max(s, axis=0, keepdims=True))
            alpha = jnp.exp2(m - m_new)
            p = jnp.exp2(s - m_new)
            pb = p.astype(BF16)
            pv = _dot(vt_ref[v_base[h] + blk0], pb[0:tb]) + _dot(vt_ref[v_base[h] + blk0 + 1], pb[tb:])
            m_ref[h] = m_new
            l_ref[h] = alpha * l_ref[h] + jnp.sum(p, axis=0, keepdims=True)
            acc_ref[h] = alpha * acc_ref[h] + pv

    m_ref[...] = jnp.full(m_ref.shape, 3.0 * NEG, F32)
    l_ref[...] = jnp.zeros(l_ref.shape, F32)
    acc_ref[...] = jnp.zeros(acc_ref.shape, F32)
    scores_into(s_a, 0, own=True)

    def body(t, carry):
        scores_into(s_b, 2 * t + 1)
        update(s_a, 2 * t)
        scores_into(s_a, 2 * t + 2)
        update(s_b, 2 * t + 1)
        return carry

    lax.fori_loop(0, lax.shift_right_logical(qi, 1), body, 0)
    odd = (qi & 1) == 1

    @pl.when(odd)
    def _():
        scores_into(s_b, qi)
        update(s_a, qi - 1)
        update(s_b, qi)

    @pl.when(jnp.logical_not(odd))
    def _():
        update(s_a, qi)

    return [(acc_ref[h], l_ref[h]) for h in range(2)]


def _moba_kernel(tab_ref, q_ref, k_ref, v_ref, km_ref, o_ref, ka_ref, vt_ref, bias_ref, corner_ref, *bufs, seq):
    hp, b, qi = pl.program_id(0), pl.program_id(1), pl.program_id(2)
    nb = seq // MOBA_BLOCK

    @pl.when((b == 0) & (qi == 0))
    def _():
        _build_bias(tab_ref, bias_ref, corner_ref, 2 * hp)

    @pl.when(qi == 0)
    def _():
        _build_kv(k_ref, v_ref, ka_ref, vt_ref, seq, mask_v=True)

    q_t = q_ref[0].astype(F32).T
    km = km_ref[:, 0, :]
    lane_k = lax.broadcasted_iota(jnp.int32, (nb, LANES), 1)
    km0 = jnp.where(lane_k < HEAD_DIM, km, 0.0)
    km1 = jnp.where(lane_k < HEAD_DIM, 0.0, km)
    pad = jnp.zeros((HEAD_DIM - nb, LANES), F32)
    kmx = jnp.concatenate([km1, pad, km0, pad], axis=0)
    kmx_hi = kmx.astype(BF16)
    kmx_lo = (kmx - kmx_hi.astype(F32)).astype(BF16)
    q_bf = q_t.astype(BF16)
    gate = _dot(kmx_hi, q_bf) + _dot(kmx_lo, q_bf)
    pens = [_block_penalties(gate[base:base + 16], _own_block(qi)) for base in (HEAD_DIM, 0)]

    qa = _augment_queries(q_t * (HEAD_DIM ** -0.5 * LOG2E), pens)
    (a0, l0), (a1, l1) = _flash(qa, ka_ref, vt_ref, bias_ref, corner_ref, qi, (0, nb), bufs)
    o_ref[0] = (a0 / l0 + a1 / l1).T.astype(o_ref.dtype)


def _diff_kernel(tab_ref, lam_ref, q_ref, k_ref, v_ref, sg_ref, o_ref, ka_ref, vt_ref, bias_ref, corner_ref, *bufs,
                 seq, lam_init):
    hd, b, qi = pl.program_id(0), pl.program_id(1), pl.program_id(2)

    @pl.when((b == 0) & (qi == 0))
    def _():
        _build_bias(tab_ref, bias_ref, corner_ref, 2 * hd)

    @pl.when(qi == 0)
    def _():
        _build_kv(k_ref, v_ref, ka_ref, vt_ref, seq, mask_v=False)

    lv = lam_ref[...]
    lam = (jnp.exp(jnp.sum(lv[0:1] * lv[1:2], axis=1, keepdims=True))
           - jnp.exp(jnp.sum(lv[2:3] * lv[3:4], axis=1, keepdims=True)) + lam_init)

    q_t = q_ref[0].astype(F32).T
    qa = _augment_queries(q_t * (HEAD_DIM ** -0.5 * LOG2E), [_block_penalties(None, _own_block(qi))] * 2)
    (a0, l0), (a1, l1) = _flash(qa, ka_ref, vt_ref, bias_ref, corner_ref, qi, (0, 0), bufs)
    att = a0 / l0 - lam * (a1 / l1)
    y = att * lax.rsqrt(jnp.mean(att * att, axis=0, keepdims=True) + RMS_EPS)
    o_ref[0] = ((y.T * sg_ref[...]) * (1.0 - lam_init)).astype(o_ref.dtype)


def _attention(kind, q_src, kv_src, rel_bias, extra, *, batch, seq, q_col, k_col, v_col, lam_init=None):
    tq = ATT_TQ
    nb = seq // MOBA_BLOCK
    n_hp = D_MODEL // LANES
    grid = (n_hp, batch, seq // tq)
    smem = pl.BlockSpec(memory_space=pltpu.SMEM)
    q_spec = pl.BlockSpec((1, tq, LANES), lambda h, b, i: (b, i, q_col + h))
    k_spec = pl.BlockSpec((1, seq, LANES), lambda h, b, i: (b, 0, k_col + h))
    v_spec = pl.BlockSpec((1, seq, LANES), lambda h, b, i: (b, 0, v_col + h))
    o_spec = pl.BlockSpec((1, tq, LANES), lambda h, b, i: (b, i, h))
    n_vt = 2 * nb if kind == "moba" else nb
    scratch = [pltpu.VMEM((2, seq, LANES), BF16),
               pltpu.VMEM((n_vt, LANES, MOBA_BLOCK), BF16),
               pltpu.VMEM((2, MOBA_BLOCK, MOBA_BLOCK), F32),
               pltpu.VMEM((2, REL_MAX_DISTANCE, REL_MAX_DISTANCE), F32),
               pltpu.VMEM((2, tq, tq), F32), pltpu.VMEM((2, tq, tq), F32),
               pltpu.VMEM((2, 1, tq), F32), pltpu.VMEM((2, 1, tq), F32), pltpu.VMEM((2, LANES, tq), F32)]
    out_shape = jax.ShapeDtypeStruct((batch, seq, D_MODEL), BF16)
    if kind == "moba":
        kmean = extra
        km_spec = pl.BlockSpec((nb, 1, LANES), lambda h, b, i: (b, 0, h))
        return pl.pallas_call(
            functools.partial(_moba_kernel, seq=seq),
            grid=grid,
            in_specs=[smem, q_spec, k_spec, v_spec, km_spec],
            out_specs=o_spec, out_shape=out_shape, scratch_shapes=scratch,
            compiler_params=_params(3), name="moba_attention",
        )(rel_bias, q_src, kv_src, kv_src, kmean)
    lam_rows, sub_gain = extra
    return pl.pallas_call(
        functools.partial(_diff_kernel, seq=seq, lam_init=lam_init),
        grid=grid,
        in_specs=[smem, _full((8, LANES)), q_spec, k_spec, v_spec, _full((1, LANES))],
        out_specs=o_spec, out_shape=out_shape, scratch_shapes=scratch,
        compiler_params=_params(3), name="diff_attention",
    )(rel_bias, lam_rows, q_src, kv_src, kv_src, sub_gain)


def _router(xn, w_hi_ref, w_lo_ref, b_ref, tri_ref, count_ref):
    lane = lax.broadcasted_iota(jnp.int32, (xn.shape[0], LANES), 1)
    live = lane < MOE_EXPERTS
    x_hi = xn.astype(BF16)
    x_lo = (xn - x_hi.astype(F32)).astype(BF16)
    logits = (_dot(x_hi, w_hi_ref[...]) + (_dot(x_hi, w_lo_ref[...]) + _dot(x_lo, w_hi_ref[...]))) + b_ref[...]
    gl = jnp.where(live, logits, -jnp.inf)
    el = pltpu.roll(logits, LANES // 2, axis=1)
    gmax = jnp.max(gl, axis=1, keepdims=True)
    gsum = jnp.sum(jnp.exp(gl - gmax), axis=1, keepdims=True) / MOE_EXPERTS_PER_GROUP
    g_gate = 1.0 / gsum
    lane_f = lane.astype(F32)
    first_lane = lambda hit: jnp.min(jnp.where(hit, lane_f, float(LANES)), axis=1, keepdims=True)
    group_of = lambda lanes: jnp.floor(lanes * (1.0 / MOE_EXPERTS_PER_GROUP))
    in_group = group_of(lane_f) == group_of(first_lane(gl == gmax))
    em = jnp.where(in_group & live, el, -jnp.inf)
    m1 = jnp.max(em, axis=1, keepdims=True)
    i1 = first_lane(em == m1)
    em2 = jnp.where(lane_f == i1, -jnp.inf, em)
    m2 = jnp.max(em2, axis=1, keepdims=True)
    i2 = first_lane(em2 == m2)
    p2 = jnp.exp(m2 - m1)
    w1 = g_gate * (1.0 / (1.0 + p2))
    w2 = g_gate * (p2 / (1.0 + p2))

    is1, is2 = lane_f == i1, lane_f == i2
    member = jnp.where(is1 | is2, 1.0, 0.0)
    before = _dot(tri_ref[...], member.astype(BF16)) + count_ref[...]
    rank1 = jnp.sum(jnp.where(is1, before, 0.0), axis=1, keepdims=True)
    rank2 = jnp.sum(jnp.where(is2, before, 0.0), axis=1, keepdims=True)
    count_ref[...] = count_ref[...] + jnp.sum(member, axis=0, keepdims=True)
    fields = (i1, i2, w1, w2, rank1, rank2)
    route = jnp.zeros(lane.shape, F32)
    for k, col in enumerate(fields):
        route = jnp.where(lane == k, col, route)
    return route


def _attn_out_kernel(a_ref, h_ref, wo_ref, fg_ref, w_hi_ref, w_lo_ref, b_ref, tri_ref,
                     h1_ref, xn_ref, route_ref, counts_ref, count_acc):
    @pl.when(pl.program_id(0) == 0)
    def _():
        count_acc[...] = jnp.zeros_like(count_acc)

    h1 = h_ref[...] + _dot(a_ref[...], wo_ref[...])
    h1_ref[...] = h1
    xn = _rms(h1, fg_ref[...])
    xn_ref[...] = xn
    route_ref[...] = _router(xn, w_hi_ref, w_lo_ref, b_ref, tri_ref, count_acc)
    counts_ref[...] = jnp.broadcast_to(count_acc[...], counts_ref.shape)


def _attn_out(att2, h2, wo_bf16, ffn_gain, w_hi, w_lo, b_x):
    t = h2.shape[0]
    tm = ROW_TILE
    row = lambda w: pl.BlockSpec((tm, w), lambda i: (i, 0))
    tri = jnp.tril(jnp.ones((tm, tm), F32), -1).astype(BF16)
    return pl.pallas_call(
        _attn_out_kernel,
        grid=(t // tm,),
        in_specs=[row(D_MODEL), row(D_MODEL), _full((D_MODEL, D_MODEL)), _full((1, D_MODEL)),
                  _full((D_MODEL, LANES)), _full((D_MODEL, LANES)), _full((1, LANES)), _full((tm, tm))],
        out_specs=[row(D_MODEL), row(D_MODEL), row(LANES), _full((8, LANES))],
        out_shape=[jax.ShapeDtypeStruct((t, D_MODEL), F32), jax.ShapeDtypeStruct((t, D_MODEL), F32),
                   jax.ShapeDtypeStruct((t, LANES), F32), jax.ShapeDtypeStruct((8, LANES), F32)],
        scratch_shapes=[pltpu.VMEM((1, LANES), F32)],
        compiler_params=_params(1), name="attn_out_router",
    )(att2, h2, wo_bf16, ffn_gain.reshape(1, D_MODEL), w_hi, w_lo, b_x, tri)


def _scatter_kernel(last_row_ref, tiles_ref, used_ref, pos1_ref, pos2_ref, xn_ref, out_hbm, zeros_ref, sems):
    tm = pos1_ref.shape[2]
    row_sem, zero_sem = sems.at[0], sems.at[1]
    n_tiles = out_hbm.shape[0] // GROUP_TILE

    def zero_tile(row):
        return pltpu.make_async_copy(zeros_ref, out_hbm.at[pl.ds(pl.multiple_of(row, GROUP_TILE), GROUP_TILE)], zero_sem)

    def spare_tiles(fn):
        def body(w, carry):
            fn(zero_tile(w * GROUP_TILE))
            return carry

        lax.fori_loop(used_ref[0], n_tiles, body, 0)

    @pl.when(pl.program_id(0) == 0)
    def _():
        zeros_ref[...] = jnp.zeros_like(zeros_ref)
        for e in range(MOE_EXPERTS):
            @pl.when(tiles_ref[e] > 0)
            def _():
                zero_tile(last_row_ref[e]).start()

        for e in range(MOE_EXPERTS):
            @pl.when(tiles_ref[e] > 0)
            def _():
                zero_tile(last_row_ref[e]).wait()

        spare_tiles(lambda cp: cp.start())

    def issue(r, carry):
        src = xn_ref.at[pl.ds(r, 1)]
        pltpu.make_async_copy(src, out_hbm.at[pl.ds(pos1_ref[0, 0, r], 1)], row_sem).start(priority=0)
        pltpu.make_async_copy(src, out_hbm.at[pl.ds(pos2_ref[0, 0, r], 1)], row_sem).start(priority=1)
        return carry

    lax.fori_loop(0, tm, issue, 0, unroll=8)
    for _ in range(2):
        pltpu.make_async_copy(xn_ref, out_hbm.at[pl.ds(0, tm)], row_sem).wait()

    @pl.when(pl.program_id(0) == pl.num_programs(0) - 1)
    def _():
        spare_tiles(lambda cp: cp.wait())


def _scatter_rows(xn, pos1, pos2, last_row, tiles_per, n_used, n_rows):
    t = xn.shape[0]
    tm = ROW_TILE
    idx = lambda: pl.BlockSpec((1, 1, tm), lambda i, *_: (i, 0, 0), memory_space=pltpu.SMEM)
    return pl.pallas_call(
        _scatter_kernel,
        grid_spec=pltpu.PrefetchScalarGridSpec(
            num_scalar_prefetch=3, grid=(t // tm,),
            in_specs=[idx(), idx(), pl.BlockSpec((tm, D_MODEL), lambda i, *_: (i, 0))],
            out_specs=pl.BlockSpec(memory_space=pl.ANY),
            scratch_shapes=[pltpu.VMEM((GROUP_TILE, D_MODEL), F32), pltpu.SemaphoreType.DMA((2,))]),
        out_shape=jax.ShapeDtypeStruct((n_rows, D_MODEL), F32),
        compiler_params=_params(1), name="moe_scatter",
    )(last_row, tiles_per, n_used, pos1.reshape(t // tm, 1, tm), pos2.reshape(t // tm, 1, tm), xn)


def _moe_group_kernel(te_ref, x_ref, wg_ref, wu_ref, wd_ref, y_ref):
    del te_ref
    x = x_ref[...].astype(BF16)
    g = _dot(x, wg_ref[0])
    u = _dot(x, wu_ref[0])
    hh = (g * jax.nn.sigmoid(g)) * u
    y_ref[...] = _dot(hh.astype(BF16), wd_ref[0])


def _moe_group(x_sorted, tile_expert, wg, wu, wd):
    n_tiles = tile_expert.shape[0]
    tmg = GROUP_TILE
    wspec = lambda a, b: pl.BlockSpec((1, a, b), lambda w, te: (te[w], 0, 0))
    rows = lambda: pl.BlockSpec((tmg, D_MODEL), lambda w, te: (w, 0))
    return pl.pallas_call(
        _moe_group_kernel,
        grid_spec=pltpu.PrefetchScalarGridSpec(
            num_scalar_prefetch=1, grid=(n_tiles,),
            in_specs=[rows(), wspec(D_MODEL, MOE_D_FF), wspec(D_MODEL, MOE_D_FF), wspec(MOE_D_FF, D_MODEL)],
            out_specs=rows()),
        out_shape=jax.ShapeDtypeStruct((n_tiles * tmg, D_MODEL), F32),
        compiler_params=_params(1), name="moe_group_ffn",
    )(tile_expert, x_sorted, wg, wu, wd)


def _routing_tables(route, counts8, t):
    n_tiles = 2 * t // GROUP_TILE + MOE_EXPERTS
    counts = counts8[0, :MOE_EXPERTS].astype(jnp.int32)
    tiles_per = (counts + GROUP_TILE - 1) // GROUP_TILE
    tile_end = jnp.cumsum(tiles_per)
    offsets = (tile_end - tiles_per) * GROUP_TILE
    last_row = jnp.maximum(tile_end - 1, 0) * GROUP_TILE
    ids = route[:, 0:2].astype(jnp.int32)
    ranks = route[:, 4:6].astype(jnp.int32)
    expert_ids = jnp.arange(MOE_EXPERTS, dtype=jnp.int32)
    pos = jnp.sum(jnp.where(ids[:, :, None] == expert_ids, offsets, 0), axis=2) + ranks
    tile_ids = jnp.arange(n_tiles, dtype=jnp.int32)
    tile_expert = jnp.minimum(jnp.sum((tile_end[None, :] <= tile_ids[:, None]).astype(jnp.int32), axis=1),
                              MOE_EXPERTS - 1)
    return pos[:, 0], pos[:, 1], tile_expert, tile_end[-1:], last_row, tiles_per, n_tiles


def _gather_copy(pos_ref, y_hbm, ybuf, sem, slot, k, r):
    return pltpu.make_async_copy(y_hbm.at[pl.ds(pos_ref[0, 0, r], 1)], ybuf.at[slot, k, pl.ds(r, 1)], sem.at[slot])


def _wait_tile(y_hbm, ybuf, sem, slot):
    tm = ybuf.shape[2]
    for k in range(2):
        pltpu.make_async_copy(y_hbm.at[pl.ds(0, tm)], ybuf.at[slot, k], sem.at[slot]).wait()


def _moe_ple_update(refs, tail):
    (pos1_ref, pos2_ref, nxt1_ref, nxt2_ref, y_hbm, route_ref, h_ref, p_ref, pg_ref, wgate_ref, wproj_ref,
     ybuf, sem) = refs
    i = pl.program_id(0)
    slot = i & 1
    nslot = 1 - slot
    tm = ybuf.shape[2]

    @pl.when(i == 0)
    def _():
        def issue(r, carry):
            _gather_copy(pos1_ref, y_hbm, ybuf, sem, 0, 0, r).start(priority=0)
            _gather_copy(pos2_ref, y_hbm, ybuf, sem, 0, 1, r).start(priority=1)
            return carry

        lax.fori_loop(0, tm, issue, 0)

    _wait_tile(y_hbm, ybuf, sem, slot)
    rows_per = tm // GATHER_CHUNKS

    def prefetch(c):
        for r in range(c * rows_per, (c + 1) * rows_per):
            _gather_copy(nxt1_ref, y_hbm, ybuf, sem, nslot, 0, r).start(priority=0)
            _gather_copy(nxt2_ref, y_hbm, ybuf, sem, nslot, 1, r).start(priority=1)

    proj = _dot(p_ref[...].astype(BF16), wproj_ref[...])
    prefetch(0)
    route = route_ref[...]
    h = h_ref[...] + (route[:, 2:3] * ybuf[slot, 0] + route[:, 3:4] * ybuf[slot, 1])
    prefetch(1)
    gate = jax.nn.sigmoid(_dot(_rms(h, pg_ref[...]).astype(BF16), wgate_ref[...]))
    prefetch(2)
    tail(h + gate * proj, prefetch)

    @pl.when(i == pl.num_programs(0) - 1)
    def _():
        _wait_tile(y_hbm, ybuf, sem, nslot)


def _ple_mid_kernel(*refs):
    kvg_ref, wkv_ref, qg_ref, wq_ref, h3_ref, kv_ref, q_ref = refs[11:18]

    def tail(h3, prefetch):
        h3_ref[...] = h3
        prefetch(3)
        kn = _rms(h3, kvg_ref[...]).astype(BF16)
        for c in range(2):
            kv_ref[:, c * D_MODEL:(c + 1) * D_MODEL] = _dot(kn, wkv_ref[:, c * D_MODEL:(c + 1) * D_MODEL]).astype(BF16)
            prefetch(4 + c)
        q_ref[...] = _dot(_rms(h3, qg_ref[...]).astype(BF16), wq_ref[...]).astype(BF16)
        prefetch(6)
        prefetch(7)

    _moe_ple_update(refs[:11] + refs[18:], tail)


def _ple_last_kernel(*refs):
    fg_ref, o_ref = refs[11:13]

    def tail(h3, prefetch):
        o_ref[...] = _rms(h3, fg_ref[...])
        for c in range(3, GATHER_CHUNKS):
            prefetch(c)

    _moe_ple_update(refs[:11] + refs[13:], tail)


def _ple_call(body, name, moe_in, h1, p2, vecs_and_weights, in_tail, out_specs, out_shape):
    pos1, pos2, y_sorted, route = moe_in
    t = h1.shape[0]
    tm = ROW_TILE
    n = t // tm
    row = lambda w: pl.BlockSpec((tm, w), lambda i: (i, 0))
    idx = lambda: pl.BlockSpec((1, 1, tm), lambda i: (i, 0, 0), memory_space=pltpu.SMEM)
    nxt = lambda: pl.BlockSpec((1, 1, tm), lambda i: (jnp.minimum(i + 1, n - 1), 0, 0), memory_space=pltpu.SMEM)
    pos1, pos2 = pos1.reshape(n, 1, tm), pos2.reshape(n, 1, tm)
    return pl.pallas_call(
        body,
        grid=(n,),
        in_specs=[idx(), idx(), nxt(), nxt(), pl.BlockSpec(memory_space=pl.ANY), row(LANES), row(D_MODEL),
                  row(PLE_DIM)] + in_tail,
        out_specs=out_specs, out_shape=out_shape,
        scratch_shapes=[pltpu.VMEM((2, 2, tm, D_MODEL), F32), pltpu.SemaphoreType.DMA((2,))],
        compiler_params=_params(1), name=name,
    )(pos1, pos2, pos1, pos2, y_sorted, route, h1, p2, *vecs_and_weights)


def _ple_mid(moe_in, h1, p2, ple_gain, wgate, wproj, kv_gain, wkv, q_gain, wq):
    t = h1.shape[0]
    tm = ROW_TILE
    row = lambda w: pl.BlockSpec((tm, w), lambda i: (i, 0))
    vec = _full((1, D_MODEL))
    return _ple_call(
        _ple_mid_kernel, "moe_combine_ple_kv_q", moe_in, h1, p2,
        (ple_gain.reshape(1, -1), wgate, wproj, kv_gain.reshape(1, -1), wkv, q_gain.reshape(1, -1), wq),
        [vec, _full((D_MODEL, D_MODEL)), _full((PLE_DIM, D_MODEL)), vec, _full((D_MODEL, 2 * D_MODEL)), vec,
         _full((D_MODEL, D_MODEL))],
        [row(D_MODEL), row(2 * D_MODEL), row(D_MODEL)],
        [jax.ShapeDtypeStruct((t, D_MODEL), F32), jax.ShapeDtypeStruct((t, 2 * D_MODEL), BF16),
         jax.ShapeDtypeStruct((t, D_MODEL), BF16)])


def _ple_last(moe_in, h1, p2, ple_gain, wgate, wproj, final_gain):
    t = h1.shape[0]
    tm = ROW_TILE
    vec = _full((1, D_MODEL))
    return _ple_call(
        _ple_last_kernel, "moe_combine_ple_final_norm", moe_in, h1, p2,
        (ple_gain.reshape(1, -1), wgate, wproj, final_gain.reshape(1, -1)),
        [vec, _full((D_MODEL, D_MODEL)), _full((PLE_DIM, D_MODEL)), vec],
        pl.BlockSpec((tm, D_MODEL), lambda i: (i, 0)),
        jax.ShapeDtypeStruct((t, D_MODEL), F32))


def _router_operands(w_group, b_group, w_router, b_router):
    gap = LANES // 2 - MOE_EXPERTS
    w = jnp.pad(jnp.concatenate([jnp.repeat(w_group, MOE_EXPERTS_PER_GROUP, axis=1),
                                 jnp.zeros((D_MODEL, gap), F32), w_router], axis=1), ((0, 0), (0, gap)))
    b = jnp.pad(jnp.concatenate([jnp.repeat(b_group, MOE_EXPERTS_PER_GROUP), jnp.zeros((gap,), F32), b_router]),
                (0, gap)).reshape(1, LANES)
    w_hi = w.astype(BF16)
    w_lo = (w - w_hi.astype(F32)).astype(BF16)
    return w_hi, w_lo, b


def _moe_layer(att, h, wo, i, ffn_norm, w_group, b_group, w_router, b_router, w_gate, w_up, w_down):
    t = h.shape[0]
    h1, xn, route, counts8 = _attn_out(att, h, wo.astype(BF16), ffn_norm[i],
                                       *_router_operands(w_group[i], b_group[i], w_router[i], b_router[i]))
    pos1, pos2, tile_expert, n_used, last_row, tiles_per, n_tiles = _routing_tables(route, counts8, t)
    x_sorted = _scatter_rows(xn, pos1, pos2, last_row, tiles_per, n_used, n_tiles * GROUP_TILE)
    wg = w_gate[i].reshape(MOE_EXPERTS, D_MODEL, MOE_D_FF).astype(BF16)
    wu = w_up[i].reshape(MOE_EXPERTS, D_MODEL, MOE_D_FF).astype(BF16)
    wd = w_down[i].reshape(MOE_EXPERTS, MOE_D_FF, D_MODEL).astype(BF16)
    y_sorted = _moe_group(x_sorted, tile_expert, wg, wu, wd)
    return (pos1, pos2, y_sorted, route), h1


def kernel(x, p, rel_bias, attn_norm_a, w_qkv_a, w_o_a, kv_norm, w_kv, attn_norm_b, w_q_b, lambda_q1, lambda_k1,
           lambda_q2, lambda_k2, subln_b, w_o_b, ffn_norm, w_group, b_group, w_router, b_router, w_gate, w_up,
           w_down, ple_norm, w_ple_gate, w_ple_proj, final_norm):
    batch, seq, d = x.shape
    assert d == D_MODEL and seq % ATT_TQ == 0 and seq // MOBA_BLOCK <= 16
    t = batch * seq
    assert t % ROW_TILE == 0
    n_hp = D_MODEL // LANES
    moe = (ffn_norm, w_group, b_group, w_router, b_router, w_gate, w_up, w_down)

    h = x.reshape(t, d)
    qkv, kmean = _qkv_proj(h, attn_norm_a[0], w_qkv_a[0].astype(BF16))
    qkv3 = qkv.reshape(batch, seq, 3 * d)
    att = _attention("moba", qkv3, qkv3, rel_bias, kmean, batch=batch, seq=seq,
                     q_col=0, k_col=n_hp, v_col=2 * n_hp)
    moe_out, h = _moe_layer(att.reshape(t, d), h, w_o_a[0], 0, *moe)
    h, kv, q = _ple_mid(moe_out, h, p[0].reshape(t, PLE_DIM), ple_norm[0], w_ple_gate[0].astype(BF16),
                        w_ple_proj[0].astype(BF16), kv_norm, w_kv.astype(BF16), attn_norm_b[0],
                        w_q_b[0].astype(BF16))

    lam_init = 0.8 - 0.6 * math.exp(-0.3 * 1)
    lam_rows = jnp.pad(jnp.stack([lambda_q1[0], lambda_k1[0], lambda_q2[0], lambda_k2[0]]).astype(F32),
                       ((0, 4), (0, LANES - HEAD_DIM)))
    att = _attention("diff", q.reshape(batch, seq, d), kv.reshape(batch, seq, 2 * d), rel_bias,
                     (lam_rows, subln_b[0].reshape(1, LANES)), batch=batch, seq=seq,
                     q_col=0, k_col=0, v_col=n_hp, lam_init=lam_init)
    moe_out, h = _moe_layer(att.reshape(t, d), h, w_o_b[0], 1, *moe)
    out = _ple_last(moe_out, h, p[1].reshape(t, PLE_DIM), ple_norm[1], w_ple_gate[1].astype(BF16),
                    w_ple_proj[1].astype(BF16), final_norm)
    return out.reshape(batch, seq, d)
```

```python
import functools
import math

import jax
import jax.numpy as jnp
from jax import lax
from jax.experimental import pallas as pl
from jax.experimental.pallas import tpu as pltpu

F32 = jnp.float32
BF16 = jnp.bfloat16

D_MODEL = 1024
DEPTH = 2
N_A_LAYERS = DEPTH // 2
HEAD_DIM = 64
LANES = 128
MOBA_BLOCK = 256
ATT_TQ = 2 * MOBA_BLOCK
MOBA_TOP_K = 3
REL_BUCKETS = 32
REL_MAX_DISTANCE = 128
MOE_GROUPS = 4
MOE_EXPERTS_PER_GROUP = 8
MOE_EXPERTS = MOE_GROUPS * MOE_EXPERTS_PER_GROUP
MOE_D_FF = D_MODEL // 4
PLE_DIM = 256
RMS_EPS = 1e-6
NEG = -1e30
LOG2E = math.log2(math.e)

ROW_TILE = 512
GROUP_TILE = 512
GATHER_CHUNKS = 8
VMEM_LIMIT = 52 * 1024 * 1024


def _dot(a, b):
    return jnp.dot(a, b, preferred_element_type=F32)


def _rms(x, gain):
    y = x * lax.rsqrt(jnp.mean(x * x, axis=-1, keepdims=True) + RMS_EPS)
    return y * gain


def _params(n_axes):
    return pltpu.CompilerParams(dimension_semantics=("arbitrary",) * n_axes,
                                vmem_limit_bytes=VMEM_LIMIT)


def _full(shape):
    nd = len(shape)
    return pl.BlockSpec(shape, lambda *_: (0,) * nd)


def _qkv_kernel(x_ref, g_ref, w_ref, qkv_ref, kmean_ref):
    hn = _rms(x_ref[...], g_ref[...]).astype(BF16)
    for c in range(3):
        y = _dot(hn, w_ref[:, c * D_MODEL:(c + 1) * D_MODEL])
        qkv_ref[:, c * D_MODEL:(c + 1) * D_MODEL] = y.astype(BF16)
        if c == 1:
            nb = y.shape[0] // MOBA_BLOCK
            kmean_ref[...] = jnp.mean(y.reshape(nb, MOBA_BLOCK, D_MODEL), axis=1, keepdims=True)


def _qkv_proj(x2, gain, w_bf16):
    t = x2.shape[0]
    tm = ROW_TILE
    return pl.pallas_call(
        _qkv_kernel,
        grid=(t // tm,),
        in_specs=[pl.BlockSpec((tm, D_MODEL), lambda i: (i, 0)),
                  _full((1, D_MODEL)),
                  _full((D_MODEL, 3 * D_MODEL))],
        out_specs=[pl.BlockSpec((tm, 3 * D_MODEL), lambda i: (i, 0)),
                   pl.BlockSpec((tm // MOBA_BLOCK, 1, D_MODEL), lambda i: (i, 0, 0))],
        out_shape=[jax.ShapeDtypeStruct((t, 3 * D_MODEL), BF16),
                   jax.ShapeDtypeStruct((t // MOBA_BLOCK, 1, D_MODEL), F32)],
        compiler_params=_params(1),
        name="qkv_proj",
    )(x2, gain.reshape(1, D_MODEL), w_bf16)


def _rel_bucket(dist):
    n = jnp.maximum(dist, 0)
    max_exact = REL_BUCKETS // 2
    nf = jnp.maximum(n, max_exact).astype(F32)
    large = max_exact + (jnp.log(nf / max_exact) / math.log(REL_MAX_DISTANCE / max_exact)
                         * (REL_BUCKETS - max_exact)).astype(jnp.int32)
    large = jnp.minimum(large, REL_BUCKETS - 1)
    return jnp.where(n < max_exact, n, large)


def _shifted_bias(tab_ref, col, dist):
    bkt = _rel_bucket(dist)
    acc = jnp.zeros(dist.shape, F32)
    for i in range(REL_BUCKETS):
        acc = jnp.where(bkt == i, tab_ref[i, col], acc)
    return (acc - tab_ref[REL_BUCKETS - 1, col]) * LOG2E


def _build_bias(tab_ref, bias_ref, corner_ref, col0):
    tb = MOBA_BLOCK
    rows = 64
    for r0 in range(0, tb, rows):
        key = lax.broadcasted_iota(jnp.int32, (rows, tb), 0) + r0
        qry = lax.broadcasted_iota(jnp.int32, (rows, tb), 1)
        dist = qry - key
        for h in range(2):
            bias_ref[h, r0:r0 + rows, :] = jnp.where(dist >= 0, _shifted_bias(tab_ref, col0 + h, dist), NEG)
    key = lax.broadcasted_iota(jnp.int32, (REL_MAX_DISTANCE, REL_MAX_DISTANCE), 0) - REL_MAX_DISTANCE
    qry = lax.broadcasted_iota(jnp.int32, (REL_MAX_DISTANCE, REL_MAX_DISTANCE), 1)
    for h in range(2):
        corner_ref[h] = _shifted_bias(tab_ref, col0 + h, qry - key)


def _build_kv(k_ref, v_ref, ka_ref, vt_ref, seq, mask_v):
    tb = MOBA_BLOCK
    nb = seq // tb
    lane = lax.broadcasted_iota(jnp.int32, (tb, LANES), 1)
    first = lane < HEAD_DIM

    def body(j, _):
        r = pl.multiple_of(j * tb, tb)
        k = k_ref[0, pl.ds(r, tb), :].astype(F32)
        oh0 = (lane == HEAD_DIM + j).astype(F32)
        oh1 = (lane == j).astype(F32)
        ka_ref[0, pl.ds(r, tb), :] = jnp.where(first, k, oh0).astype(BF16)
        ka_ref[1, pl.ds(r, tb), :] = jnp.where(first, oh1, k).astype(BF16)
        v_t = v_ref[0, pl.ds(r, tb), :].astype(F32).T
        if mask_v:
            none = jnp.zeros((HEAD_DIM, tb), F32)
            vt_ref[j] = jnp.concatenate([v_t[0:HEAD_DIM], none], axis=0).astype(BF16)
            vt_ref[nb + j] = jnp.concatenate([none, v_t[HEAD_DIM:]], axis=0).astype(BF16)
        else:
            vt_ref[j] = v_t.astype(BF16)
        return 0

    lax.fori_loop(0, nb, body, 0)


def _block_penalties(gate, own):
    n = lax.broadcasted_iota(jnp.int32, (16, own.shape[1]), 0)
    pen = jnp.where(n == own, 0.0, NEG)
    if gate is None:
        return jnp.where(n < own, 0.0, pen)
    g = jnp.where(n < own, gate, -jnp.inf)
    for _ in range(MOBA_TOP_K):
        mx = jnp.max(g, axis=0, keepdims=True)
        idx = jnp.min(jnp.where(g == mx, n, 1 << 20), axis=0, keepdims=True)
        pick = (n == idx) & (mx > -jnp.inf)
        pen = jnp.where(pick, 0.0, pen)
        g = jnp.where(pick, -jnp.inf, g)
    return pen


def _own_block(qi):
    col = lax.broadcasted_iota(jnp.int32, (1, ATT_TQ), 1)
    return qi * (ATT_TQ // MOBA_BLOCK) + col // MOBA_BLOCK


def _augment_queries(qs_t, pens):
    nq = qs_t.shape[1]
    out = []
    for h in range(2):
        tail = [pens[h], jnp.zeros((HEAD_DIM - 16, nq), F32)]
        parts = [qs_t[0:HEAD_DIM]] + tail if h == 0 else tail + [qs_t[HEAD_DIM:]]
        out.append(jnp.concatenate(parts, axis=0).astype(BF16))
    return out


def _flash(qa, ka_ref, vt_ref, bias_ref, corner_ref, qi, v_base, bufs):
    tb = MOBA_BLOCK
    near = REL_MAX_DISTANCE
    s_a, s_b, m_ref, l_ref, acc_ref = bufs

    def scores_into(buf, j, own=False):
        r = pl.multiple_of((qi - j) * ATT_TQ, ATT_TQ)
        prev = jnp.where(j == 1, 1.0, 0.0).astype(F32)
        for h in range(2):
            s = _dot(ka_ref[h, pl.ds(r, ATT_TQ), :], qa[h])
            buf[h] = s
            if own:
                for lo in (0, tb):
                    buf[h, lo:lo + tb, lo:lo + tb] = s[lo:lo + tb, lo:lo + tb] + bias_ref[h]
                buf[h, tb - near:tb, tb:tb + near] = s[tb - near:tb, tb:tb + near] + corner_ref[h]
            else:
                buf[h, ATT_TQ - near:ATT_TQ, 0:near] = s[ATT_TQ - near:, 0:near] + prev * corner_ref[h]

    def update(buf, j):
        blk0 = (qi - j) * (ATT_TQ // tb)
        for h in range(2):
            s = buf[h]
            m = m_ref[h]
            m_new = jnp.maximum(m, jnp.max(s, axis=0, keepdims=True))
            alpha = jnp.exp2(m - m_new)
            p = jnp.exp2(s - m_new)
            pb = p.astype(BF16)
            pv = _dot(vt_ref[v_base[h] + blk0], pb[0:tb]) + _dot(vt_ref[v_base[h] + blk0 + 1], pb[tb:])
            m_ref[h] = m_new
            l_ref[h] = alpha * l_ref[h] + jnp.sum(p, axis=0, keepdims=True)
            acc_ref[h] = alpha * acc_ref[h] + pv

    m_ref[...] = jnp.full(m_ref.shape, 3.0 * NEG, F32)
    l_ref[...] = jnp.zeros(l_ref.shape, F32)
    acc_ref[...] = jnp.zeros(acc_ref.shape, F32)
    scores_into(s_a, 0, own=True)

    def body(t, carry):
        scores_into(s_b, 2 * t + 1)
        update(s_a, 2 * t)
        scores_into(s_a, 2 * t + 2)
        update(s_b, 2 * t + 1)
        return carry

    lax.fori_loop(0, lax.shift_right_logical(qi, 1), body, 0)
    odd = (qi & 1) == 1

    @pl.when(odd)
    def _():
        scores_into(s_b, qi)
        update(s_a, qi - 1)
        update(s_b, qi)

    @pl.when(jnp.logical_not(odd))
    def _():
        update(s_a, qi)

    return [(acc_ref[h], l_ref[h]) for h in range(2)]


def _moba_kernel(tab_ref, q_ref, k_ref, v_ref, km_ref, o_ref, ka_ref, vt_ref, bias_ref, corner_ref, *bufs, seq):
    hp, b, qi = pl.program_id(0), pl.program_id(1), pl.program_id(2)
    nb = seq // MOBA_BLOCK

    @pl.when((b == 0) & (qi == 0))
    def _():
        _build_bias(tab_ref, bias_ref, corner_ref, 2 * hp)

    @pl.when(qi == 0)
    def _():
        _build_kv(k_ref, v_ref, ka_ref, vt_ref, seq, mask_v=True)

    q_t = q_ref[0].astype(F32).T
    km = km_ref[:, 0, :]
    lane_k = lax.broadcasted_iota(jnp.int32, (nb, LANES), 1)
    km0 = jnp.where(lane_k < HEAD_DIM, km, 0.0)
    km1 = jnp.where(lane_k < HEAD_DIM, 0.0, km)
    pad = jnp.zeros((HEAD_DIM - nb, LANES), F32)
    kmx = jnp.concatenate([km1, pad, km0, pad], axis=0)
    kmx_hi = kmx.astype(BF16)
    kmx_lo = (kmx - kmx_hi.astype(F32)).astype(BF16)
    q_bf = q_t.astype(BF16)
    gate = _dot(kmx_hi, q_bf) + _dot(kmx_lo, q_bf)
    pens = [_block_penalties(gate[base:base + 16], _own_block(qi)) for base in (HEAD_DIM, 0)]

    qa = _augment_queries(q_t * (HEAD_DIM ** -0.5 * LOG2E), pens)
    (a0, l0), (a1, l1) = _flash(qa, ka_ref, vt_ref, bias_ref, corner_ref, qi, (0, nb), bufs)
    o_ref[0] = (a0 / l0 + a1 / l1).T.astype(o_ref.dtype)


def _diff_kernel(tab_ref, lam_ref, q_ref, k_ref, v_ref, sg_ref, o_ref, ka_ref, vt_ref, bias_ref, corner_ref, *bufs,
                 seq, lam_init):
    hd, b, qi = pl.program_id(0), pl.program_id(1), pl.program_id(2)

    @pl.when((b == 0) & (qi == 0))
    def _():
        _build_bias(tab_ref, bias_ref, corner_ref, 2 * hd)

    @pl.when(qi == 0)
    def _():
        _build_kv(k_ref, v_ref, ka_ref, vt_ref, seq, mask_v=False)

    lv = lam_ref[...]
    lam = (jnp.exp(jnp.sum(lv[0:1] * lv[1:2], axis=1, keepdims=True))
           - jnp.exp(jnp.sum(lv[2:3] * lv[3:4], axis=1, keepdims=True)) + lam_init)

    q_t = q_ref[0].astype(F32).T
    qa = _augment_queries(q_t * (HEAD_DIM ** -0.5 * LOG2E), [_block_penalties(None, _own_block(qi))] * 2)
    (a0, l0), (a1, l1) = _flash(qa, ka_ref, vt_ref, bias_ref, corner_ref, qi, (0, 0), bufs)
    att = a0 / l0 - lam * (a1 / l1)
    y = att * lax.rsqrt(jnp.mean(att * att, axis=0, keepdims=True) + RMS_EPS)
    o_ref[0] = ((y.T * sg_ref[...]) * (1.0 - lam_init)).astype(o_ref.dtype)


def _attention(kind, q_src, kv_src, rel_bias, extra, *, batch, seq, q_col, k_col, v_col, lam_init=None):
    tq = ATT_TQ
    nb = seq // MOBA_BLOCK
    n_hp = D_MODEL // LANES
    grid = (n_hp, batch, seq // tq)
    smem = pl.BlockSpec(memory_space=pltpu.SMEM)
    q_spec = pl.BlockSpec((1, tq, LANES), lambda h, b, i: (b, i, q_col + h))
    k_spec = pl.BlockSpec((1, seq, LANES), lambda h, b, i: (b, 0, k_col + h))
    v_spec = pl.BlockSpec((1, seq, LANES), lambda h, b, i: (b, 0, v_col + h))
    o_spec = pl.BlockSpec((1, tq, LANES), lambda h, b, i: (b, i, h))
    n_vt = 2 * nb if kind == "moba" else nb
    scratch = [pltpu.VMEM((2, seq, LANES), BF16),
               pltpu.VMEM((n_vt, LANES, MOBA_BLOCK), BF16),
               pltpu.VMEM((2, MOBA_BLOCK, MOBA_BLOCK), F32),
               pltpu.VMEM((2, REL_MAX_DISTANCE, REL_MAX_DISTANCE), F32),
               pltpu.VMEM((2, tq, tq), F32), pltpu.VMEM((2, tq, tq), F32),
               pltpu.VMEM((2, 1, tq), F32), pltpu.VMEM((2, 1, tq), F32), pltpu.VMEM((2, LANES, tq), F32)]
    out_shape = jax.ShapeDtypeStruct((batch, seq, D_MODEL), BF16)
    if kind == "moba":
        kmean = extra
        km_spec = pl.BlockSpec((nb, 1, LANES), lambda h, b, i: (b, 0, h))
        return pl.pallas_call(
            functools.partial(_moba_kernel, seq=seq),
            grid=grid,
            in_specs=[smem, q_spec, k_spec, v_spec, km_spec],
            out_specs=o_spec, out_shape=out_shape, scratch_shapes=scratch,
            compiler_params=_params(3), name="moba_attention",
        )(rel_bias, q_src, kv_src, kv_src, kmean)
    lam_rows, sub_gain = extra
    return pl.pallas_call(
        functools.partial(_diff_kernel, seq=seq, lam_init=lam_init),
        grid=grid,
        in_specs=[smem, _full((8, LANES)), q_spec, k_spec, v_spec, _full((1, LANES))],
        out_specs=o_spec, out_shape=out_shape, scratch_shapes=scratch,
        compiler_params=_params(3), name="diff_attention",
    )(rel_bias, lam_rows, q_src, kv_src, kv_src, sub_gain)


def _router_logits(xn, w_hi_ref, w_lo_ref, b_ref):
    x_hi = xn.astype(BF16)
    x_lo = (xn - x_hi.astype(F32)).astype(BF16)
    return (_dot(x_hi, w_hi_ref[...]) + (_dot(x_hi, w_lo_ref[...]) + _dot(x_lo, w_hi_ref[...]))) + b_ref[...]


def _route(logits, tri_ref, count_ref, counted):
    lane = lax.broadcasted_iota(jnp.int32, logits.shape, 1)
    live = lane < MOE_EXPERTS
    gl = jnp.where(live, logits, -jnp.inf)
    el = pltpu.roll(logits, LANES // 2, axis=1)
    gmax = jnp.max(gl, axis=1, keepdims=True)
    gsum = jnp.sum(jnp.exp(gl - gmax), axis=1, keepdims=True) / MOE_EXPERTS_PER_GROUP
    g_gate = 1.0 / gsum
    lane_f = lane.astype(F32)
    first_lane = lambda hit: jnp.min(jnp.where(hit, lane_f, float(LANES)), axis=1, keepdims=True)
    group_of = lambda lanes: jnp.floor(lanes * (1.0 / MOE_EXPERTS_PER_GROUP))
    in_group = group_of(lane_f) == group_of(first_lane(gl == gmax))
    em = jnp.where(in_group & live, el, -jnp.inf)
    m1 = jnp.max(em, axis=1, keepdims=True)
    i1 = first_lane(em == m1)
    em2 = jnp.where(lane_f == i1, -jnp.inf, em)
    m2 = jnp.max(em2, axis=1, keepdims=True)
    i2 = first_lane(em2 == m2)
    p2 = jnp.exp(m2 - m1)
    w1 = g_gate * (1.0 / (1.0 + p2))
    w2 = g_gate * (p2 / (1.0 + p2))

    is1, is2 = lane_f == i1, lane_f == i2
    member = jnp.where(is1 | is2, counted, 0.0)
    before = _dot(tri_ref[...], member.astype(BF16)) + count_ref[...]
    rank1 = jnp.sum(jnp.where(is1, before, 0.0), axis=1, keepdims=True)
    rank2 = jnp.sum(jnp.where(is2, before, 0.0), axis=1, keepdims=True)
    count_ref[...] = count_ref[...] + jnp.sum(member, axis=0, keepdims=True)
    fields = (i1, i2, w1, w2, rank1, rank2)
    route = jnp.zeros(lane.shape, F32)
    for k, col in enumerate(fields):
        route = jnp.where(lane == k, col, route)
    return route


def _attn_out_kernel(a_ref, h_ref, wo_ref, fg_ref, w_hi_ref, w_lo_ref, b_ref, tri_ref,
                     h1_ref, xn_ref, route_ref, counts_ref, count_acc, logit_buf):
    i = pl.program_id(0)

    @pl.when(i == 0)
    def _():
        count_acc[...] = jnp.zeros_like(count_acc)
        logit_buf[...] = jnp.zeros_like(logit_buf)

    counted = jnp.where(i > 0, 1.0, 0.0).astype(F32)

    def step(mine, other):
        h1 = h_ref[...] + _dot(a_ref[...], wo_ref[...])
        h1_ref[...] = h1
        xn = _rms(h1, fg_ref[...])
        xn_ref[...] = xn
        logit_buf[mine] = _router_logits(xn, w_hi_ref, w_lo_ref, b_ref)
        route_ref[...] = _route(logit_buf[other], tri_ref, count_acc, counted)
        counts_ref[...] = jnp.broadcast_to(count_acc[...], counts_ref.shape)

    @pl.when((i & 1) == 0)
    def _():
        step(0, 1)

    @pl.when((i & 1) == 1)
    def _():
        step(1, 0)


def _attn_out(att2, h2, wo_bf16, ffn_gain, w_hi, w_lo, b_x):
    t = h2.shape[0]
    tm = ROW_TILE
    n = t // tm
    row = lambda w: pl.BlockSpec((tm, w), lambda i: (jnp.minimum(i, n - 1), 0))
    tri = jnp.tril(jnp.ones((tm, tm), F32), -1).astype(BF16)
    return pl.pallas_call(
        _attn_out_kernel,
        grid=(n + 1,),
        in_specs=[row(D_MODEL), row(D_MODEL), _full((D_MODEL, D_MODEL)), _full((1, D_MODEL)),
                  _full((D_MODEL, LANES)), _full((D_MODEL, LANES)), _full((1, LANES)), _full((tm, tm))],
        out_specs=[row(D_MODEL), row(D_MODEL), pl.BlockSpec((tm, LANES), lambda i: (jnp.maximum(i - 1, 0), 0)),
                   _full((8, LANES))],
        out_shape=[jax.ShapeDtypeStruct((t, D_MODEL), F32), jax.ShapeDtypeStruct((t, D_MODEL), F32),
                   jax.ShapeDtypeStruct((t, LANES), F32), jax.ShapeDtypeStruct((8, LANES), F32)],
        scratch_shapes=[pltpu.VMEM((1, LANES), F32), pltpu.VMEM((2, tm, LANES), F32)],
        compiler_params=_params(1), name="attn_out_router",
    )(att2, h2, wo_bf16, ffn_gain.reshape(1, D_MODEL), w_hi, w_lo, b_x, tri)


def _scatter_kernel(last_row_ref, tiles_ref, used_ref, pos1_ref, pos2_ref, xn_ref, out_hbm, zeros_ref, sems):
    tm = pos1_ref.shape[2]
    row_sem, zero_sem = sems.at[0], sems.at[1]
    n_tiles = out_hbm.shape[0] // GROUP_TILE

    def zero_tile(row):
        return pltpu.make_async_copy(zeros_ref, out_hbm.at[pl.ds(pl.multiple_of(row, GROUP_TILE), GROUP_TILE)], zero_sem)

    def spare_tiles(fn):
        def body(w, carry):
            fn(zero_tile(w * GROUP_TILE))
            return carry

        lax.fori_loop(used_ref[0], n_tiles, body, 0)

    @pl.when(pl.program_id(0) == 0)
    def _():
        zeros_ref[...] = jnp.zeros_like(zeros_ref)
        for e in range(MOE_EXPERTS):
            @pl.when(tiles_ref[e] > 0)
            def _():
                zero_tile(last_row_ref[e]).start()

        for e in range(MOE_EXPERTS):
            @pl.when(tiles_ref[e] > 0)
            def _():
                zero_tile(last_row_ref[e]).wait()

        spare_tiles(lambda cp: cp.start())

    def issue(r, carry):
        src = xn_ref.at[pl.ds(r, 1)]
        pltpu.make_async_copy(src, out_hbm.at[pl.ds(pos1_ref[0, 0, r], 1)], row_sem).start(priority=0)
        pltpu.make_async_copy(src, out_hbm.at[pl.ds(pos2_ref[0, 0, r], 1)], row_sem).start(priority=1)
        return carry

    lax.fori_loop(0, tm, issue, 0, unroll=8)
    for _ in range(2):
        pltpu.make_async_copy(xn_ref, out_hbm.at[pl.ds(0, tm)], row_sem).wait()

    @pl.when(pl.program_id(0) == pl.num_programs(0) - 1)
    def _():
        spare_tiles(lambda cp: cp.wait())


def _scatter_rows(xn, pos1, pos2, last_row, tiles_per, n_used, n_rows):
    t = xn.shape[0]
    tm = ROW_TILE
    idx = lambda: pl.BlockSpec((1, 1, tm), lambda i, *_: (i, 0, 0), memory_space=pltpu.SMEM)
    return pl.pallas_call(
        _scatter_kernel,
        grid_spec=pltpu.PrefetchScalarGridSpec(
            num_scalar_prefetch=3, grid=(t // tm,),
            in_specs=[idx(), idx(), pl.BlockSpec((tm, D_MODEL), lambda i, *_: (i, 0))],
            out_specs=pl.BlockSpec(memory_space=pl.ANY),
            scratch_shapes=[pltpu.VMEM((GROUP_TILE, D_MODEL), F32), pltpu.SemaphoreType.DMA((2,))]),
        out_shape=jax.ShapeDtypeStruct((n_rows, D_MODEL), F32),
        compiler_params=_params(1), name="moe_scatter",
    )(last_row, tiles_per, n_used, pos1.reshape(t // tm, 1, tm), pos2.reshape(t // tm, 1, tm), xn)


def _moe_group_kernel(te_ref, x_ref, wg_ref, wu_ref, wd_ref, y_ref):
    del te_ref
    x = x_ref[...].astype(BF16)
    g = _dot(x, wg_ref[0])
    u = _dot(x, wu_ref[0])
    hh = (g * jax.nn.sigmoid(g)) * u
    y_ref[...] = _dot(hh.astype(BF16), wd_ref[0])


def _moe_group(x_sorted, tile_expert, wg, wu, wd):
    n_tiles = tile_expert.shape[0]
    tmg = GROUP_TILE
    wspec = lambda a, b: pl.BlockSpec((1, a, b), lambda w, te: (te[w], 0, 0))
    rows = lambda: pl.BlockSpec((tmg, D_MODEL), lambda w, te: (w, 0))
    return pl.pallas_call(
        _moe_group_kernel,
        grid_spec=pltpu.PrefetchScalarGridSpec(
            num_scalar_prefetch=1, grid=(n_tiles,),
            in_specs=[rows(), wspec(D_MODEL, MOE_D_FF), wspec(D_MODEL, MOE_D_FF), wspec(MOE_D_FF, D_MODEL)],
            out_specs=rows()),
        out_shape=jax.ShapeDtypeStruct((n_tiles * tmg, D_MODEL), F32),
        compiler_params=_params(1), name="moe_group_ffn",
    )(tile_expert, x_sorted, wg, wu, wd)


def _routing_tables(route, counts8, t):
    n_tiles = 2 * t // GROUP_TILE + MOE_EXPERTS
    counts = counts8[0, :MOE_EXPERTS].astype(jnp.int32)
    tiles_per = (counts + GROUP_TILE - 1) // GROUP_TILE
    tile_end = jnp.cumsum(tiles_per)
    offsets = (tile_end - tiles_per) * GROUP_TILE
    last_row = jnp.maximum(tile_end - 1, 0) * GROUP_TILE
    ids = route[:, 0:2].astype(jnp.int32)
    ranks = route[:, 4:6].astype(jnp.int32)
    expert_ids = jnp.arange(MOE_EXPERTS, dtype=jnp.int32)
    pos = jnp.sum(jnp.where(ids[:, :, None] == expert_ids, offsets, 0), axis=2) + ranks
    tile_ids = jnp.arange(n_tiles, dtype=jnp.int32)
    tile_expert = jnp.minimum(jnp.sum((tile_end[None, :] <= tile_ids[:, None]).astype(jnp.int32), axis=1),
                              MOE_EXPERTS - 1)
    return pos[:, 0], pos[:, 1], tile_expert, tile_end[-1:], last_row, tiles_per, n_tiles


def _gather_copy(pos_ref, y_hbm, ybuf, sem, slot, k, r):
    return pltpu.make_async_copy(y_hbm.at[pl.ds(pos_ref[0, 0, r], 1)], ybuf.at[slot, k, pl.ds(r, 1)], sem.at[slot])


def _wait_tile(y_hbm, ybuf, sem, slot):
    tm = ybuf.shape[2]
    for k in range(2):
        pltpu.make_async_copy(y_hbm.at[pl.ds(0, tm)], ybuf.at[slot, k], sem.at[slot]).wait()


def _moe_ple_update(refs, tail):
    (pos1_ref, pos2_ref, nxt1_ref, nxt2_ref, y_hbm, route_ref, h_ref, p_ref, pg_ref, wgate_ref, wproj_ref,
     ybuf, sem) = refs
    i = pl.program_id(0)
    slot = i & 1
    nslot = 1 - slot
    tm = ybuf.shape[2]

    @pl.when(i == 0)
    def _():
        def issue(r, carry):
            _gather_copy(pos1_ref, y_hbm, ybuf, sem, 0, 0, r).start(priority=0)
            _gather_copy(pos2_ref, y_hbm, ybuf, sem, 0, 1, r).start(priority=1)
            return carry

        lax.fori_loop(0, tm, issue, 0)

    _wait_tile(y_hbm, ybuf, sem, slot)
    rows_per = tm // GATHER_CHUNKS

    def prefetch(c):
        for r in range(c * rows_per, (c + 1) * rows_per):
            _gather_copy(nxt1_ref, y_hbm, ybuf, sem, nslot, 0, r).start(priority=0)
            _gather_copy(nxt2_ref, y_hbm, ybuf, sem, nslot, 1, r).start(priority=1)

    proj = _dot(p_ref[...].astype(BF16), wproj_ref[...])
    prefetch(0)
    route = route_ref[...]
    h = h_ref[...] + (route[:, 2:3] * ybuf[slot, 0] + route[:, 3:4] * ybuf[slot, 1])
    prefetch(1)
    gate = jax.nn.sigmoid(_dot(_rms(h, pg_ref[...]).astype(BF16), wgate_ref[...]))
    prefetch(2)
    tail(h + gate * proj, prefetch)

    @pl.when(i == pl.num_programs(0) - 1)
    def _():
        _wait_tile(y_hbm, ybuf, sem, nslot)


def _ple_mid_kernel(*refs):
    kvg_ref, wkv_ref, qg_ref, wq_ref, h3_ref, kv_ref, q_ref = refs[11:18]

    def tail(h3, prefetch):
        h3_ref[...] = h3
        prefetch(3)
        kn = _rms(h3, kvg_ref[...]).astype(BF16)
        for c in range(2):
            kv_ref[:, c * D_MODEL:(c + 1) * D_MODEL] = _dot(kn, wkv_ref[:, c * D_MODEL:(c + 1) * D_MODEL]).astype(BF16)
            prefetch(4 + c)
        q_ref[...] = _dot(_rms(h3, qg_ref[...]).astype(BF16), wq_ref[...]).astype(BF16)
        prefetch(6)
        prefetch(7)

    _moe_ple_update(refs[:11] + refs[18:], tail)


def _ple_last_kernel(*refs):
    fg_ref, o_ref = refs[11:13]

    def tail(h3, prefetch):
        o_ref[...] = _rms(h3, fg_ref[...])
        for c in range(3, GATHER_CHUNKS):
            prefetch(c)

    _moe_ple_update(refs[:11] + refs[13:], tail)


def _ple_call(body, name, moe_in, h1, p2, vecs_and_weights, in_tail, out_specs, out_shape):
    pos1, pos2, y_sorted, route = moe_in
    t = h1.shape[0]
    tm = ROW_TILE
    n = t // tm
    row = lambda w: pl.BlockSpec((tm, w), lambda i: (i, 0))
    idx = lambda: pl.BlockSpec((1, 1, tm), lambda i: (i, 0, 0), memory_space=pltpu.SMEM)
    nxt = lambda: pl.BlockSpec((1, 1, tm), lambda i: (jnp.minimum(i + 1, n - 1), 0, 0), memory_space=pltpu.SMEM)
    pos1, pos2 = pos1.reshape(n, 1, tm), pos2.reshape(n, 1, tm)
    return pl.pallas_call(
        body,
        grid=(n,),
        in_specs=[idx(), idx(), nxt(), nxt(), pl.BlockSpec(memory_space=pl.ANY), row(LANES), row(D_MODEL),
                  row(PLE_DIM)] + in_tail,
        out_specs=out_specs, out_shape=out_shape,
        scratch_shapes=[pltpu.VMEM((2, 2, tm, D_MODEL), F32), pltpu.SemaphoreType.DMA((2,))],
        compiler_params=_params(1), name=name,
    )(pos1, pos2, pos1, pos2, y_sorted, route, h1, p2, *vecs_and_weights)


def _ple_mid(moe_in, h1, p2, ple_gain, wgate, wproj, kv_gain, wkv, q_gain, wq):
    t = h1.shape[0]
    tm = ROW_TILE
    row = lambda w: pl.BlockSpec((tm, w), lambda i: (i, 0))
    vec = _full((1, D_MODEL))
    return _ple_call(
        _ple_mid_kernel, "moe_combine_ple_kv_q", moe_in, h1, p2,
        (ple_gain.reshape(1, -1), wgate, wproj, kv_gain.reshape(1, -1), wkv, q_gain.reshape(1, -1), wq),
        [vec, _full((D_MODEL, D_MODEL)), _full((PLE_DIM, D_MODEL)), vec, _full((D_MODEL, 2 * D_MODEL)), vec,
         _full((D_MODEL, D_MODEL))],
        [row(D_MODEL), row(2 * D_MODEL), row(D_MODEL)],
        [jax.ShapeDtypeStruct((t, D_MODEL), F32), jax.ShapeDtypeStruct((t, 2 * D_MODEL), BF16),
         jax.ShapeDtypeStruct((t, D_MODEL), BF16)])


def _ple_last(moe_in, h1, p2, ple_gain, wgate, wproj, final_gain):
    t = h1.shape[0]
    tm = ROW_TILE
    vec = _full((1, D_MODEL))
    return _ple_call(
        _ple_last_kernel, "moe_combine_ple_final_norm", moe_in, h1, p2,
        (ple_gain.reshape(1, -1), wgate, wproj, final_gain.reshape(1, -1)),
        [vec, _full((D_MODEL, D_MODEL)), _full((PLE_DIM, D_MODEL)), vec],
        pl.BlockSpec((tm, D_MODEL), lambda i: (i, 0)),
        jax.ShapeDtypeStruct((t, D_MODEL), F32))


def _router_operands(w_group, b_group, w_router, b_router):
    gap = LANES // 2 - MOE_EXPERTS
    w = jnp.pad(jnp.concatenate([jnp.repeat(w_group, MOE_EXPERTS_PER_GROUP, axis=1),
                                 jnp.zeros((D_MODEL, gap), F32), w_router], axis=1), ((0, 0), (0, gap)))
    b = jnp.pad(jnp.concatenate([jnp.repeat(b_group, MOE_EXPERTS_PER_GROUP), jnp.zeros((gap,), F32), b_router]),
                (0, gap)).reshape(1, LANES)
    w_hi = w.astype(BF16)
    w_lo = (w - w_hi.astype(F32)).astype(BF16)
    return w_hi, w_lo, b


def _moe_layer(att, h, wo, i, ffn_norm, w_group, b_group, w_router, b_router, w_gate, w_up, w_down):
    t = h.shape[0]
    h1, xn, route, counts8 = _attn_out(att, h, wo.astype(BF16), ffn_norm[i],
                                       *_router_operands(w_group[i], b_group[i], w_router[i], b_router[i]))
    pos1, pos2, tile_expert, n_used, last_row, tiles_per, n_tiles = _routing_tables(route, counts8, t)
    x_sorted = _scatter_rows(xn, pos1, pos2, last_row, tiles_per, n_used, n_tiles * GROUP_TILE)
    wg = w_gate[i].reshape(MOE_EXPERTS, D_MODEL, MOE_D_FF).astype(BF16)
    wu = w_up[i].reshape(MOE_EXPERTS, D_MODEL, MOE_D_FF).astype(BF16)
    wd = w_down[i].reshape(MOE_EXPERTS, MOE_D_FF, D_MODEL).astype(BF16)
    y_sorted = _moe_group(x_sorted, tile_expert, wg, wu, wd)
    return (pos1, pos2, y_sorted, route), h1


def kernel(x, p, rel_bias, attn_norm_a, w_qkv_a, w_o_a, kv_norm, w_kv, attn_norm_b, w_q_b, lambda_q1, lambda_k1,
           lambda_q2, lambda_k2, subln_b, w_o_b, ffn_norm, w_group, b_group, w_router, b_router, w_gate, w_up,
           w_down, ple_norm, w_ple_gate, w_ple_proj, final_norm):
    batch, seq, d = x.shape
    assert d == D_MODEL and seq % ATT_TQ == 0 and seq // MOBA_BLOCK <= 16
    t = batch * seq
    assert t % ROW_TILE == 0
    n_hp = D_MODEL // LANES
    moe = (ffn_norm, w_group, b_group, w_router, b_router, w_gate, w_up, w_down)

    h = x.reshape(t, d)
    qkv, kmean = _qkv_proj(h, attn_norm_a[0], w_qkv_a[0].astype(BF16))
    qkv3 = qkv.reshape(batch, seq, 3 * d)
    att = _attention("moba", qkv3, qkv3, rel_bias, kmean, batch=batch, seq=seq,
                     q_col=0, k_col=n_hp, v_col=2 * n_hp)
    moe_out, h = _moe_layer(att.reshape(t, d), h, w_o_a[0], 0, *moe)
    h, kv, q = _ple_mid(moe_out, h, p[0].reshape(t, PLE_DIM), ple_norm[0], w_ple_gate[0].astype(BF16),
                        w_ple_proj[0].astype(BF16), kv_norm, w_kv.astype(BF16), attn_norm_b[0],
                        w_q_b[0].astype(BF16))

    lam_init = 0.8 - 0.6 * math.exp(-0.3 * 1)
    lam_rows = jnp.pad(jnp.stack([lambda_q1[0], lambda_k1[0], lambda_q2[0], lambda_k2[0]]).astype(F32),
                       ((0, 4), (0, LANES - HEAD_DIM)))
    att = _attention("diff", q.reshape(batch, seq, d), kv.reshape(batch, seq, 2 * d), rel_bias,
                     (lam_rows, subln_b[0].reshape(1, LANES)), batch=batch, seq=seq,
                     q_col=0, k_col=0, v_col=n_hp, lam_init=lam_init)
    moe_out, h = _moe_layer(att.reshape(t, d), h, w_o_b[0], 1, *moe)
    out = _ple_last(moe_out, h, p[1].reshape(t, PLE_DIM), ple_norm[1], w_ple_gate[1].astype(BF16),
                    w_ple_proj[1].astype(BF16), final_norm)
    return out.reshape(batch, seq, d)
```

```python
import functools
import math

import jax
import jax.numpy as jnp
from jax import lax
from jax.experimental import pallas as pl
from jax.experimental.pallas import tpu as pltpu

F32 = jnp.float32
BF16 = jnp.bfloat16

D_MODEL = 1024
DEPTH = 2
N_A_LAYERS = DEPTH // 2
HEAD_DIM = 64
LANES = 128
MOBA_BLOCK = 256
ATT_TQ = 2 * MOBA_BLOCK
MOBA_TOP_K = 3
REL_BUCKETS = 32
REL_MAX_DISTANCE = 128
MOE_GROUPS = 4
MOE_EXPERTS_PER_GROUP = 8
MOE_EXPERTS = MOE_GROUPS * MOE_EXPERTS_PER_GROUP
MOE_D_FF = D_MODEL // 4
PLE_DIM = 256
RMS_EPS = 1e-6
NEG = -1e30
LOG2E = math.log2(math.e)

ROW_TILE = 512
GROUP_TILE = 512
SCATTER_TILE = 1024
GATHER_CHUNKS = 8
VMEM_LIMIT = 52 * 1024 * 1024


def _dot(a, b):
    return jnp.dot(a, b, preferred_element_type=F32)


def _rms(x, gain):
    y = x * lax.rsqrt(jnp.mean(x * x, axis=-1, keepdims=True) + RMS_EPS)
    return y * gain


def _params(n_axes):
    return pltpu.CompilerParams(dimension_semantics=("arbitrary",) * n_axes,
                                vmem_limit_bytes=VMEM_LIMIT)


def _full(shape):
    nd = len(shape)
    return pl.BlockSpec(shape, lambda *_: (0,) * nd)


def _qkv_kernel(x_ref, g_ref, w_ref, qkv_ref, kmean_ref):
    hn = _rms(x_ref[...], g_ref[...]).astype(BF16)
    for c in range(3):
        y = _dot(hn, w_ref[:, c * D_MODEL:(c + 1) * D_MODEL])
        qkv_ref[:, c * D_MODEL:(c + 1) * D_MODEL] = y.astype(BF16)
        if c == 1:
            nb = y.shape[0] // MOBA_BLOCK
            kmean_ref[...] = jnp.mean(y.reshape(nb, MOBA_BLOCK, D_MODEL), axis=1, keepdims=True)


def _qkv_proj(x2, gain, w_bf16):
    t = x2.shape[0]
    tm = ROW_TILE
    return pl.pallas_call(
        _qkv_kernel,
        grid=(t // tm,),
        in_specs=[pl.BlockSpec((tm, D_MODEL), lambda i: (i, 0)),
                  _full((1, D_MODEL)),
                  _full((D_MODEL, 3 * D_MODEL))],
        out_specs=[pl.BlockSpec((tm, 3 * D_MODEL), lambda i: (i, 0)),
                   pl.BlockSpec((tm // MOBA_BLOCK, 1, D_MODEL), lambda i: (i, 0, 0))],
        out_shape=[jax.ShapeDtypeStruct((t, 3 * D_MODEL), BF16),
                   jax.ShapeDtypeStruct((t // MOBA_BLOCK, 1, D_MODEL), F32)],
        compiler_params=_params(1),
        name="qkv_proj",
    )(x2, gain.reshape(1, D_MODEL), w_bf16)


def _rel_bucket(dist):
    n = jnp.maximum(dist, 0)
    max_exact = REL_BUCKETS // 2
    nf = jnp.maximum(n, max_exact).astype(F32)
    large = max_exact + (jnp.log(nf / max_exact) / math.log(REL_MAX_DISTANCE / max_exact)
                         * (REL_BUCKETS - max_exact)).astype(jnp.int32)
    large = jnp.minimum(large, REL_BUCKETS - 1)
    return jnp.where(n < max_exact, n, large)


def _shifted_bias(tab_ref, col, dist):
    bkt = _rel_bucket(dist)
    acc = jnp.zeros(dist.shape, F32)
    for i in range(REL_BUCKETS):
        acc = jnp.where(bkt == i, tab_ref[i, col], acc)
    return (acc - tab_ref[REL_BUCKETS - 1, col]) * LOG2E


def _build_bias(tab_ref, bias_ref, corner_ref, col0):
    tb = MOBA_BLOCK
    rows = 64
    for r0 in range(0, tb, rows):
        key = lax.broadcasted_iota(jnp.int32, (rows, tb), 0) + r0
        qry = lax.broadcasted_iota(jnp.int32, (rows, tb), 1)
        dist = qry - key
        for h in range(2):
            bias_ref[h, r0:r0 + rows, :] = jnp.where(dist >= 0, _shifted_bias(tab_ref, col0 + h, dist), NEG)
    key = lax.broadcasted_iota(jnp.int32, (REL_MAX_DISTANCE, REL_MAX_DISTANCE), 0) - REL_MAX_DISTANCE
    qry = lax.broadcasted_iota(jnp.int32, (REL_MAX_DISTANCE, REL_MAX_DISTANCE), 1)
    for h in range(2):
        corner_ref[h] = _shifted_bias(tab_ref, col0 + h, qry - key)


def _build_kv(k_ref, v_ref, ka_ref, vt_ref, seq, mask_v):
    tb = MOBA_BLOCK
    nb = seq // tb
    lane = lax.broadcasted_iota(jnp.int32, (tb, LANES), 1)
    first = lane < HEAD_DIM

    def body(j, _):
        r = pl.multiple_of(j * tb, tb)
        k = k_ref[0, pl.ds(r, tb), :].astype(F32)
        oh0 = (lane == HEAD_DIM + j).astype(F32)
        oh1 = (lane == j).astype(F32)
        ka_ref[0, pl.ds(r, tb), :] = jnp.where(first, k, oh0).astype(BF16)
        ka_ref[1, pl.ds(r, tb), :] = jnp.where(first, oh1, k).astype(BF16)
        v_t = v_ref[0, pl.ds(r, tb), :].astype(F32).T
        if mask_v:
            none = jnp.zeros((HEAD_DIM, tb), F32)
            vt_ref[j] = jnp.concatenate([v_t[0:HEAD_DIM], none], axis=0).astype(BF16)
            vt_ref[nb + j] = jnp.concatenate([none, v_t[HEAD_DIM:]], axis=0).astype(BF16)
        else:
            vt_ref[j] = v_t.astype(BF16)
        return 0

    lax.fori_loop(0, nb, body, 0)


def _block_penalties(gate, own):
    n = lax.broadcasted_iota(jnp.int32, (16, own.shape[1]), 0)
    pen = jnp.where(n == own, 0.0, NEG)
    if gate is None:
        return jnp.where(n < own, 0.0, pen)
    g = jnp.where(n < own, gate, -jnp.inf)
    for _ in range(MOBA_TOP_K):
        mx = jnp.max(g, axis=0, keepdims=True)
        idx = jnp.min(jnp.where(g == mx, n, 1 << 20), axis=0, keepdims=True)
        pick = (n == idx) & (mx > -jnp.inf)
        pen = jnp.where(pick, 0.0, pen)
        g = jnp.where(pick, -jnp.inf, g)
    return pen


def _own_block(qi):
    col = lax.broadcasted_iota(jnp.int32, (1, ATT_TQ), 1)
    return qi * (ATT_TQ // MOBA_BLOCK) + col // MOBA_BLOCK


def _augment_queries(qs_t, pens):
    nq = qs_t.shape[1]
    out = []
    for h in range(2):
        tail = [pens[h], jnp.zeros((HEAD_DIM - 16, nq), F32)]
        parts = [qs_t[0:HEAD_DIM]] + tail if h == 0 else tail + [qs_t[HEAD_DIM:]]
        out.append(jnp.concatenate(parts, axis=0).astype(BF16))
    return out


def _flash(qa, ka_ref, vt_ref, bias_ref, corner_ref, qi, v_base, bufs):
    tb = MOBA_BLOCK
    near = REL_MAX_DISTANCE
    s_a, s_b, m_ref, l_ref, acc_ref = bufs

    def scores_into(buf, j, own=False):
        r = pl.multiple_of((qi - j) * ATT_TQ, ATT_TQ)
        prev = jnp.where(j == 1, 1.0, 0.0).astype(F32)
        for h in range(2):
            s = _dot(ka_ref[h, pl.ds(r, ATT_TQ), :], qa[h])
            buf[h] = s
            if own:
                for lo in (0, tb):
                    buf[h, lo:lo + tb, lo:lo + tb] = s[lo:lo + tb, lo:lo + tb] + bias_ref[h]
                buf[h, tb - near:tb, tb:tb + near] = s[tb - near:tb, tb:tb + near] + corner_ref[h]
            else:
                buf[h, ATT_TQ - near:ATT_TQ, 0:near] = s[ATT_TQ - near:, 0:near] + prev * corner_ref[h]

    def update(buf, j):
        blk0 = (qi - j) * (ATT_TQ // tb)
        for h in range(2):
            s = buf[h]
            m = m_ref[h]
            m_new = jnp.maximum(m, jnp.max(s, axis=0, keepdims=True))
            alpha = jnp.exp2(m - m_new)
            p = jnp.exp2(s - m_new)
            pb = p.astype(BF16)
            pv = _dot(vt_ref[v_base[h] + blk0], pb[0:tb]) + _dot(vt_ref[v_base[h] + blk0 + 1], pb[tb:])
            m_ref[h] = m_new
            l_ref[h] = alpha * l_ref[h] + jnp.sum(p, axis=0, keepdims=True)
            acc_ref[h] = alpha * acc_ref[h] + pv

    m_ref[...] = jnp.full(m_ref.shape, 3.0 * NEG, F32)
    l_ref[...] = jnp.zeros(l_ref.shape, F32)
    acc_ref[...] = jnp.zeros(acc_ref.shape, F32)
    scores_into(s_a, 0, own=True)

    def body(t, carry):
        scores_into(s_b, 2 * t + 1)
        update(s_a, 2 * t)
        scores_into(s_a, 2 * t + 2)
        update(s_b, 2 * t + 1)
        return carry

    lax.fori_loop(0, lax.shift_right_logical(qi, 1), body, 0)
    odd = (qi & 1) == 1

    @pl.when(odd)
    def _():
        scores_into(s_b, qi)
        update(s_a, qi - 1)
        update(s_b, qi)

    @pl.when(jnp.logical_not(odd))
    def _():
        update(s_a, qi)

    return [(acc_ref[h], l_ref[h]) for h in range(2)]


def _moba_kernel(tab_ref, q_ref, k_ref, v_ref, km_ref, o_ref, ka_ref, vt_ref, bias_ref, corner_ref, *bufs, seq):
    hp, b, qi = pl.program_id(0), pl.program_id(1), pl.program_id(2)
    nb = seq // MOBA_BLOCK

    @pl.when((b == 0) & (qi == 0))
    def _():
        _build_bias(tab_ref, bias_ref, corner_ref, 2 * hp)

    @pl.when(qi == 0)
    def _():
        _build_kv(k_ref, v_ref, ka_ref, vt_ref, seq, mask_v=True)

    q_t = q_ref[0].astype(F32).T
    km = km_ref[:, 0, :]
    lane_k = lax.broadcasted_iota(jnp.int32, (nb, LANES), 1)
    km0 = jnp.where(lane_k < HEAD_DIM, km, 0.0)
    km1 = jnp.where(lane_k < HEAD_DIM, 0.0, km)
    pad = jnp.zeros((HEAD_DIM - nb, LANES), F32)
    kmx = jnp.concatenate([km1, pad, km0, pad], axis=0)
    kmx_hi = kmx.astype(BF16)
    kmx_lo = (kmx - kmx_hi.astype(F32)).astype(BF16)
    q_bf = q_t.astype(BF16)
    gate = _dot(kmx_hi, q_bf) + _dot(kmx_lo, q_bf)
    pens = [_block_penalties(gate[base:base + 16], _own_block(qi)) for base in (HEAD_DIM, 0)]

    qa = _augment_queries(q_t * (HEAD_DIM ** -0.5 * LOG2E), pens)
    (a0, l0), (a1, l1) = _flash(qa, ka_ref, vt_ref, bias_ref, corner_ref, qi, (0, nb), bufs)
    o_ref[0] = (a0 / l0 + a1 / l1).T.astype(o_ref.dtype)


def _diff_kernel(tab_ref, lam_ref, q_ref, k_ref, v_ref, sg_ref, o_ref, ka_ref, vt_ref, bias_ref, corner_ref, *bufs,
                 seq, lam_init):
    hd, b, qi = pl.program_id(0), pl.program_id(1), pl.program_id(2)

    @pl.when((b == 0) & (qi == 0))
    def _():
        _build_bias(tab_ref, bias_ref, corner_ref, 2 * hd)

    @pl.when(qi == 0)
    def _():
        _build_kv(k_ref, v_ref, ka_ref, vt_ref, seq, mask_v=False)

    lv = lam_ref[...]
    lam = (jnp.exp(jnp.sum(lv[0:1] * lv[1:2], axis=1, keepdims=True))
           - jnp.exp(jnp.sum(lv[2:3] * lv[3:4], axis=1, keepdims=True)) + lam_init)

    q_t = q_ref[0].astype(F32).T
    qa = _augment_queries(q_t * (HEAD_DIM ** -0.5 * LOG2E), [_block_penalties(None, _own_block(qi))] * 2)
    (a0, l0), (a1, l1) = _flash(qa, ka_ref, vt_ref, bias_ref, corner_ref, qi, (0, 0), bufs)
    att = a0 / l0 - lam * (a1 / l1)
    y = att * lax.rsqrt(jnp.mean(att * att, axis=0, keepdims=True) + RMS_EPS)
    o_ref[0] = ((y.T * sg_ref[...]) * (1.0 - lam_init)).astype(o_ref.dtype)


def _attention(kind, q_src, kv_src, rel_bias, extra, *, batch, seq, q_col, k_col, v_col, lam_init=None):
    tq = ATT_TQ
    nb = seq // MOBA_BLOCK
    n_hp = D_MODEL // LANES
    grid = (n_hp, batch, seq // tq)
    smem = pl.BlockSpec(memory_space=pltpu.SMEM)
    q_spec = pl.BlockSpec((1, tq, LANES), lambda h, b, i: (b, i, q_col + h))
    k_spec = pl.BlockSpec((1, seq, LANES), lambda h, b, i: (b, 0, k_col + h))
    v_spec = pl.BlockSpec((1, seq, LANES), lambda h, b, i: (b, 0, v_col + h))
    o_spec = pl.BlockSpec((1, tq, LANES), lambda h, b, i: (b, i, h))
    n_vt = 2 * nb if kind == "moba" else nb
    scratch = [pltpu.VMEM((2, seq, LANES), BF16),
               pltpu.VMEM((n_vt, LANES, MOBA_BLOCK), BF16),
               pltpu.VMEM((2, MOBA_BLOCK, MOBA_BLOCK), F32),
               pltpu.VMEM((2, REL_MAX_DISTANCE, REL_MAX_DISTANCE), F32),
               pltpu.VMEM((2, tq, tq), F32), pltpu.VMEM((2, tq, tq), F32),
               pltpu.VMEM((2, 1, tq), F32), pltpu.VMEM((2, 1, tq), F32), pltpu.VMEM((2, LANES, tq), F32)]
    out_shape = jax.ShapeDtypeStruct((batch, seq, D_MODEL), BF16)
    if kind == "moba":
        kmean = extra
        km_spec = pl.BlockSpec((nb, 1, LANES), lambda h, b, i: (b, 0, h))
        return pl.pallas_call(
            functools.partial(_moba_kernel, seq=seq),
            grid=grid,
            in_specs=[smem, q_spec, k_spec, v_spec, km_spec],
            out_specs=o_spec, out_shape=out_shape, scratch_shapes=scratch,
            compiler_params=_params(3), name="moba_attention",
        )(rel_bias, q_src, kv_src, kv_src, kmean)
    lam_rows, sub_gain = extra
    return pl.pallas_call(
        functools.partial(_diff_kernel, seq=seq, lam_init=lam_init),
        grid=grid,
        in_specs=[smem, _full((8, LANES)), q_spec, k_spec, v_spec, _full((1, LANES))],
        out_specs=o_spec, out_shape=out_shape, scratch_shapes=scratch,
        compiler_params=_params(3), name="diff_attention",
    )(rel_bias, lam_rows, q_src, kv_src, kv_src, sub_gain)


def _router_logits(xn, w_hi_ref, w_lo_ref, b_ref):
    x_hi = xn.astype(BF16)
    x_lo = (xn - x_hi.astype(F32)).astype(BF16)
    return (_dot(x_hi, w_hi_ref[...]) + (_dot(x_hi, w_lo_ref[...]) + _dot(x_lo, w_hi_ref[...]))) + b_ref[...]


def _route(logits, tri_ref, count_ref, counted):
    lane = lax.broadcasted_iota(jnp.int32, logits.shape, 1)
    live = lane < MOE_EXPERTS
    gl = jnp.where(live, logits, -jnp.inf)
    el = pltpu.roll(logits, LANES // 2, axis=1)
    gmax = jnp.max(gl, axis=1, keepdims=True)
    gsum = jnp.sum(jnp.exp(gl - gmax), axis=1, keepdims=True) / MOE_EXPERTS_PER_GROUP
    g_gate = 1.0 / gsum
    lane_f = lane.astype(F32)
    first_lane = lambda hit: jnp.min(jnp.where(hit, lane_f, float(LANES)), axis=1, keepdims=True)
    group_of = lambda lanes: jnp.floor(lanes * (1.0 / MOE_EXPERTS_PER_GROUP))
    in_group = group_of(lane_f) == group_of(first_lane(gl == gmax))
    em = jnp.where(in_group & live, el, -jnp.inf)
    m1 = jnp.max(em, axis=1, keepdims=True)
    i1 = first_lane(em == m1)
    em2 = jnp.where(lane_f == i1, -jnp.inf, em)
    m2 = jnp.max(em2, axis=1, keepdims=True)
    i2 = first_lane(em2 == m2)
    p2 = jnp.exp(m2 - m1)
    w1 = g_gate * (1.0 / (1.0 + p2))
    w2 = g_gate * (p2 / (1.0 + p2))

    is1, is2 = lane_f == i1, lane_f == i2
    member = jnp.where(is1 | is2, counted, 0.0)
    before = _dot(tri_ref[...], member.astype(BF16)) + count_ref[...]
    rank1 = jnp.sum(jnp.where(is1, before, 0.0), axis=1, keepdims=True)
    rank2 = jnp.sum(jnp.where(is2, before, 0.0), axis=1, keepdims=True)
    count_ref[...] = count_ref[...] + jnp.sum(member, axis=0, keepdims=True)
    fields = (i1, i2, w1, w2, rank1, rank2)
    route = jnp.zeros(lane.shape, F32)
    for k, col in enumerate(fields):
        route = jnp.where(lane == k, col, route)
    return route


def _attn_out_kernel(a_ref, h_ref, wo_ref, fg_ref, w_hi_ref, w_lo_ref, b_ref, tri_ref,
                     h1_ref, xn_ref, route_ref, counts_ref, count_acc, logit_buf):
    i = pl.program_id(0)

    @pl.when(i == 0)
    def _():
        count_acc[...] = jnp.zeros_like(count_acc)
        logit_buf[...] = jnp.zeros_like(logit_buf)

    counted = jnp.where(i > 0, 1.0, 0.0).astype(F32)

    def step(mine, other):
        h1 = h_ref[...] + _dot(a_ref[...], wo_ref[...])
        h1_ref[...] = h1
        xn = _rms(h1, fg_ref[...])
        xn_ref[...] = xn
        logit_buf[mine] = _router_logits(xn, w_hi_ref, w_lo_ref, b_ref)
        route_ref[...] = _route(logit_buf[other], tri_ref, count_acc, counted)
        counts_ref[...] = jnp.broadcast_to(count_acc[...], counts_ref.shape)

    @pl.when((i & 1) == 0)
    def _():
        step(0, 1)

    @pl.when((i & 1) == 1)
    def _():
        step(1, 0)


def _attn_out(att2, h2, wo_bf16, ffn_gain, w_hi, w_lo, b_x):
    t = h2.shape[0]
    tm = ROW_TILE
    n = t // tm
    row = lambda w: pl.BlockSpec((tm, w), lambda i: (jnp.minimum(i, n - 1), 0))
    tri = jnp.tril(jnp.ones((tm, tm), F32), -1).astype(BF16)
    return pl.pallas_call(
        _attn_out_kernel,
        grid=(n + 1,),
        in_specs=[row(D_MODEL), row(D_MODEL), _full((D_MODEL, D_MODEL)), _full((1, D_MODEL)),
                  _full((D_MODEL, LANES)), _full((D_MODEL, LANES)), _full((1, LANES)), _full((tm, tm))],
        out_specs=[row(D_MODEL), row(D_MODEL), pl.BlockSpec((tm, LANES), lambda i: (jnp.maximum(i - 1, 0), 0)),
                   _full((8, LANES))],
        out_shape=[jax.ShapeDtypeStruct((t, D_MODEL), F32), jax.ShapeDtypeStruct((t, D_MODEL), F32),
                   jax.ShapeDtypeStruct((t, LANES), F32), jax.ShapeDtypeStruct((8, LANES), F32)],
        scratch_shapes=[pltpu.VMEM((1, LANES), F32), pltpu.VMEM((2, tm, LANES), F32)],
        compiler_params=_params(1), name="attn_out_router",
    )(att2, h2, wo_bf16, ffn_gain.reshape(1, D_MODEL), w_hi, w_lo, b_x, tri)


def _scatter_kernel(last_row_ref, tiles_ref, used_ref, pos1_ref, pos2_ref, xn_ref, out_hbm, zeros_ref, sems):
    tm = pos1_ref.shape[2]
    row_sem, zero_sem = sems.at[0], sems.at[1]
    n_tiles = out_hbm.shape[0] // GROUP_TILE

    def zero_tile(row):
        return pltpu.make_async_copy(zeros_ref, out_hbm.at[pl.ds(pl.multiple_of(row, GROUP_TILE), GROUP_TILE)], zero_sem)

    def spare_tiles(fn):
        def body(w, carry):
            fn(zero_tile(w * GROUP_TILE))
            return carry

        lax.fori_loop(used_ref[0], n_tiles, body, 0)

    @pl.when(pl.program_id(0) == 0)
    def _():
        zeros_ref[...] = jnp.zeros_like(zeros_ref)
        for e in range(MOE_EXPERTS):
            @pl.when(tiles_ref[e] > 0)
            def _():
                zero_tile(last_row_ref[e]).start()

        for e in range(MOE_EXPERTS):
            @pl.when(tiles_ref[e] > 0)
            def _():
                zero_tile(last_row_ref[e]).wait()

        spare_tiles(lambda cp: cp.start())

    def issue(r, carry):
        src = xn_ref.at[pl.ds(r, 1)]
        pltpu.make_async_copy(src, out_hbm.at[pl.ds(pos1_ref[0, 0, r], 1)], row_sem).start(priority=0)
        pltpu.make_async_copy(src, out_hbm.at[pl.ds(pos2_ref[0, 0, r], 1)], row_sem).start(priority=1)
        return carry

    lax.fori_loop(0, tm, issue, 0, unroll=8)
    for _ in range(2):
        pltpu.make_async_copy(xn_ref, out_hbm.at[pl.ds(0, tm)], row_sem).wait()

    @pl.when(pl.program_id(0) == pl.num_programs(0) - 1)
    def _():
        spare_tiles(lambda cp: cp.wait())


def _scatter_rows(xn, pos1, pos2, last_row, tiles_per, n_used, n_rows):
    t = xn.shape[0]
    tm = SCATTER_TILE
    idx = lambda: pl.BlockSpec((1, 1, tm), lambda i, *_: (i, 0, 0), memory_space=pltpu.SMEM)
    return pl.pallas_call(
        _scatter_kernel,
        grid_spec=pltpu.PrefetchScalarGridSpec(
            num_scalar_prefetch=3, grid=(t // tm,),
            in_specs=[idx(), idx(), pl.BlockSpec((tm, D_MODEL), lambda i, *_: (i, 0))],
            out_specs=pl.BlockSpec(memory_space=pl.ANY),
            scratch_shapes=[pltpu.VMEM((GROUP_TILE, D_MODEL), F32), pltpu.SemaphoreType.DMA((2,))]),
        out_shape=jax.ShapeDtypeStruct((n_rows, D_MODEL), F32),
        compiler_params=_params(1), name="moe_scatter",
    )(last_row, tiles_per, n_used, pos1.reshape(t // tm, 1, tm), pos2.reshape(t // tm, 1, tm), xn)


def _moe_group_kernel(te_ref, x_ref, wg_ref, wu_ref, wd_ref, y_ref):
    del te_ref
    x = x_ref[...].astype(BF16)
    g = _dot(x, wg_ref[0])
    u = _dot(x, wu_ref[0])
    hh = (g * jax.nn.sigmoid(g)) * u
    y_ref[...] = _dot(hh.astype(BF16), wd_ref[0])


def _moe_group(x_sorted, tile_expert, wg, wu, wd):
    n_tiles = tile_expert.shape[0]
    tmg = GROUP_TILE
    wspec = lambda a, b: pl.BlockSpec((1, a, b), lambda w, te: (te[w], 0, 0))
    rows = lambda: pl.BlockSpec((tmg, D_MODEL), lambda w, te: (w, 0))
    return pl.pallas_call(
        _moe_group_kernel,
        grid_spec=pltpu.PrefetchScalarGridSpec(
            num_scalar_prefetch=1, grid=(n_tiles,),
            in_specs=[rows(), wspec(D_MODEL, MOE_D_FF), wspec(D_MODEL, MOE_D_FF), wspec(MOE_D_FF, D_MODEL)],
            out_specs=rows()),
        out_shape=jax.ShapeDtypeStruct((n_tiles * tmg, D_MODEL), F32),
        compiler_params=_params(1), name="moe_group_ffn",
    )(tile_expert, x_sorted, wg, wu, wd)


def _routing_tables(route, counts8, t):
    n_tiles = 2 * t // GROUP_TILE + MOE_EXPERTS
    counts = counts8[0, :MOE_EXPERTS].astype(jnp.int32)
    tiles_per = (counts + GROUP_TILE - 1) // GROUP_TILE
    tile_end = jnp.cumsum(tiles_per)
    offsets = (tile_end - tiles_per) * GROUP_TILE
    last_row = jnp.maximum(tile_end - 1, 0) * GROUP_TILE
    ids = route[:, 0:2].astype(jnp.int32)
    ranks = route[:, 4:6].astype(jnp.int32)
    expert_ids = jnp.arange(MOE_EXPERTS, dtype=jnp.int32)
    pos = jnp.sum(jnp.where(ids[:, :, None] == expert_ids, offsets, 0), axis=2) + ranks
    tile_ids = jnp.arange(n_tiles, dtype=jnp.int32)
    tile_expert = jnp.minimum(jnp.sum((tile_end[None, :] <= tile_ids[:, None]).astype(jnp.int32), axis=1),
                              MOE_EXPERTS - 1)
    return pos[:, 0], pos[:, 1], tile_expert, tile_end[-1:], last_row, tiles_per, n_tiles


def _gather_copy(pos_ref, y_hbm, ybuf, sem, slot, k, r):
    return pltpu.make_async_copy(y_hbm.at[pl.ds(pos_ref[0, 0, r], 1)], ybuf.at[slot, k, pl.ds(r, 1)], sem.at[slot])


def _wait_tile(y_hbm, ybuf, sem, slot):
    tm = ybuf.shape[2]
    for k in range(2):
        pltpu.make_async_copy(y_hbm.at[pl.ds(0, tm)], ybuf.at[slot, k], sem.at[slot]).wait()


def _moe_ple_update(refs, tail):
    (pos1_ref, pos2_ref, nxt1_ref, nxt2_ref, y_hbm, route_ref, h_ref, p_ref, pg_ref, wgate_ref, wproj_ref,
     ybuf, sem) = refs
    i = pl.program_id(0)
    slot = i & 1
    nslot = 1 - slot
    tm = ybuf.shape[2]

    @pl.when(i == 0)
    def _():
        def issue(r, carry):
            _gather_copy(pos1_ref, y_hbm, ybuf, sem, 0, 0, r).start(priority=0)
            _gather_copy(pos2_ref, y_hbm, ybuf, sem, 0, 1, r).start(priority=1)
            return carry

        lax.fori_loop(0, tm, issue, 0)

    _wait_tile(y_hbm, ybuf, sem, slot)
    rows_per = tm // GATHER_CHUNKS

    def prefetch(c):
        for r in range(c * rows_per, (c + 1) * rows_per):
            _gather_copy(nxt1_ref, y_hbm, ybuf, sem, nslot, 0, r).start(priority=0)
            _gather_copy(nxt2_ref, y_hbm, ybuf, sem, nslot, 1, r).start(priority=1)

    proj = _dot(p_ref[...].astype(BF16), wproj_ref[...])
    prefetch(0)
    route = route_ref[...]
    h = h_ref[...] + (route[:, 2:3] * ybuf[slot, 0] + route[:, 3:4] * ybuf[slot, 1])
    prefetch(1)
    gate = jax.nn.sigmoid(_dot(_rms(h, pg_ref[...]).astype(BF16), wgate_ref[...]))
    prefetch(2)
    tail(h + gate * proj, prefetch)

    @pl.when(i == pl.num_programs(0) - 1)
    def _():
        _wait_tile(y_hbm, ybuf, sem, nslot)


def _ple_mid_kernel(*refs):
    kvg_ref, wkv_ref, qg_ref, wq_ref, h3_ref, kv_ref, q_ref = refs[11:18]

    def tail(h3, prefetch):
        h3_ref[...] = h3
        prefetch(3)
        kn = _rms(h3, kvg_ref[...]).astype(BF16)
        for c in range(2):
            kv_ref[:, c * D_MODEL:(c + 1) * D_MODEL] = _dot(kn, wkv_ref[:, c * D_MODEL:(c + 1) * D_MODEL]).astype(BF16)
            prefetch(4 + c)
        q_ref[...] = _dot(_rms(h3, qg_ref[...]).astype(BF16), wq_ref[...]).astype(BF16)
        prefetch(6)
        prefetch(7)

    _moe_ple_update(refs[:11] + refs[18:], tail)


def _ple_last_kernel(*refs):
    fg_ref, o_ref = refs[11:13]

    def tail(h3, prefetch):
        o_ref[...] = _rms(h3, fg_ref[...])
        for c in range(3, GATHER_CHUNKS):
            prefetch(c)

    _moe_ple_update(refs[:11] + refs[13:], tail)


def _ple_call(body, name, moe_in, h1, p2, vecs_and_weights, in_tail, out_specs, out_shape):
    pos1, pos2, y_sorted, route = moe_in
    t = h1.shape[0]
    tm = ROW_TILE
    n = t // tm
    row = lambda w: pl.BlockSpec((tm, w), lambda i: (i, 0))
    idx = lambda: pl.BlockSpec((1, 1, tm), lambda i: (i, 0, 0), memory_space=pltpu.SMEM)
    nxt = lambda: pl.BlockSpec((1, 1, tm), lambda i: (jnp.minimum(i + 1, n - 1), 0, 0), memory_space=pltpu.SMEM)
    pos1, pos2 = pos1.reshape(n, 1, tm), pos2.reshape(n, 1, tm)
    return pl.pallas_call(
        body,
        grid=(n,),
        in_specs=[idx(), idx(), nxt(), nxt(), pl.BlockSpec(memory_space=pl.ANY), row(LANES), row(D_MODEL),
                  row(PLE_DIM)] + in_tail,
        out_specs=out_specs, out_shape=out_shape,
        scratch_shapes=[pltpu.VMEM((2, 2, tm, D_MODEL), F32), pltpu.SemaphoreType.DMA((2,))],
        compiler_params=_params(1), name=name,
    )(pos1, pos2, pos1, pos2, y_sorted, route, h1, p2, *vecs_and_weights)


def _ple_mid(moe_in, h1, p2, ple_gain, wgate, wproj, kv_gain, wkv, q_gain, wq):
    t = h1.shape[0]
    tm = ROW_TILE
    row = lambda w: pl.BlockSpec((tm, w), lambda i: (i, 0))
    vec = _full((1, D_MODEL))
    return _ple_call(
        _ple_mid_kernel, "moe_combine_ple_kv_q", moe_in, h1, p2,
        (ple_gain.reshape(1, -1), wgate, wproj, kv_gain.reshape(1, -1), wkv, q_gain.reshape(1, -1), wq),
        [vec, _full((D_MODEL, D_MODEL)), _full((PLE_DIM, D_MODEL)), vec, _full((D_MODEL, 2 * D_MODEL)), vec,
         _full((D_MODEL, D_MODEL))],
        [row(D_MODEL), row(2 * D_MODEL), row(D_MODEL)],
        [jax.ShapeDtypeStruct((t, D_MODEL), F32), jax.ShapeDtypeStruct((t, 2 * D_MODEL), BF16),
         jax.ShapeDtypeStruct((t, D_MODEL), BF16)])


def _ple_last(moe_in, h1, p2, ple_gain, wgate, wproj, final_gain):
    t = h1.shape[0]
    tm = ROW_TILE
    vec = _full((1, D_MODEL))
    return _ple_call(
        _ple_last_kernel, "moe_combine_ple_final_norm", moe_in, h1, p2,
        (ple_gain.reshape(1, -1), wgate, wproj, final_gain.reshape(1, -1)),
        [vec, _full((D_MODEL, D_MODEL)), _full((PLE_DIM, D_MODEL)), vec],
        pl.BlockSpec((tm, D_MODEL), lambda i: (i, 0)),
        jax.ShapeDtypeStruct((t, D_MODEL), F32))


def _router_operands(w_group, b_group, w_router, b_router):
    gap = LANES // 2 - MOE_EXPERTS
    w = jnp.pad(jnp.concatenate([jnp.repeat(w_group, MOE_EXPERTS_PER_GROUP, axis=1),
                                 jnp.zeros((D_MODEL, gap), F32), w_router], axis=1), ((0, 0), (0, gap)))
    b = jnp.pad(jnp.concatenate([jnp.repeat(b_group, MOE_EXPERTS_PER_GROUP), jnp.zeros((gap,), F32), b_router]),
                (0, gap)).reshape(1, LANES)
    w_hi = w.astype(BF16)
    w_lo = (w - w_hi.astype(F32)).astype(BF16)
    return w_hi, w_lo, b


def _moe_layer(att, h, wo, i, ffn_norm, w_group, b_group, w_router, b_router, w_gate, w_up, w_down):
    t = h.shape[0]
    h1, xn, route, counts8 = _attn_out(att, h, wo.astype(BF16), ffn_norm[i],
                                       *_router_operands(w_group[i], b_group[i], w_router[i], b_router[i]))
    pos1, pos2, tile_expert, n_used, last_row, tiles_per, n_tiles = _routing_tables(route, counts8, t)
    x_sorted = _scatter_rows(xn, pos1, pos2, last_row, tiles_per, n_used, n_tiles * GROUP_TILE)
    wg = w_gate[i].reshape(MOE_EXPERTS, D_MODEL, MOE_D_FF).astype(BF16)
    wu = w_up[i].reshape(MOE_EXPERTS, D_MODEL, MOE_D_FF).astype(BF16)
    wd = w_down[i].reshape(MOE_EXPERTS, MOE_D_FF, D_MODEL).astype(BF16)
    y_sorted = _moe_group(x_sorted, tile_expert, wg, wu, wd)
    return (pos1, pos2, y_sorted, route), h1


def kernel(x, p, rel_bias, attn_norm_a, w_qkv_a, w_o_a, kv_norm, w_kv, attn_norm_b, w_q_b, lambda_q1, lambda_k1,
           lambda_q2, lambda_k2, subln_b, w_o_b, ffn_norm, w_group, b_group, w_router, b_router, w_gate, w_up,
           w_down, ple_norm, w_ple_gate, w_ple_proj, final_norm):
    batch, seq, d = x.shape
    assert d == D_MODEL and seq % ATT_TQ == 0 and seq // MOBA_BLOCK <= 16
    t = batch * seq
    assert t % ROW_TILE == 0 and t % SCATTER_TILE == 0
    n_hp = D_MODEL // LANES
    moe = (ffn_norm, w_group, b_group, w_router, b_router, w_gate, w_up, w_down)

    h = x.reshape(t, d)
    qkv, kmean = _qkv_proj(h, attn_norm_a[0], w_qkv_a[0].astype(BF16))
    qkv3 = qkv.reshape(batch, seq, 3 * d)
    att = _attention("moba", qkv3, qkv3, rel_bias, kmean, batch=batch, seq=seq,
                     q_col=0, k_col=n_hp, v_col=2 * n_hp)
    moe_out, h = _moe_layer(att.reshape(t, d), h, w_o_a[0], 0, *moe)
    h, kv, q = _ple_mid(moe_out, h, p[0].reshape(t, PLE_DIM), ple_norm[0], w_ple_gate[0].astype(BF16),
                        w_ple_proj[0].astype(BF16), kv_norm, w_kv.astype(BF16), attn_norm_b[0],
                        w_q_b[0].astype(BF16))

    lam_init = 0.8 - 0.6 * math.exp(-0.3 * 1)
    lam_rows = jnp.pad(jnp.stack([lambda_q1[0], lambda_k1[0], lambda_q2[0], lambda_k2[0]]).astype(F32),
                       ((0, 4), (0, LANES - HEAD_DIM)))
    att = _attention("diff", q.reshape(batch, seq, d), kv.reshape(batch, seq, 2 * d), rel_bias,
                     (lam_rows, subln_b[0].reshape(1, LANES)), batch=batch, seq=seq,
                     q_col=0, k_col=0, v_col=n_hp, lam_init=lam_init)
    moe_out, h = _moe_layer(att.reshape(t, d), h, w_o_b[0], 1, *moe)
    out = _ple_last(moe_out, h, p[1].reshape(t, PLE_DIM), ple_norm[1], w_ple_gate[1].astype(BF16),
                    w_ple_proj[1].astype(BF16), final_norm)
    return out.reshape(batch, seq, d)
```

```python
import functools
import math

import jax
import jax.numpy as jnp
from jax import lax
from jax.experimental import pallas as pl
from jax.experimental.pallas import tpu as pltpu

F32 = jnp.float32
BF16 = jnp.bfloat16

D_MODEL = 1024
DEPTH = 2
N_A_LAYERS = DEPTH // 2
HEAD_DIM = 64
LANES = 128
MOBA_BLOCK = 256
ATT_TQ = 2 * MOBA_BLOCK
MOBA_TOP_K = 3
REL_BUCKETS = 32
REL_MAX_DISTANCE = 128
MOE_GROUPS = 4
MOE_EXPERTS_PER_GROUP = 8
MOE_EXPERTS = MOE_GROUPS * MOE_EXPERTS_PER_GROUP
MOE_D_FF = D_MODEL // 4
PLE_DIM = 256
RMS_EPS = 1e-6
NEG = -1e30
LOG2E = math.log2(math.e)

ROW_TILE = 512
GROUP_TILE = 512
SCATTER_TILE = 1024
GATHER_CHUNKS = 8
VMEM_LIMIT = 52 * 1024 * 1024


def _dot(a, b):
    return jnp.dot(a, b, preferred_element_type=F32)


def _rms(x, gain):
    y = x * lax.rsqrt(jnp.mean(x * x, axis=-1, keepdims=True) + RMS_EPS)
    return y * gain


def _params(n_axes):
    return pltpu.CompilerParams(dimension_semantics=("arbitrary",) * n_axes,
                                vmem_limit_bytes=VMEM_LIMIT)


def _full(shape):
    nd = len(shape)
    return pl.BlockSpec(shape, lambda *_: (0,) * nd)


def _qkv_kernel(x_ref, g_ref, w_ref, qkv_ref, kmean_ref):
    hn = _rms(x_ref[...], g_ref[...]).astype(BF16)
    for c in range(3):
        y = _dot(hn, w_ref[:, c * D_MODEL:(c + 1) * D_MODEL])
        qkv_ref[:, c * D_MODEL:(c + 1) * D_MODEL] = y.astype(BF16)
        if c == 1:
            nb = y.shape[0] // MOBA_BLOCK
            kmean_ref[...] = jnp.mean(y.reshape(nb, MOBA_BLOCK, D_MODEL), axis=1, keepdims=True)


def _qkv_proj(x2, gain, w_bf16):
    t = x2.shape[0]
    tm = ROW_TILE
    return pl.pallas_call(
        _qkv_kernel,
        grid=(t // tm,),
        in_specs=[pl.BlockSpec((tm, D_MODEL), lambda i: (i, 0)),
                  _full((1, D_MODEL)),
                  _full((D_MODEL, 3 * D_MODEL))],
        out_specs=[pl.BlockSpec((tm, 3 * D_MODEL), lambda i: (i, 0)),
                   pl.BlockSpec((tm // MOBA_BLOCK, 1, D_MODEL), lambda i: (i, 0, 0))],
        out_shape=[jax.ShapeDtypeStruct((t, 3 * D_MODEL), BF16),
                   jax.ShapeDtypeStruct((t // MOBA_BLOCK, 1, D_MODEL), F32)],
        compiler_params=_params(1),
        name="qkv_proj",
    )(x2, gain.reshape(1, D_MODEL), w_bf16)


def _rel_bucket(dist):
    n = jnp.maximum(dist, 0)
    max_exact = REL_BUCKETS // 2
    nf = jnp.maximum(n, max_exact).astype(F32)
    large = max_exact + (jnp.log(nf / max_exact) / math.log(REL_MAX_DISTANCE / max_exact)
                         * (REL_BUCKETS - max_exact)).astype(jnp.int32)
    large = jnp.minimum(large, REL_BUCKETS - 1)
    return jnp.where(n < max_exact, n, large)


def _shifted_bias(tab_ref, col, dist):
    bkt = _rel_bucket(dist)
    acc = jnp.zeros(dist.shape, F32)
    for i in range(REL_BUCKETS):
        acc = jnp.where(bkt == i, tab_ref[i, col], acc)
    return (acc - tab_ref[REL_BUCKETS - 1, col]) * LOG2E


def _build_bias(tab_ref, bias_ref, corner_ref, col0):
    tb = MOBA_BLOCK
    rows = 64
    for r0 in range(0, tb, rows):
        key = lax.broadcasted_iota(jnp.int32, (rows, tb), 0) + r0
        qry = lax.broadcasted_iota(jnp.int32, (rows, tb), 1)
        dist = qry - key
        for h in range(2):
            bias_ref[h, r0:r0 + rows, :] = jnp.where(dist >= 0, _shifted_bias(tab_ref, col0 + h, dist), NEG)
    key = lax.broadcasted_iota(jnp.int32, (REL_MAX_DISTANCE, REL_MAX_DISTANCE), 0) - REL_MAX_DISTANCE
    qry = lax.broadcasted_iota(jnp.int32, (REL_MAX_DISTANCE, REL_MAX_DISTANCE), 1)
    for h in range(2):
        corner_ref[h] = _shifted_bias(tab_ref, col0 + h, qry - key)


def _build_kv(k_ref, v_ref, ka_ref, vt_ref, seq, mask_v):
    tb = MOBA_BLOCK
    nb = seq // tb
    lane = lax.broadcasted_iota(jnp.int32, (tb, LANES), 1)
    first = lane < HEAD_DIM

    def body(j, _):
        r = pl.multiple_of(j * tb, tb)
        k = k_ref[0, pl.ds(r, tb), :].astype(F32)
        oh0 = (lane == HEAD_DIM + j).astype(F32)
        oh1 = (lane == j).astype(F32)
        ka_ref[0, pl.ds(r, tb), :] = jnp.where(first, k, oh0).astype(BF16)
        ka_ref[1, pl.ds(r, tb), :] = jnp.where(first, oh1, k).astype(BF16)
        v_t = v_ref[0, pl.ds(r, tb), :].astype(F32).T
        if mask_v:
            none = jnp.zeros((HEAD_DIM, tb), F32)
            vt_ref[j] = jnp.concatenate([v_t[0:HEAD_DIM], none], axis=0).astype(BF16)
            vt_ref[nb + j] = jnp.concatenate([none, v_t[HEAD_DIM:]], axis=0).astype(BF16)
        else:
            vt_ref[j] = v_t.astype(BF16)
        return 0

    lax.fori_loop(0, nb, body, 0)


def _block_penalties(gate, own):
    n = lax.broadcasted_iota(jnp.int32, (16, own.shape[1]), 0)
    pen = jnp.where(n == own, 0.0, NEG)
    if gate is None:
        return jnp.where(n < own, 0.0, pen)
    g = jnp.where(n < own, gate, -jnp.inf)
    for _ in range(MOBA_TOP_K):
        mx = jnp.max(g, axis=0, keepdims=True)
        idx = jnp.min(jnp.where(g == mx, n, 1 << 20), axis=0, keepdims=True)
        pick = (n == idx) & (mx > -jnp.inf)
        pen = jnp.where(pick, 0.0, pen)
        g = jnp.where(pick, -jnp.inf, g)
    return pen


def _own_block(qi):
    col = lax.broadcasted_iota(jnp.int32, (1, ATT_TQ), 1)
    return qi * (ATT_TQ // MOBA_BLOCK) + col // MOBA_BLOCK


def _augment_queries(qs_t, pens):
    nq = qs_t.shape[1]
    out = []
    for h in range(2):
        tail = [pens[h], jnp.zeros((HEAD_DIM - 16, nq), F32)]
        parts = [qs_t[0:HEAD_DIM]] + tail if h == 0 else tail + [qs_t[HEAD_DIM:]]
        out.append(jnp.concatenate(parts, axis=0).astype(BF16))
    return out


def _flash(qa, ka_ref, vt_ref, bias_ref, corner_ref, qi, v_base, bufs):
    tb = MOBA_BLOCK
    near = REL_MAX_DISTANCE
    s_a, s_b, m_ref, l_ref, acc_ref = bufs

    def scores_into(buf, j, own=False):
        r = pl.multiple_of((qi - j) * ATT_TQ, ATT_TQ)
        prev = jnp.where(j == 1, 1.0, 0.0).astype(F32)
        for h in range(2):
            s = _dot(ka_ref[h, pl.ds(r, ATT_TQ), :], qa[h])
            buf[h] = s
            if own:
                for lo in (0, tb):
                    buf[h, lo:lo + tb, lo:lo + tb] = s[lo:lo + tb, lo:lo + tb] + bias_ref[h]
                buf[h, tb - near:tb, tb:tb + near] = s[tb - near:tb, tb:tb + near] + corner_ref[h]
            else:
                buf[h, ATT_TQ - near:ATT_TQ, 0:near] = s[ATT_TQ - near:, 0:near] + prev * corner_ref[h]

    def update(buf, j):
        blk0 = (qi - j) * (ATT_TQ // tb)
        for h in range(2):
            s = buf[h]
            m = m_ref[h]
            m_new = jnp.maximum(m, jnp.max(s, axis=0, keepdims=True))
            alpha = jnp.exp2(m - m_new)
            p = jnp.exp2(s - m_new)
            pb = p.astype(BF16)
            pv = _dot(vt_ref[v_base[h] + blk0], pb[0:tb]) + _dot(vt_ref[v_base[h] + blk0 + 1], pb[tb:])
            m_ref[h] = m_new
            l_ref[h] = alpha * l_ref[h] + jnp.sum(p, axis=0, keepdims=True)
            acc_ref[h] = alpha * acc_ref[h] + pv

    m_ref[...] = jnp.full(m_ref.shape, 3.0 * NEG, F32)
    l_ref[...] = jnp.zeros(l_ref.shape, F32)
    acc_ref[...] = jnp.zeros(acc_ref.shape, F32)
    scores_into(s_a, 0, own=True)

    def body(t, carry):
        scores_into(s_b, 2 * t + 1)
        update(s_a, 2 * t)
        scores_into(s_a, 2 * t + 2)
        update(s_b, 2 * t + 1)
        return carry

    lax.fori_loop(0, lax.shift_right_logical(qi, 1), body, 0)
    odd = (qi & 1) == 1

    @pl.when(odd)
    def _():
        scores_into(s_b, qi)
        update(s_a, qi - 1)
        update(s_b, qi)

    @pl.when(jnp.logical_not(odd))
    def _():
        update(s_a, qi)

    return [(acc_ref[h], l_ref[h]) for h in range(2)]


def _moba_kernel(tab_ref, q_ref, k_ref, v_ref, km_ref, o_ref, ka_ref, vt_ref, bias_ref, corner_ref, *bufs, seq):
    hp, b, qi = pl.program_id(0), pl.program_id(1), pl.program_id(2)
    nb = seq // MOBA_BLOCK

    @pl.when((b == 0) & (qi == 0))
    def _():
        _build_bias(tab_ref, bias_ref, corner_ref, 2 * hp)

    @pl.when(qi == 0)
    def _():
        _build_kv(k_ref, v_ref, ka_ref, vt_ref, seq, mask_v=True)

    q_t = q_ref[0].astype(F32).T
    km = km_ref[:, 0, :]
    lane_k = lax.broadcasted_iota(jnp.int32, (nb, LANES), 1)
    km0 = jnp.where(lane_k < HEAD_DIM, km, 0.0)
    km1 = jnp.where(lane_k < HEAD_DIM, 0.0, km)
    pad = jnp.zeros((HEAD_DIM - nb, LANES), F32)
    kmx = jnp.concatenate([km1, pad, km0, pad], axis=0)
    kmx_hi = kmx.astype(BF16)
    kmx_lo = (kmx - kmx_hi.astype(F32)).astype(BF16)
    q_bf = q_t.astype(BF16)
    gate = _dot(kmx_hi, q_bf) + _dot(kmx_lo, q_bf)
    pens = [_block_penalties(gate[base:base + 16], _own_block(qi)) for base in (HEAD_DIM, 0)]

    qa = _augment_queries(q_t * (HEAD_DIM ** -0.5 * LOG2E), pens)
    (a0, l0), (a1, l1) = _flash(qa, ka_ref, vt_ref, bias_ref, corner_ref, qi, (0, nb), bufs)
    o_ref[0] = (a0 / l0 + a1 / l1).T.astype(o_ref.dtype)


def _diff_kernel(tab_ref, lam_ref, q_ref, k_ref, v_ref, sg_ref, o_ref, ka_ref, vt_ref, bias_ref, corner_ref, *bufs,
                 seq, lam_init):
    hd, b, qi = pl.program_id(0), pl.program_id(1), pl.program_id(2)

    @pl.when((b == 0) & (qi == 0))
    def _():
        _build_bias(tab_ref, bias_ref, corner_ref, 2 * hd)

    @pl.when(qi == 0)
    def _():
        _build_kv(k_ref, v_ref, ka_ref, vt_ref, seq, mask_v=False)

    lv = lam_ref[...]
    lam = (jnp.exp(jnp.sum(lv[0:1] * lv[1:2], axis=1, keepdims=True))
           - jnp.exp(jnp.sum(lv[2:3] * lv[3:4], axis=1, keepdims=True)) + lam_init)

    q_t = q_ref[0].astype(F32).T
    qa = _augment_queries(q_t * (HEAD_DIM ** -0.5 * LOG2E), [_block_penalties(None, _own_block(qi))] * 2)
    (a0, l0), (a1, l1) = _flash(qa, ka_ref, vt_ref, bias_ref, corner_ref, qi, (0, 0), bufs)
    att = a0 / l0 - lam * (a1 / l1)
    y = att * lax.rsqrt(jnp.mean(att * att, axis=0, keepdims=True) + RMS_EPS)
    o_ref[0] = ((y.T * sg_ref[...]) * (1.0 - lam_init)).astype(o_ref.dtype)


def _attention(kind, q_src, kv_src, rel_bias, extra, *, batch, seq, q_col, k_col, v_col, lam_init=None):
    tq = ATT_TQ
    nb = seq // MOBA_BLOCK
    n_hp = D_MODEL // LANES
    grid = (n_hp, batch, seq // tq)
    smem = pl.BlockSpec(memory_space=pltpu.SMEM)
    q_spec = pl.BlockSpec((1, tq, LANES), lambda h, b, i: (b, i, q_col + h))
    k_spec = pl.BlockSpec((1, seq, LANES), lambda h, b, i: (b, 0, k_col + h))
    v_spec = pl.BlockSpec((1, seq, LANES), lambda h, b, i: (b, 0, v_col + h))
    o_spec = pl.BlockSpec((1, tq, LANES), lambda h, b, i: (b, i, h))
    n_vt = 2 * nb if kind == "moba" else nb
    scratch = [pltpu.VMEM((2, seq, LANES), BF16),
               pltpu.VMEM((n_vt, LANES, MOBA_BLOCK), BF16),
               pltpu.VMEM((2, MOBA_BLOCK, MOBA_BLOCK), F32),
               pltpu.VMEM((2, REL_MAX_DISTANCE, REL_MAX_DISTANCE), F32),
               pltpu.VMEM((2, tq, tq), F32), pltpu.VMEM((2, tq, tq), F32),
               pltpu.VMEM((2, 1, tq), F32), pltpu.VMEM((2, 1, tq), F32), pltpu.VMEM((2, LANES, tq), F32)]
    out_shape = jax.ShapeDtypeStruct((batch, seq, D_MODEL), BF16)
    if kind == "moba":
        kmean = extra
        km_spec = pl.BlockSpec((nb, 1, LANES), lambda h, b, i: (b, 0, h))
        return pl.pallas_call(
            functools.partial(_moba_kernel, seq=seq),
            grid=grid,
            in_specs=[smem, q_spec, k_spec, v_spec, km_spec],
            out_specs=o_spec, out_shape=out_shape, scratch_shapes=scratch,
            compiler_params=_params(3), name="moba_attention",
        )(rel_bias, q_src, kv_src, kv_src, kmean)
    lam_rows, sub_gain = extra
    return pl.pallas_call(
        functools.partial(_diff_kernel, seq=seq, lam_init=lam_init),
        grid=grid,
        in_specs=[smem, _full((8, LANES)), q_spec, k_spec, v_spec, _full((1, LANES))],
        out_specs=o_spec, out_shape=out_shape, scratch_shapes=scratch,
        compiler_params=_params(3), name="diff_attention",
    )(rel_bias, lam_rows, q_src, kv_src, kv_src, sub_gain)


def _router_logits(xn, w_hi_ref, w_lo_ref, b_ref):
    x_hi = xn.astype(BF16)
    x_lo = (xn - x_hi.astype(F32)).astype(BF16)
    both = _dot(x_hi, jnp.concatenate([w_hi_ref[...], w_lo_ref[...]], axis=1))
    return (both[:, :LANES] + (both[:, LANES:] + _dot(x_lo, w_hi_ref[...]))) + b_ref[...]


def _route(logits, tri_ref, count_ref, counted):
    lane = lax.broadcasted_iota(jnp.int32, logits.shape, 1)
    live = lane < MOE_EXPERTS
    gl = jnp.where(live, logits, -jnp.inf)
    el = pltpu.roll(logits, LANES // 2, axis=1)
    gmax = jnp.max(gl, axis=1, keepdims=True)
    gsum = jnp.sum(jnp.exp(gl - gmax), axis=1, keepdims=True) / MOE_EXPERTS_PER_GROUP
    g_gate = 1.0 / gsum
    lane_f = lane.astype(F32)
    first_lane = lambda hit: jnp.min(jnp.where(hit, lane_f, float(LANES)), axis=1, keepdims=True)
    group_of = lambda lanes: jnp.floor(lanes * (1.0 / MOE_EXPERTS_PER_GROUP))
    in_group = group_of(lane_f) == group_of(first_lane(gl == gmax))
    em = jnp.where(in_group & live, el, -jnp.inf)
    m1 = jnp.max(em, axis=1, keepdims=True)
    i1 = first_lane(em == m1)
    em2 = jnp.where(lane_f == i1, -jnp.inf, em)
    m2 = jnp.max(em2, axis=1, keepdims=True)
    i2 = first_lane(em2 == m2)
    p2 = jnp.exp(m2 - m1)
    w1 = g_gate * (1.0 / (1.0 + p2))
    w2 = g_gate * (p2 / (1.0 + p2))

    is1, is2 = lane_f == i1, lane_f == i2
    member = jnp.where(is1 | is2, counted, 0.0)
    before = _dot(tri_ref[...], member.astype(BF16)) + count_ref[...]
    rank1 = jnp.sum(jnp.where(is1, before, 0.0), axis=1, keepdims=True)
    rank2 = jnp.sum(jnp.where(is2, before, 0.0), axis=1, keepdims=True)
    count_ref[...] = count_ref[...] + jnp.sum(member, axis=0, keepdims=True)
    fields = (i1, i2, w1, w2, rank1, rank2)
    route = jnp.zeros(lane.shape, F32)
    for k, col in enumerate(fields):
        route = jnp.where(lane == k, col, route)
    return route


def _attn_out_kernel(a_ref, h_ref, wo_ref, fg_ref, w_hi_ref, w_lo_ref, b_ref, tri_ref,
                     h1_ref, xn_ref, route_ref, counts_ref, count_acc, logit_buf):
    i = pl.program_id(0)

    @pl.when(i == 0)
    def _():
        count_acc[...] = jnp.zeros_like(count_acc)
        logit_buf[...] = jnp.zeros_like(logit_buf)

    counted = jnp.where(i > 0, 1.0, 0.0).astype(F32)

    def step(mine, other):
        h1 = h_ref[...] + _dot(a_ref[...], wo_ref[...])
        h1_ref[...] = h1
        xn = _rms(h1, fg_ref[...])
        xn_ref[...] = xn
        logit_buf[mine] = _router_logits(xn, w_hi_ref, w_lo_ref, b_ref)
        route_ref[...] = _route(logit_buf[other], tri_ref, count_acc, counted)
        counts_ref[...] = jnp.broadcast_to(count_acc[...], counts_ref.shape)

    @pl.when((i & 1) == 0)
    def _():
        step(0, 1)

    @pl.when((i & 1) == 1)
    def _():
        step(1, 0)


def _attn_out(att2, h2, wo_bf16, ffn_gain, w_hi, w_lo, b_x):
    t = h2.shape[0]
    tm = ROW_TILE
    n = t // tm
    row = lambda w: pl.BlockSpec((tm, w), lambda i: (jnp.minimum(i, n - 1), 0))
    tri = jnp.tril(jnp.ones((tm, tm), F32), -1).astype(BF16)
    return pl.pallas_call(
        _attn_out_kernel,
        grid=(n + 1,),
        in_specs=[row(D_MODEL), row(D_MODEL), _full((D_MODEL, D_MODEL)), _full((1, D_MODEL)),
                  _full((D_MODEL, LANES)), _full((D_MODEL, LANES)), _full((1, LANES)), _full((tm, tm))],
        out_specs=[row(D_MODEL), row(D_MODEL), pl.BlockSpec((tm, LANES), lambda i: (jnp.maximum(i - 1, 0), 0)),
                   _full((8, LANES))],
        out_shape=[jax.ShapeDtypeStruct((t, D_MODEL), F32), jax.ShapeDtypeStruct((t, D_MODEL), F32),
                   jax.ShapeDtypeStruct((t, LANES), F32), jax.ShapeDtypeStruct((8, LANES), F32)],
        scratch_shapes=[pltpu.VMEM((1, LANES), F32), pltpu.VMEM((2, tm, LANES), F32)],
        compiler_params=_params(1), name="attn_out_router",
    )(att2, h2, wo_bf16, ffn_gain.reshape(1, D_MODEL), w_hi, w_lo, b_x, tri)


def _scatter_kernel(last_row_ref, tiles_ref, used_ref, pos1_ref, pos2_ref, xn_ref, out_hbm, zeros_ref, sems):
    tm = pos1_ref.shape[2]
    row_sem, zero_sem = sems.at[0], sems.at[1]
    n_tiles = out_hbm.shape[0] // GROUP_TILE

    def zero_tile(row):
        return pltpu.make_async_copy(zeros_ref, out_hbm.at[pl.ds(pl.multiple_of(row, GROUP_TILE), GROUP_TILE)], zero_sem)

    def spare_tiles(fn):
        def body(w, carry):
            fn(zero_tile(w * GROUP_TILE))
            return carry

        lax.fori_loop(used_ref[0], n_tiles, body, 0)

    @pl.when(pl.program_id(0) == 0)
    def _():
        zeros_ref[...] = jnp.zeros_like(zeros_ref)
        for e in range(MOE_EXPERTS):
            @pl.when(tiles_ref[e] > 0)
            def _():
                zero_tile(last_row_ref[e]).start()

        for e in range(MOE_EXPERTS):
            @pl.when(tiles_ref[e] > 0)
            def _():
                zero_tile(last_row_ref[e]).wait()

        spare_tiles(lambda cp: cp.start())

    def issue(r, carry):
        src = xn_ref.at[pl.ds(r, 1)]
        pltpu.make_async_copy(src, out_hbm.at[pl.ds(pos1_ref[0, 0, r], 1)], row_sem).start(priority=0)
        pltpu.make_async_copy(src, out_hbm.at[pl.ds(pos2_ref[0, 0, r], 1)], row_sem).start(priority=1)
        return carry

    lax.fori_loop(0, tm, issue, 0, unroll=8)
    for _ in range(2):
        pltpu.make_async_copy(xn_ref, out_hbm.at[pl.ds(0, tm)], row_sem).wait()

    @pl.when(pl.program_id(0) == pl.num_programs(0) - 1)
    def _():
        spare_tiles(lambda cp: cp.wait())


def _scatter_rows(xn, pos1, pos2, last_row, tiles_per, n_used, n_rows):
    t = xn.shape[0]
    tm = SCATTER_TILE
    idx = lambda: pl.BlockSpec((1, 1, tm), lambda i, *_: (i, 0, 0), memory_space=pltpu.SMEM)
    return pl.pallas_call(
        _scatter_kernel,
        grid_spec=pltpu.PrefetchScalarGridSpec(
            num_scalar_prefetch=3, grid=(t // tm,),
            in_specs=[idx(), idx(), pl.BlockSpec((tm, D_MODEL), lambda i, *_: (i, 0))],
            out_specs=pl.BlockSpec(memory_space=pl.ANY),
            scratch_shapes=[pltpu.VMEM((GROUP_TILE, D_MODEL), F32), pltpu.SemaphoreType.DMA((2,))]),
        out_shape=jax.ShapeDtypeStruct((n_rows, D_MODEL), F32),
        compiler_params=_params(1), name="moe_scatter",
    )(last_row, tiles_per, n_used, pos1.reshape(t // tm, 1, tm), pos2.reshape(t // tm, 1, tm), xn)


def _moe_group_kernel(te_ref, x_ref, wg_ref, wu_ref, wd_ref, y_ref):
    del te_ref
    x = x_ref[...].astype(BF16)
    g = _dot(x, wg_ref[0])
    u = _dot(x, wu_ref[0])
    hh = (g * jax.nn.sigmoid(g)) * u
    y_ref[...] = _dot(hh.astype(BF16), wd_ref[0])


def _moe_group(x_sorted, tile_expert, wg, wu, wd):
    n_tiles = tile_expert.shape[0]
    tmg = GROUP_TILE
    wspec = lambda a, b: pl.BlockSpec((1, a, b), lambda w, te: (te[w], 0, 0))
    rows = lambda: pl.BlockSpec((tmg, D_MODEL), lambda w, te: (w, 0))
    return pl.pallas_call(
        _moe_group_kernel,
        grid_spec=pltpu.PrefetchScalarGridSpec(
            num_scalar_prefetch=1, grid=(n_tiles,),
            in_specs=[rows(), wspec(D_MODEL, MOE_D_FF), wspec(D_MODEL, MOE_D_FF), wspec(MOE_D_FF, D_MODEL)],
            out_specs=rows()),
        out_shape=jax.ShapeDtypeStruct((n_tiles * tmg, D_MODEL), F32),
        compiler_params=_params(1), name="moe_group_ffn",
    )(tile_expert, x_sorted, wg, wu, wd)


def _routing_tables(route, counts8, t):
    n_tiles = 2 * t // GROUP_TILE + MOE_EXPERTS
    counts = counts8[0, :MOE_EXPERTS].astype(jnp.int32)
    tiles_per = (counts + GROUP_TILE - 1) // GROUP_TILE
    tile_end = jnp.cumsum(tiles_per)
    offsets = (tile_end - tiles_per) * GROUP_TILE
    last_row = jnp.maximum(tile_end - 1, 0) * GROUP_TILE
    ids = route[:, 0:2].astype(jnp.int32)
    ranks = route[:, 4:6].astype(jnp.int32)
    expert_ids = jnp.arange(MOE_EXPERTS, dtype=jnp.int32)
    pos = jnp.sum(jnp.where(ids[:, :, None] == expert_ids, offsets, 0), axis=2) + ranks
    tile_ids = jnp.arange(n_tiles, dtype=jnp.int32)
    tile_expert = jnp.minimum(jnp.sum((tile_end[None, :] <= tile_ids[:, None]).astype(jnp.int32), axis=1),
                              MOE_EXPERTS - 1)
    return pos[:, 0], pos[:, 1], tile_expert, tile_end[-1:], last_row, tiles_per, n_tiles


def _gather_copy(pos_ref, y_hbm, ybuf, sem, slot, k, r):
    return pltpu.make_async_copy(y_hbm.at[pl.ds(pos_ref[0, 0, r], 1)], ybuf.at[slot, k, pl.ds(r, 1)], sem.at[slot])


def _wait_tile(y_hbm, ybuf, sem, slot):
    tm = ybuf.shape[2]
    for k in range(2):
        pltpu.make_async_copy(y_hbm.at[pl.ds(0, tm)], ybuf.at[slot, k], sem.at[slot]).wait()


def _moe_ple_update(refs, tail):
    (pos1_ref, pos2_ref, nxt1_ref, nxt2_ref, y_hbm, route_ref, h_ref, p_ref, pg_ref, wgate_ref, wproj_ref,
     ybuf, sem) = refs
    i = pl.program_id(0)
    slot = i & 1
    nslot = 1 - slot
    tm = ybuf.shape[2]

    @pl.when(i == 0)
    def _():
        def issue(r, carry):
            _gather_copy(pos1_ref, y_hbm, ybuf, sem, 0, 0, r).start(priority=0)
            _gather_copy(pos2_ref, y_hbm, ybuf, sem, 0, 1, r).start(priority=1)
            return carry

        lax.fori_loop(0, tm, issue, 0)

    _wait_tile(y_hbm, ybuf, sem, slot)
    rows_per = tm // GATHER_CHUNKS

    def prefetch(c):
        for r in range(c * rows_per, (c + 1) * rows_per):
            _gather_copy(nxt1_ref, y_hbm, ybuf, sem, nslot, 0, r).start(priority=0)
            _gather_copy(nxt2_ref, y_hbm, ybuf, sem, nslot, 1, r).start(priority=1)

    proj = _dot(p_ref[...].astype(BF16), wproj_ref[...])
    prefetch(0)
    route = route_ref[...]
    h = h_ref[...] + (route[:, 2:3] * ybuf[slot, 0] + route[:, 3:4] * ybuf[slot, 1])
    prefetch(1)
    gate = jax.nn.sigmoid(_dot(_rms(h, pg_ref[...]).astype(BF16), wgate_ref[...]))
    prefetch(2)
    tail(h + gate * proj, prefetch)

    @pl.when(i == pl.num_programs(0) - 1)
    def _():
        _wait_tile(y_hbm, ybuf, sem, nslot)


def _ple_mid_kernel(*refs):
    kvg_ref, wkv_ref, qg_ref, wq_ref, h3_ref, kv_ref, q_ref = refs[11:18]

    def tail(h3, prefetch):
        h3_ref[...] = h3
        prefetch(3)
        kn = _rms(h3, kvg_ref[...]).astype(BF16)
        for c in range(2):
            kv_ref[:, c * D_MODEL:(c + 1) * D_MODEL] = _dot(kn, wkv_ref[:, c * D_MODEL:(c + 1) * D_MODEL]).astype(BF16)
            prefetch(4 + c)
        q_ref[...] = _dot(_rms(h3, qg_ref[...]).astype(BF16), wq_ref[...]).astype(BF16)
        prefetch(6)
        prefetch(7)

    _moe_ple_update(refs[:11] + refs[18:], tail)


def _ple_last_kernel(*refs):
    fg_ref, o_ref = refs[11:13]

    def tail(h3, prefetch):
        o_ref[...] = _rms(h3, fg_ref[...])
        for c in range(3, GATHER_CHUNKS):
            prefetch(c)

    _moe_ple_update(refs[:11] + refs[13:], tail)


def _ple_call(body, name, moe_in, h1, p2, vecs_and_weights, in_tail, out_specs, out_shape):
    pos1, pos2, y_sorted, route = moe_in
    t = h1.shape[0]
    tm = ROW_TILE
    n = t // tm
    row = lambda w: pl.BlockSpec((tm, w), lambda i: (i, 0))
    idx = lambda: pl.BlockSpec((1, 1, tm), lambda i: (i, 0, 0), memory_space=pltpu.SMEM)
    nxt = lambda: pl.BlockSpec((1, 1, tm), lambda i: (jnp.minimum(i + 1, n - 1), 0, 0), memory_space=pltpu.SMEM)
    pos1, pos2 = pos1.reshape(n, 1, tm), pos2.reshape(n, 1, tm)
    return pl.pallas_call(
        body,
        grid=(n,),
        in_specs=[idx(), idx(), nxt(), nxt(), pl.BlockSpec(memory_space=pl.ANY), row(LANES), row(D_MODEL),
                  row(PLE_DIM)] + in_tail,
        out_specs=out_specs, out_shape=out_shape,
        scratch_shapes=[pltpu.VMEM((2, 2, tm, D_MODEL), F32), pltpu.SemaphoreType.DMA((2,))],
        compiler_params=_params(1), name=name,
    )(pos1, pos2, pos1, pos2, y_sorted, route, h1, p2, *vecs_and_weights)


def _ple_mid(moe_in, h1, p2, ple_gain, wgate, wproj, kv_gain, wkv, q_gain, wq):
    t = h1.shape[0]
    tm = ROW_TILE
    row = lambda w: pl.BlockSpec((tm, w), lambda i: (i, 0))
    vec = _full((1, D_MODEL))
    return _ple_call(
        _ple_mid_kernel, "moe_combine_ple_kv_q", moe_in, h1, p2,
        (ple_gain.reshape(1, -1), wgate, wproj, kv_gain.reshape(1, -1), wkv, q_gain.reshape(1, -1), wq),
        [vec, _full((D_MODEL, D_MODEL)), _full((PLE_DIM, D_MODEL)), vec, _full((D_MODEL, 2 * D_MODEL)), vec,
         _full((D_MODEL, D_MODEL))],
        [row(D_MODEL), row(2 * D_MODEL), row(D_MODEL)],
        [jax.ShapeDtypeStruct((t, D_MODEL), F32), jax.ShapeDtypeStruct((t, 2 * D_MODEL), BF16),
         jax.ShapeDtypeStruct((t, D_MODEL), BF16)])


def _ple_last(moe_in, h1, p2, ple_gain, wgate, wproj, final_gain):
    t = h1.shape[0]
    tm = ROW_TILE
    vec = _full((1, D_MODEL))
    return _ple_call(
        _ple_last_kernel, "moe_combine_ple_final_norm", moe_in, h1, p2,
        (ple_gain.reshape(1, -1), wgate, wproj, final_gain.reshape(1, -1)),
        [vec, _full((D_MODEL, D_MODEL)), _full((PLE_DIM, D_MODEL)), vec],
        pl.BlockSpec((tm, D_MODEL), lambda i: (i, 0)),
        jax.ShapeDtypeStruct((t, D_MODEL), F32))


def _router_operands(w_group, b_group, w_router, b_router):
    gap = LANES // 2 - MOE_EXPERTS
    w = jnp.pad(jnp.concatenate([jnp.repeat(w_group, MOE_EXPERTS_PER_GROUP, axis=1),
                                 jnp.zeros((D_MODEL, gap), F32), w_router], axis=1), ((0, 0), (0, gap)))
    b = jnp.pad(jnp.concatenate([jnp.repeat(b_group, MOE_EXPERTS_PER_GROUP), jnp.zeros((gap,), F32), b_router]),
                (0, gap)).reshape(1, LANES)
    w_hi = w.astype(BF16)
    w_lo = (w - w_hi.astype(F32)).astype(BF16)
    return w_hi, w_lo, b


def _moe_layer(att, h, wo, i, ffn_norm, w_group, b_group, w_router, b_router, w_gate, w_up, w_down):
    t = h.shape[0]
    h1, xn, route, counts8 = _attn_out(att, h, wo.astype(BF16), ffn_norm[i],
                                       *_router_operands(w_group[i], b_group[i], w_router[i], b_router[i]))
    pos1, pos2, tile_expert, n_used, last_row, tiles_per, n_tiles = _routing_tables(route, counts8, t)
    x_sorted = _scatter_rows(xn, pos1, pos2, last_row, tiles_per, n_used, n_tiles * GROUP_TILE)
    wg = w_gate[i].reshape(MOE_EXPERTS, D_MODEL, MOE_D_FF).astype(BF16)
    wu = w_up[i].reshape(MOE_EXPERTS, D_MODEL, MOE_D_FF).astype(BF16)
    wd = w_down[i].reshape(MOE_EXPERTS, MOE_D_FF, D_MODEL).astype(BF16)
    y_sorted = _moe_group(x_sorted, tile_expert, wg, wu, wd)
    return (pos1, pos2, y_sorted, route), h1


def kernel(x, p, rel_bias, attn_norm_a, w_qkv_a, w_o_a, kv_norm, w_kv, attn_norm_b, w_q_b, lambda_q1, lambda_k1,
           lambda_q2, lambda_k2, subln_b, w_o_b, ffn_norm, w_group, b_group, w_router, b_router, w_gate, w_up,
           w_down, ple_norm, w_ple_gate, w_ple_proj, final_norm):
    batch, seq, d = x.shape
    assert d == D_MODEL and seq % ATT_TQ == 0 and seq // MOBA_BLOCK <= 16
    t = batch * seq
    assert t % ROW_TILE == 0 and t % SCATTER_TILE == 0
    n_hp = D_MODEL // LANES
    moe = (ffn_norm, w_group, b_group, w_router, b_router, w_gate, w_up, w_down)

    h = x.reshape(t, d)
    qkv, kmean = _qkv_proj(h, attn_norm_a[0], w_qkv_a[0].astype(BF16))
    qkv3 = qkv.reshape(batch, seq, 3 * d)
    att = _attention("moba", qkv3, qkv3, rel_bias, kmean, batch=batch, seq=seq,
                     q_col=0, k_col=n_hp, v_col=2 * n_hp)
    moe_out, h = _moe_layer(att.reshape(t, d), h, w_o_a[0], 0, *moe)
    h, kv, q = _ple_mid(moe_out, h, p[0].reshape(t, PLE_DIM), ple_norm[0], w_ple_gate[0].astype(BF16),
                        w_ple_proj[0].astype(BF16), kv_norm, w_kv.astype(BF16), attn_norm_b[0],
                        w_q_b[0].astype(BF16))

    lam_init = 0.8 - 0.6 * math.exp(-0.3 * 1)
    lam_rows = jnp.pad(jnp.stack([lambda_q1[0], lambda_k1[0], lambda_q2[0], lambda_k2[0]]).astype(F32),
                       ((0, 4), (0, LANES - HEAD_DIM)))
    att = _attention("diff", q.reshape(batch, seq, d), kv.reshape(batch, seq, 2 * d), rel_bias,
                     (lam_rows, subln_b[0].reshape(1, LANES)), batch=batch, seq=seq,
                     q_col=0, k_col=0, v_col=n_hp, lam_init=lam_init)
    moe_out, h = _moe_layer(att.reshape(t, d), h, w_o_b[0], 1, *moe)
    out = _ple_last(moe_out, h, p[1].reshape(t, PLE_DIM), ple_norm[1], w_ple_gate[1].astype(BF16),
                    w_ple_proj[1].astype(BF16), final_norm)
    return out.reshape(batch, seq, d)
```
